```python
import math
import numpy as np
import jax
import jax.numpy as jnp
from jax import lax

D_MODEL = 2048
BATCH = 8
SEQ = 4096
DEPTH = 4

HEAD_DIM = 128
D_MIX = D_MODEL
N_HEADS_TOTAL = D_MIX // HEAD_DIM
N_HEADS_B = N_HEADS_TOTAL // 4
N_HEADS_C = (3 * N_HEADS_TOTAL) // 8
N_HEADS_A = N_HEADS_TOTAL - N_HEADS_B - N_HEADS_C
CONV_WIDTH = 4
GDN_CHUNK = 64
SGU_CHUNK = 128
NSA_KV_HEADS = 2
CMP_LEN = 32
CMP_STRIDE = 16
SEL_LEN = 64
SEL_TOPK = 16
SEL_QBLOCK = 32
WINDOW = 512
WIN_BLOCK = 128
RPB_BUCKETS = 32
RPB_MAX_DIST = 128
D_FF = 4 * D_MODEL
NORM_EPS = 1e-6
NEG_BIG = -1e30
SEL_FORCE = 1e9

kernel_name = 'hybrid_gdn_sgu_nsa_trunk'


def _proj_sizes():
    da = N_HEADS_A * HEAD_DIM
    db = N_HEADS_B * HEAD_DIM
    dc = N_HEADS_C * HEAD_DIM
    dkv = NSA_KV_HEADS * HEAD_DIM
    return [da, da, da, da, N_HEADS_A, N_HEADS_A, db, db, dc, dkv, dkv, dkv, dkv, dkv, dkv, 3 * N_HEADS_C]


def _split_points():
    return [int(v) for v in np.cumsum(_proj_sizes())[:-1]]


def _rms_norm(x, gain):
    xf = x.astype(jnp.float32)
    y = xf * lax.rsqrt(jnp.mean(xf * xf, axis=-1, keepdims=True) + NORM_EPS)
    return (y * gain.astype(jnp.float32)).astype(x.dtype)


def _layer_norm(x, gain, bias):
    xf = x.astype(jnp.float32)
    mu = jnp.mean(xf, axis=-1, keepdims=True)
    var = jnp.mean(jnp.square(xf - mu), axis=-1, keepdims=True)
    y = (xf - mu) * lax.rsqrt(var + NORM_EPS) * gain.astype(jnp.float32) + bias.astype(jnp.float32)
    return y.astype(x.dtype)


def _l2norm(x):
    return x * lax.rsqrt(jnp.sum(x * x, axis=-1, keepdims=True) + NORM_EPS)


def _masked_softmax(s, mask):
    s = jnp.where(mask, s.astype(jnp.float32), NEG_BIG)
    m = jnp.max(s, axis=-1, keepdims=True)
    p = jnp.where(mask, jnp.exp(s - m), 0.0)
    return p / jnp.maximum(jnp.sum(p, axis=-1, keepdims=True), 1e-30)


def _t5_bucket(dist):
    n = jnp.maximum(dist, 0)
    max_exact = RPB_BUCKETS // 2
    log_ratio = jnp.log(jnp.maximum(n, 1).astype(jnp.float32) / max_exact) / math.log(RPB_MAX_DIST / max_exact)
    large = jnp.minimum(max_exact + (log_ratio * (RPB_BUCKETS - max_exact)).astype(jnp.int32), RPB_BUCKETS - 1)
    return jnp.where(n < max_exact, n, large)


def _rel_bias(dist, table):
    b = table[_t5_bucket(dist)]
    return jnp.moveaxis(b, -1, 0).reshape(NSA_KV_HEADS, N_HEADS_C // NSA_KV_HEADS, *dist.shape)


def _causal_short_conv(x, w):
    k = w.shape[0]
    s_ = x.shape[1]
    xp = jnp.pad(x, ((0, 0), (k - 1, 0), (0, 0)))
    y = xp[:, 0:s_] * w[0]
    for j in range(1, k):
        y = y + xp[:, j:j + s_] * w[j]
    return y


def _gated_delta_rule(q, k, v, g, beta):
    b_, h_, s_, dk = q.shape
    dv = v.shape[-1]
    c = GDN_CHUNK
    n = s_ // c
    q = q * (dk ** -0.5)
    q, k, v = (t.reshape(b_, h_, n, c, t.shape[-1]) for t in (q, k, v))
    g = jnp.cumsum(g.reshape(b_, h_, n, c), axis=-1)
    beta = beta.reshape(b_, h_, n, c)
    incl = jnp.tril(jnp.ones((c, c), dtype=bool))
    strict = jnp.tril(jnp.ones((c, c), dtype=bool), -1)
    decay = jnp.exp(jnp.where(incl, g[..., :, None] - g[..., None, :], -jnp.inf))
    k_beta = k * beta[..., None]
    v_beta = v * beta[..., None]
    lower = jnp.where(strict, jnp.einsum('bhnik,bhnjk->bhnij', k_beta, k) * decay, 0.0)
    eye = jnp.eye(c, dtype=q.dtype)
    t_inv = lax.linalg.triangular_solve(lower + eye, jnp.broadcast_to(eye, lower.shape),
                                        left_side=True, lower=True, unit_diagonal=True)
    u = t_inv @ v_beta
    w = t_inv @ (k_beta * jnp.exp(g)[..., None])
    a_intra = jnp.where(incl, jnp.einsum('bhnik,bhnjk->bhnij', q, k) * decay, 0.0)

    def step(state, xs):
        q_c, k_c, u_c, w_c, g_c, a_c = xs
        v_new = u_c - w_c @ state
        o_c = (q_c * jnp.exp(g_c)[..., None]) @ state + a_c @ v_new
        g_last = g_c[..., -1:]
        state = state * jnp.exp(g_last)[..., None] + jnp.einsum(
            'bhck,bhcv->bhkv', k_c * jnp.exp(g_last - g_c)[..., None], v_new)
        return state, o_c

    xs = tuple(jnp.moveaxis(t, 2, 0) for t in (q, k, u, w, g, a_intra))
    state0 = jnp.zeros((b_, h_, dk, dv), q.dtype)
    _, o = lax.scan(step, state0, xs)
    return jnp.moveaxis(o, 0, 2).reshape(b_, h_, s_, dv)


def _mixer_gdn(q, k, v, z, b_raw, a_raw, conv_w, a_log, dt_bias, norm_g):
    out_dtype = q.dtype
    bsz, s_, _ = q.shape
    h_, d_ = N_HEADS_A, HEAD_DIM
    qkv = jax.nn.silu(_causal_short_conv(jnp.concatenate([q, k, v], axis=-1), conv_w))
    q, k, v = jnp.split(qkv.astype(jnp.float32), 3, axis=-1)
    heads = lambda t: t.reshape(bsz, s_, h_, d_).transpose(0, 2, 1, 3)
    q, k, v = _l2norm(heads(q)), _l2norm(heads(k)), heads(v)
    beta = jax.nn.sigmoid(b_raw.astype(jnp.float32)).transpose(0, 2, 1)
    g = (-jnp.exp(a_log.astype(jnp.float32))
         * jax.nn.softplus(a_raw.astype(jnp.float32) + dt_bias.astype(jnp.float32))).transpose(0, 2, 1)
    o = _gated_delta_rule(q, k, v, g, beta).transpose(0, 2, 1, 3)
    o = _rms_norm(o, norm_g) * jax.nn.silu(z.astype(jnp.float32).reshape(bsz, s_, h_, d_))
    return o.reshape(bsz, s_, h_ * d_).astype(out_dtype)


def _mixer_sgu(u, v, ln_g, ln_b, w_s, b_s):
    bsz, s_, _ = u.shape
    n = s_ // SGU_CHUNK
    u = jax.nn.gelu(u)
    v = _layer_norm(jax.nn.gelu(v), ln_g, ln_b)
    v = v.reshape(bsz, n, SGU_CHUNK, N_HEADS_B, HEAD_DIM)
    causal = jnp.tril(jnp.ones((SGU_CHUNK, SGU_CHUNK), dtype=bool))
    w = jnp.where(causal, w_s, 0.0).astype(v.dtype)
    mixed = jnp.einsum('gts,bnsgc->bntgc', w, v) + b_s.T[None, None, :, :, None].astype(v.dtype)
    return u * mixed.reshape(bsz, s_, N_HEADS_B * HEAD_DIM)


def _mixer_nsa(q, k_cmp, v_cmp, k_slc, v_slc, k_win, v_win, gate_raw,
               q_norm_g, k_norm_g, cmp_pos, cmp_w1, cmp_w2, rel_bias):
    bsz, s_, _ = q.shape
    g_, h_, d_ = NSA_KV_HEADS, N_HEADS_C, HEAD_DIM
    r_ = h_ // g_
    pos = jnp.arange(s_, dtype=jnp.int32)
    q = _rms_norm(q.reshape(bsz, s_, g_, r_, d_), q_norm_g).transpose(0, 2, 3, 1, 4) * (d_ ** -0.5)
    kv_heads = lambda t: t.reshape(bsz, s_, g_, d_).transpose(0, 2, 1, 3)

    n_cmp = (s_ - CMP_LEN) // CMP_STRIDE + 1
    cmp_start = np.arange(n_cmp) * CMP_STRIDE
    blk_idx = cmp_start[:, None] + np.arange(CMP_LEN)[None, :]

    def compress(t, p, w1, w2):
        blocks = t[:, :, blk_idx] + p
        return jax.nn.gelu(blocks.reshape(bsz, g_, n_cmp, CMP_LEN * d_) @ w1) @ w2

    kc = _rms_norm(compress(kv_heads(k_cmp), cmp_pos[0], cmp_w1[0], cmp_w2[0]), k_norm_g)
    vc = compress(kv_heads(v_cmp), cmp_pos[1], cmp_w1[1], cmp_w2[1])
    cmp_end = jnp.asarray(cmp_start + CMP_LEN - 1, dtype=jnp.int32)
    dist_c = pos[:, None] - cmp_end[None, :]
    s_c = jnp.einsum('bgrsd,bgnd->bgrsn', q, kc) + _rel_bias(dist_c, rel_bias)
    p_c = _masked_softmax(s_c, dist_c >= 0)
    o_c = jnp.einsum('bgrsn,bgnd->bgrsd', p_c.astype(vc.dtype), vc)

    n_sel = s_ // SEL_LEN
    sel_start = np.arange(n_sel) * SEL_LEN
    overlap = (cmp_start[:, None] < sel_start[None, :] + SEL_LEN) & (cmp_start[:, None] + CMP_LEN > sel_start[None, :])
    importance = jnp.einsum('bgrsn,nj->bgsj', p_c, jnp.asarray(overlap, jnp.float32))
    cur = pos // SEL_LEN
    jsel = jnp.arange(n_sel, dtype=jnp.int32)
    causal_blk = jsel[None, :] <= cur[:, None]
    forced = (jsel[None, :] == 0) | (jsel[None, :] == cur[:, None]) | (jsel[None, :] == cur[:, None] - 1)
    score = jnp.where(forced, SEL_FORCE, jnp.where(causal_blk, importance, NEG_BIG))
    n_top = min(SEL_TOPK, n_sel)
    top_score, top_idx = lax.top_k(score, n_top)
    top_ok = top_score > 0.5 * NEG_BIG

    ks = _rms_norm(kv_heads(k_slc), k_norm_g).reshape(bsz, g_, n_sel, SEL_LEN, d_)
    vs = kv_heads(v_slc).reshape(bsz, g_, n_sel, SEL_LEN, d_)
    qb_len = SEL_QBLOCK
    n_qb = s_ // qb_len
    gather = jax.vmap(jax.vmap(lambda blocks, idx: blocks[idx]))
    table_g = rel_bias.reshape(RPB_BUCKETS, g_, r_).transpose(1, 0, 2)
    g_ar = jnp.arange(g_)[None, :, None, None, None]

    def sel_block(args):
        qb, idxb, okb, i = args
        kg = gather(ks, idxb)
        vg = gather(vs, idxb)
        qpos = i * qb_len + jnp.arange(qb_len, dtype=jnp.int32)
        kpos = idxb[..., None] * SEL_LEN + jnp.arange(SEL_LEN, dtype=jnp.int32)
        dist = qpos[None, None, :, None, None] - kpos
        mask = okb[..., None] & (dist >= 0)
        bias = table_g[g_ar, _t5_bucket(dist)].transpose(0, 1, 5, 2, 3, 4)
        s = jnp.einsum('bgrqd,bgqnld->bgrqnl', qb, kg) + bias
        p = _masked_softmax(s.reshape(bsz, g_, r_, qb_len, n_top * SEL_LEN),
                            mask.reshape(bsz, g_, 1, qb_len, n_top * SEL_LEN))
        return jnp.einsum('bgrqm,bgqmd->bgrqd', p.astype(vg.dtype),
                          vg.reshape(bsz, g_, qb_len, n_top * SEL_LEN, d_))

    xs = (jnp.moveaxis(q.reshape(bsz, g_, r_, n_qb, qb_len, d_), 3, 0),
          jnp.moveaxis(top_idx.reshape(bsz, g_, n_qb, qb_len, n_top), 2, 0),
          jnp.moveaxis(top_ok.reshape(bsz, g_, n_qb, qb_len, n_top), 2, 0),
          jnp.arange(n_qb, dtype=jnp.int32))
    o_s = lax.map(sel_block, xs)
    o_s = jnp.moveaxis(o_s, 0, 3).reshape(bsz, g_, r_, s_, d_)

    n_prev = WINDOW // WIN_BLOCK
    n_wb = s_ // WIN_BLOCK
    band_len = (n_prev + 1) * WIN_BLOCK

    def band(t):
        tp = jnp.pad(t, ((0, 0), (0, 0), (n_prev * WIN_BLOCK, 0), (0, 0))).reshape(bsz, g_, n_wb + n_prev, WIN_BLOCK, d_)
        return jnp.concatenate([tp[:, :, j:j + n_wb] for j in range(n_prev + 1)], axis=3)

    kw = band(_rms_norm(kv_heads(k_win), k_norm_g))
    vw = band(kv_heads(v_win))
    qpos_w = pos.reshape(n_wb, WIN_BLOCK)
    kpos_w = (jnp.arange(n_wb, dtype=jnp.int32)[:, None] - n_prev) * WIN_BLOCK + jnp.arange(band_len, dtype=jnp.int32)[None, :]
    dist_w = qpos_w[:, :, None] - kpos_w[:, None, :]
    mask_w = (dist_w >= 0) & (dist_w < WINDOW) & (kpos_w[:, None, :] >= 0)
    s_w = jnp.einsum('bgrnqd,bgnkd->bgrnqk', q.reshape(bsz, g_, r_, n_wb, WIN_BLOCK, d_), kw) + _rel_bias(dist_w, rel_bias)
    p_w = _masked_softmax(s_w, mask_w)
    o_w = jnp.einsum('bgrnqk,bgnkd->bgrnqd', p_w.astype(vw.dtype), vw).reshape(bsz, g_, r_, s_, d_)

    gates = jax.nn.sigmoid(gate_raw.astype(jnp.float32)).reshape(bsz, s_, 3, g_, r_)
    gates = gates.transpose(2, 0, 3, 4, 1)[..., None].astype(o_c.dtype)
    o = gates[0] * o_c + gates[1] * o_s + gates[2] * o_w
    return o.transpose(0, 3, 1, 2, 4).reshape(bsz, s_, h_ * d_)


def setup_inputs(seed: int = 0) -> dict:
    key = jax.random.key(seed)
    ks = jax.random.split(key, 21)
    f32 = jnp.float32
    nrm = lambda k, shape, scale: scale * jax.random.normal(k, shape, f32)
    gain = lambda k, shape: 1.0 + 0.02 * jax.random.normal(k, shape, f32)
    d_proj = sum(_proj_sizes())
    x = jax.random.normal(ks[0], (BATCH, SEQ, D_MODEL), f32)
    attn_norm = gain(ks[1], (DEPTH, D_MODEL))
    w_in = nrm(ks[2], (DEPTH, D_MODEL, d_proj), D_MODEL ** -0.5)
    conv_a = nrm(ks[3], (DEPTH, CONV_WIDTH, 3 * N_HEADS_A * HEAD_DIM), CONV_WIDTH ** -0.5)
    a_log = jnp.log(jax.random.uniform(ks[4], (DEPTH, N_HEADS_A), f32, 1.0, 16.0))
    dt = jnp.exp(jax.random.uniform(ks[5], (DEPTH, N_HEADS_A), f32, math.log(1e-3), math.log(1e-1)))
    dt_bias = dt + jnp.log(-jnp.expm1(-dt))
    gdn_norm = gain(ks[6], (DEPTH, HEAD_DIM))
    sgu_ln_g = gain(ks[7], (DEPTH, N_HEADS_B * HEAD_DIM))
    sgu_ln_b = nrm(ks[8], (DEPTH, N_HEADS_B * HEAD_DIM), 0.02)
    sgu_w = nrm(ks[9], (DEPTH, N_HEADS_B, SGU_CHUNK, SGU_CHUNK), SGU_CHUNK ** -0.5)
    sgu_b = gain(ks[10], (DEPTH, N_HEADS_B, SGU_CHUNK))
    nsa_q_norm = gain(ks[11], (DEPTH, HEAD_DIM))
    nsa_k_norm = gain(ks[12], (DEPTH, HEAD_DIM))
    cmp_pos = nrm(ks[13], (DEPTH, 2, CMP_LEN, HEAD_DIM), 0.02)
    cmp_w1 = nrm(ks[14], (DEPTH, 2, CMP_LEN * HEAD_DIM, HEAD_DIM), (CMP_LEN * HEAD_DIM) ** -0.5)
    cmp_w2 = nrm(ks[15], (DEPTH, 2, HEAD_DIM, HEAD_DIM), HEAD_DIM ** -0.5)
    rel_bias = nrm(ks[16], (RPB_BUCKETS, N_HEADS_C), 0.2)
    w_out = nrm(ks[17], (DEPTH, D_MIX, D_MODEL), D_MIX ** -0.5)
    mlp_norm = gain(ks[18], (DEPTH, D_MODEL))
    w_up = nrm(ks[19], (DEPTH, D_MODEL, D_FF), D_MODEL ** -0.5)
    w_down = nrm(ks[20], (DEPTH, D_FF, D_MODEL), D_FF ** -0.5)
    return {'x': x, 'attn_norm': attn_norm, 'w_in': w_in, 'conv_a': conv_a, 'a_log': a_log,
            'dt_bias': dt_bias, 'gdn_norm': gdn_norm, 'sgu_ln_g': sgu_ln_g, 'sgu_ln_b': sgu_ln_b,
            'sgu_w': sgu_w, 'sgu_b': sgu_b, 'nsa_q_norm': nsa_q_norm, 'nsa_k_norm': nsa_k_norm,
            'cmp_pos': cmp_pos, 'cmp_w1': cmp_w1, 'cmp_w2': cmp_w2, 'rel_bias': rel_bias,
            'w_out': w_out, 'mlp_norm': mlp_norm, 'w_up': w_up, 'w_down': w_down}


def reference(x, attn_norm, w_in, conv_a, a_log, dt_bias, gdn_norm, sgu_ln_g, sgu_ln_b, sgu_w, sgu_b,
              nsa_q_norm, nsa_k_norm, cmp_pos, cmp_w1, cmp_w2, rel_bias, w_out, mlp_norm, w_up, w_down):
    for l in range(DEPTH):
        h = _rms_norm(x, attn_norm[l])
        (qa, ka, va, za, ba, aa, ub, vb, qc, kcc, vcc, ksl, vsl, kwn, vwn, gc) = jnp.split(
            h @ w_in[l], _split_points(), axis=-1)
        mix = jnp.concatenate([
            _mixer_gdn(qa, ka, va, za, ba, aa, conv_a[l], a_log[l], dt_bias[l], gdn_norm[l]),
            _mixer_sgu(ub, vb, sgu_ln_g[l], sgu_ln_b[l], sgu_w[l], sgu_b[l]),
            _mixer_nsa(qc, kcc, vcc, ksl, vsl, kwn, vwn, gc, nsa_q_norm[l], nsa_k_norm[l],
                       cmp_pos[l], cmp_w1[l], cmp_w2[l], rel_bias),
        ], axis=-1)
        x = x + mix @ w_out[l]
        h = _rms_norm(x, mlp_norm[l])
        x = x + jnp.square(jax.nn.relu(h @ w_up[l])) @ w_down[l]
    return x
```

```python
import functools
import math

import numpy as np
import jax
import jax.numpy as jnp
from jax import lax
from jax.experimental import pallas as pl
from jax.experimental.pallas import tpu as pltpu

F32 = jnp.float32
BF16 = jnp.bfloat16

LANES = 128
HEAD_DIM = 128
N_HEADS_A = 6
N_HEADS_B = 4
N_HEADS_C = 6
KV_HEADS = 2
Q_PER_KV = N_HEADS_C // KV_HEADS
CONV_WIDTH = 4
GDN_CHUNK = 64
SGU_CHUNK = 128
CMP_LEN = 32
CMP_STRIDE = 16
SEL_LEN = 64
SEL_TOPK = 16
WINDOW = 512
WIN_BLOCK = 128
N_WIN_PREV = WINDOW // WIN_BLOCK
RPB_BUCKETS = 32
RPB_MAX_DIST = 128
NORM_EPS = 1e-6
NEG_BIG = -1e30
SEL_FORCE = 1e9
VMEM_LIMIT = 56 * 1024 * 1024

G_QA, G_KA, G_VA, G_ZA = 0, 6, 12, 18
G_QC = 24
G_KCMP = 30
G_UB, G_VB = 32, 36
G_VCMP = 40
G_KSLC, G_VSLC, G_KWIN, G_VWIN = 42, 44, 46, 48
G_SMALL = 50
N_GROUPS = 52
LANE_BETA, LANE_DECAY, LANE_GATE = 0, 6, 12


def _dot(a, b, precision=None):
    return jnp.dot(a, b, preferred_element_type=F32, precision=precision)


def _dot_nt(a, b):
    return lax.dot_general(a, b, (((1,), (1,)), ((), ())), preferred_element_type=F32)


def _dot_tn(a, b):
    return lax.dot_general(a, b, (((0,), (0,)), ((), ())), preferred_element_type=F32)


def _rms(x, gain):
    return x * lax.rsqrt(jnp.mean(x * x, axis=-1, keepdims=True) + NORM_EPS) * gain


def _cparams(sem):
    return pltpu.CompilerParams(dimension_semantics=sem, vmem_limit_bytes=VMEM_LIMIT)


def _inproj_kernel(x_ref, g_ref, w_ref, o_ref, h_ref):
    @pl.when(pl.program_id(1) == 0)
    def _():
        h_ref[...] = _rms(x_ref[...], g_ref[...]).astype(BF16)

    r = _dot(h_ref[...], w_ref[...])
    for c in range(o_ref.shape[0]):
        o_ref[c] = r[:, c * LANES:(c + 1) * LANES]


def _inproj(x2, gain, w, layer, tm=512, tn=13 * LANES):
    t, d = x2.shape
    n = w.shape[-1]
    return pl.pallas_call(
        _inproj_kernel,
        out_shape=jax.ShapeDtypeStruct((n // LANES, t, LANES), F32),
        grid=(t // tm, n // tn),
        in_specs=[pl.BlockSpec((tm, d), lambda i, j: (i, 0)),
                  pl.BlockSpec((None, 1, d), lambda i, j: (layer, 0, 0)),
                  pl.BlockSpec((None, d, tn), lambda i, j: (layer, 0, j))],
        out_specs=pl.BlockSpec((tn // LANES, tm, LANES), lambda i, j: (j, i, 0)),
        scratch_shapes=[pltpu.VMEM((tm, d), BF16)],
        compiler_params=_cparams(("parallel", "arbitrary")),
        name="inproj",
    )(x2, gain, w)


def _outproj_kernel(x_ref, a_ref, b_ref, c_ref, w_ref, o_ref):
    da, db = a_ref.shape[1], b_ref.shape[1]
    acc = _dot(a_ref[...], w_ref[0:da, :])
    acc += _dot(b_ref[...], w_ref[da:da + db, :])
    acc += _dot(c_ref[...], w_ref[da + db:, :])
    o_ref[...] = x_ref[...] + acc


def _outproj(x2, ma, mb, mc, w_out, layer, tm=512):
    t, d = x2.shape
    da, db, dc = ma.shape[1], mb.shape[1], mc.shape[1]
    return pl.pallas_call(
        _outproj_kernel,
        out_shape=jax.ShapeDtypeStruct((t, d), F32),
        grid=(t // tm,),
        in_specs=[pl.BlockSpec((tm, d), lambda i: (i, 0)),
                  pl.BlockSpec((tm, da), lambda i: (i, 0)),
                  pl.BlockSpec((tm, db), lambda i: (i, 0)),
                  pl.BlockSpec((tm, dc), lambda i: (i, 0)),
                  pl.BlockSpec((None, da + db + dc, d), lambda i: (layer, 0, 0))],
        out_specs=pl.BlockSpec((tm, d), lambda i: (i, 0)),
        compiler_params=_cparams(("parallel",)),
        name="outproj",
    )(x2, ma, mb, mc, w_out)


def _mlp_kernel(x_ref, g_ref, wu_ref, wd_ref, o_ref, h_ref, acc_ref):
    k = pl.program_id(1)

    @pl.when(k == 0)
    def _():
        h_ref[...] = _rms(x_ref[...], g_ref[...]).astype(BF16)
        acc_ref[...] = jnp.zeros_like(acc_ref)

    a = jnp.maximum(_dot(h_ref[...], wu_ref[...]), 0.0)
    acc_ref[...] += _dot((a * a).astype(BF16), wd_ref[...])

    @pl.when(k == pl.num_programs(1) - 1)
    def _():
        o_ref[...] = x_ref[...] + acc_ref[...]


def _mlp(x2, gain, w_up, w_down, layer, tm=512, tf=1024):
    t, d = x2.shape
    f = w_up.shape[-1]
    return pl.pallas_call(
        _mlp_kernel,
        out_shape=jax.ShapeDtypeStruct((t, d), F32),
        grid=(t // tm, f // tf),
        in_specs=[pl.BlockSpec((tm, d), lambda i, k: (i, 0)),
                  pl.BlockSpec((None, 1, d), lambda i, k: (layer, 0, 0)),
                  pl.BlockSpec((None, d, tf), lambda i, k: (layer, 0, k)),
                  pl.BlockSpec((None, tf, d), lambda i, k: (layer, k, 0))],
        out_specs=pl.BlockSpec((tm, d), lambda i, k: (i, 0)),
        scratch_shapes=[pltpu.VMEM((tm, d), BF16), pltpu.VMEM((tm, d), F32)],
        compiler_params=_cparams(("parallel", "arbitrary")),
        name="mlp",
    )(x2, gain, w_up, w_down)


def _sgu_kernel(p_ref, lg_ref, lb_ref, w_ref, b_ref, o_ref):
    ts = p_ref.shape[1]
    nb = N_HEADS_B
    v = [jax.nn.gelu(p_ref[nb + g]) for g in range(nb)]
    width = float(nb * LANES)
    mu = sum(jnp.sum(vg, axis=-1, keepdims=True) for vg in v) / width
    var = sum(jnp.sum(jnp.square(vg - mu), axis=-1, keepdims=True) for vg in v) / width
    inv = lax.rsqrt(var + NORM_EPS)
    row = lax.broadcasted_iota(jnp.int32, (SGU_CHUNK, SGU_CHUNK), 0)
    col = lax.broadcasted_iota(jnp.int32, (SGU_CHUNK, SGU_CHUNK), 1)
    causal = col <= row
    for g in range(nb):
        vn = ((v[g] - mu) * inv * lg_ref[g:g + 1, :] + lb_ref[g:g + 1, :]).astype(BF16)
        wg = jnp.where(causal, w_ref[g], 0.0).astype(BF16)
        for c in range(ts // SGU_CHUNK):
            rows = slice(c * SGU_CHUNK, (c + 1) * SGU_CHUNK)
            mixed = _dot(wg, vn[rows]) + b_ref[g]
            u = jax.nn.gelu(p_ref[g, rows, :])
            o_ref[rows, g * LANES:(g + 1) * LANES] = (u * mixed).astype(BF16)


def _sgu(p4, ln_g, ln_b, w_s, b_s, layer, ts=512):
    _, b, s, _ = p4.shape
    nb = N_HEADS_B
    return pl.pallas_call(
        _sgu_kernel,
        out_shape=jax.ShapeDtypeStruct((b, s, nb * LANES), BF16),
        grid=(b, s // ts),
        in_specs=[pl.BlockSpec((2 * nb, None, ts, LANES), lambda i, j: (G_UB // (2 * nb), i, j, 0)),
                  pl.BlockSpec((None, nb, LANES), lambda i, j: (layer, 0, 0)),
                  pl.BlockSpec((None, nb, LANES), lambda i, j: (layer, 0, 0)),
                  pl.BlockSpec((None, nb, SGU_CHUNK, SGU_CHUNK), lambda i, j: (layer, 0, 0, 0)),
                  pl.BlockSpec((None, nb, SGU_CHUNK, LANES), lambda i, j: (layer, 0, 0, 0))],
        out_specs=pl.BlockSpec((None, ts, nb * LANES), lambda i, j: (i, j, 0)),
        compiler_params=_cparams(("parallel", "parallel")),
        name="sgu",
    )(p4, ln_g, ln_b, w_s, b_s)


def _unit_lower_inverse_minus_eye(lower, idx_i, idx_j):
    c = lower.shape[0]
    bf = lambda m: m.astype(BF16)
    base = 8
    d = jnp.where((idx_i // base) == (idx_j // base), lower, 0.0)
    db = bf(d)
    p1 = _dot(db, db)
    p1b = bf(p1)
    p2 = _dot(p1b, p1b)
    na = p1 - d - _dot(db, p1b)
    n = na + p2 + _dot(bf(na), bf(p2))
    size = base
    while size < c:
        off = ((idx_i // (2 * size)) == (idx_j // (2 * size))) & ((idx_i // size) != (idx_j // size))
        cm = jnp.where(off, lower, 0.0)
        y = cm + _dot(bf(n), bf(cm))
        n = n - (y + _dot(bf(y), bf(n)))
        size *= 2
    return n


def _gdn_kernel(a_ref, sm_ref, cw_ref, ab_ref, ng_ref, o_ref,
                xbuf, q_s, k_s, v_s, beta_s, gcum_s, st_ref):
    ts = a_ref.shape[1]
    ck = GDN_CHUNK
    nh = N_HEADS_A
    pad = 8

    @pl.when(pl.program_id(1) == 0)
    def _():
        xbuf[:, 0:pad, :] = jnp.zeros((3 * nh, pad, LANES), F32)
        st_ref[...] = jnp.zeros_like(st_ref)

    xbuf[:, pad:pad + ts, :] = a_ref[0:3 * nh]
    dests = (q_s, k_s, v_s)
    for grp in range(3 * nh):
        acc = None
        for j in range(CONV_WIDTH):
            lo = pad - (CONV_WIDTH - 1) + j
            term = xbuf[grp, lo:lo + ts, :] * cw_ref[j * 3 * nh + grp:j * 3 * nh + grp + 1, :]
            acc = term if acc is None else acc + term
        y = acc * jax.nn.sigmoid(acc)
        kind, h = divmod(grp, nh)
        if kind < 2:
            y = y * lax.rsqrt(jnp.sum(y * y, axis=-1, keepdims=True) + NORM_EPS)
        dests[kind][h] = y
    xbuf[:, 0:pad, :] = xbuf[:, ts:ts + pad, :]

    sm = sm_ref[...]
    beta_s[...] = jax.nn.sigmoid(sm)
    z = sm + ab_ref[1:2, :]
    softplus = jnp.maximum(z, 0.0) + jnp.log1p(jnp.exp(-jnp.abs(z)))
    g_all = -jnp.exp(ab_ref[0:1, :]) * softplus

    ii = lax.broadcasted_iota(jnp.int32, (ck, ck), 0)
    jj = lax.broadcasted_iota(jnp.int32, (ck, ck), 1)
    incl = jj <= ii
    strict = jj < ii
    tri = jnp.where(incl, 1.0, 0.0).astype(F32)
    for c in range(ts // ck):
        rows = slice(c * ck, (c + 1) * ck)
        gcum_s[rows, :] = _dot(tri, g_all[rows], precision=lax.Precision.HIGHEST)

    scale = HEAD_DIM ** -0.5
    gain = ng_ref[...]

    def chunk(c, carry):
        r0 = pl.multiple_of(c * ck, ck)
        rows = pl.ds(r0, ck)
        gc = gcum_s[rows, :]
        bt = beta_s[rows, :]
        gct = gc.T
        for h in range(nh):
            g_col = gc[:, LANE_DECAY + h:LANE_DECAY + h + 1]
            g_row = gct[LANE_DECAY + h:LANE_DECAY + h + 1, :]
            g_last = gc[ck - 1:ck, LANE_DECAY + h:LANE_DECAY + h + 1]
            beta = bt[:, LANE_BETA + h:LANE_BETA + h + 1]
            decay = jnp.exp(jnp.where(incl, g_col - g_row, NEG_BIG))
            q = q_s[h, rows, :] * scale
            k = k_s[h, rows, :]
            v = v_s[h, rows, :]
            eg = jnp.exp(g_col)
            kb = k * beta
            vb = v * beta
            kbf = k.astype(BF16)
            lower = jnp.where(strict, _dot_nt(kb.astype(BF16), kbf) * decay, 0.0)
            n = _unit_lower_inverse_minus_eye(lower, ii, jj).astype(BF16)
            u = vb + _dot(n, vb.astype(BF16))
            kw = kb * eg
            w = kw + _dot(n, kw.astype(BF16))
            a_intra = jnp.where(incl, _dot_nt(q.astype(BF16), kbf) * decay, 0.0)
            state = st_ref[h]
            sb = state.astype(BF16)
            v_new = u - _dot(w.astype(BF16), sb)
            vnb = v_new.astype(BF16)
            o = _dot((q * eg).astype(BF16), sb) + _dot(a_intra.astype(BF16), vnb)
            kd = k * jnp.exp(g_last - g_col)
            st_ref[h] = state * jnp.exp(g_last) + _dot_tn(kd.astype(BF16), vnb)
            zg = a_ref[3 * nh + h, rows, :]
            out = _rms(o, gain) * (zg * jax.nn.sigmoid(zg))
            o_ref[rows, h * LANES:(h + 1) * LANES] = out.astype(BF16)
        return carry

    lax.fori_loop(0, ts // ck, chunk, 0)


def _gdn(p4, conv_w, ab, norm_g, layer, ts=512):
    _, b, s, _ = p4.shape
    nh = N_HEADS_A
    return pl.pallas_call(
        _gdn_kernel,
        out_shape=jax.ShapeDtypeStruct((b, s, nh * LANES), BF16),
        grid=(b, s // ts),
        in_specs=[pl.BlockSpec((4 * nh, None, ts, LANES), lambda i, j: (0, i, j, 0)),
                  pl.BlockSpec((None, None, ts, LANES), lambda i, j: (G_SMALL, i, j, 0)),
                  pl.BlockSpec((None, CONV_WIDTH * 3 * nh, LANES), lambda i, j: (layer, 0, 0)),
                  pl.BlockSpec((None, 2, LANES), lambda i, j: (layer, 0, 0)),
                  pl.BlockSpec((None, 1, LANES), lambda i, j: (layer, 0, 0))],
        out_specs=pl.BlockSpec((None, ts, nh * LANES), lambda i, j: (i, j, 0)),
        scratch_shapes=[pltpu.VMEM((3 * nh, ts + 8, LANES), F32),
                        pltpu.VMEM((nh, ts, LANES), F32),
                        pltpu.VMEM((nh, ts, LANES), F32),
                        pltpu.VMEM((nh, ts, LANES), F32),
                        pltpu.VMEM((ts, LANES), F32),
                        pltpu.VMEM((ts, LANES), F32),
                        pltpu.VMEM((nh, HEAD_DIM, HEAD_DIM), F32)],
        compiler_params=_cparams(("parallel", "arbitrary")),
        name="gdn",
    )(p4, p4, conv_w, ab, norm_g)


def _nsa_prep_kernel(kc2_ref, vc2_ref, ksl_ref, vsl_ref, kwn_ref, vwn_ref,
                     w1_ref, w2_ref, pos_ref, kg_ref,
                     kc_o, vc_o, ks_o, vs_o, kw_o, vw_o):
    nc = kc2_ref.shape[0]
    half = kc2_ref.shape[1]
    kg = kg_ref[...]
    for kind, (src, dst) in enumerate(((kc2_ref, kc_o), (vc2_ref, vc_o))):
        t2 = src[...].astype(BF16)
        first = _dot(t2, w1_ref[kind, 0:half, :])
        second = _dot(t2, w1_ref[kind, half:2 * half, :])
        const = _dot(pos_ref[kind], w1_ref[kind])[0:1, :]
        pre = first + pltpu.roll(second, nc - 1, 0) + const
        y = _dot(jax.nn.gelu(pre).astype(BF16), w2_ref[kind])
        if kind == 0:
            y = _rms(y, kg)
        dst[...] = y.astype(BF16)
    ks_o[...] = _rms(ksl_ref[...], kg).astype(BF16)
    kw_o[...] = _rms(kwn_ref[...], kg).astype(BF16)
    vs_o[...] = vsl_ref[...].astype(BF16)
    vw_o[...] = vwn_ref[...].astype(BF16)


def _nsa_prep(p4, w1, w2, pos, k_norm, layer):
    ng, b, s, _ = p4.shape
    nc = s // CMP_STRIDE
    half = CMP_STRIDE * LANES
    p16 = p4.reshape(ng, b, nc, half)
    grp = lambda base: (lambda i, g: (base + g, i, 0, 0))
    small = jax.ShapeDtypeStruct((b, KV_HEADS, nc, LANES), BF16)
    full = jax.ShapeDtypeStruct((b, KV_HEADS, s, LANES), BF16)
    out_small = pl.BlockSpec((None, None, nc, LANES), lambda i, g: (i, g, 0, 0))
    out_full = pl.BlockSpec((None, None, s, LANES), lambda i, g: (i, g, 0, 0))
    return pl.pallas_call(
        _nsa_prep_kernel,
        out_shape=(small, small, full, full, full, full),
        grid=(b, KV_HEADS),
        in_specs=[pl.BlockSpec((None, None, nc, half), grp(G_KCMP)),
                  pl.BlockSpec((None, None, nc, half), grp(G_VCMP)),
                  pl.BlockSpec((None, None, s, LANES), grp(G_KSLC)),
                  pl.BlockSpec((None, None, s, LANES), grp(G_VSLC)),
                  pl.BlockSpec((None, None, s, LANES), grp(G_KWIN)),
                  pl.BlockSpec((None, None, s, LANES), grp(G_VWIN)),
                  pl.BlockSpec((None, 2, 2 * half, LANES), lambda i, g: (layer, 0, 0, 0)),
                  pl.BlockSpec((None, 2, LANES, LANES), lambda i, g: (layer, 0, 0, 0)),
                  pl.BlockSpec((None, 2, 8, 2 * half), lambda i, g: (layer, 0, 0, 0)),
                  pl.BlockSpec((None, 1, LANES), lambda i, g: (layer, 0, 0))],
        out_specs=(out_small, out_small, out_full, out_full, out_full, out_full),
        compiler_params=_cparams(("parallel", "parallel")),
        name="nsa_prep",
    )(p16, p16, p4, p4, p4, p4, w1, w2, pos, k_norm)


def _nsa_kernel(q_ref, gate_ref, kc_ref, vc_ref, ks_ref, vs_ref, kw_ref, vw_ref,
                bc_ref, bsel_ref, bw_ref, ov_ref, e_ref, qg_ref, o_ref, sc_ref, *, n_cmp, n_sel):
    tq = q_ref.shape[1]
    r3 = Q_PER_KV
    i = pl.program_id(2)
    nc = kc_ref.shape[0]
    rows = r3 * tq

    qg = qg_ref[...]
    qf = jnp.concatenate([_rms(q_ref[r], qg) * (HEAD_DIM ** -0.5) for r in range(r3)], axis=0)
    qb = qf.astype(BF16)

    qq = lax.broadcasted_iota(jnp.int32, (tq, tq), 0)
    kk = lax.broadcasted_iota(jnp.int32, (tq, tq), 1)

    s_c = _dot_nt(qb, kc_ref[...]).reshape(r3, tq, nc) + bc_ref[...]
    qpos_c = i * tq + lax.broadcasted_iota(jnp.int32, (tq, nc), 0)
    n_idx = lax.broadcasted_iota(jnp.int32, (tq, nc), 1)
    mask_c = ((qpos_c >= n_idx * CMP_STRIDE + (CMP_LEN - 1)) & (n_idx < n_cmp))[None]
    s_c = jnp.where(mask_c, s_c, NEG_BIG)
    m_c = jnp.max(s_c, axis=-1, keepdims=True)
    p_c = jnp.where(mask_c, jnp.exp(s_c - m_c), 0.0)
    p_c = p_c / jnp.maximum(jnp.sum(p_c, axis=-1, keepdims=True), 1e-30)
    o_c = _dot(p_c.reshape(rows, nc).astype(BF16), vc_ref[...])
    importance = _dot(p_c[0] + p_c[1] + p_c[2], ov_ref[...], precision=lax.Precision.HIGHEST)

    cur = (i * tq + qq) // SEL_LEN
    forced = (kk == 0) | (kk == cur) | (kk == cur - 1)
    score = jnp.where(forced, SEL_FORCE, jnp.where(kk <= cur, importance, NEG_BIG))
    n_rows = ((n_sel + 7) // 8) * 8
    st = score.T[0:n_rows, :]
    blk = lax.broadcasted_iota(jnp.int32, (n_rows, tq), 0)
    rank = jnp.zeros((n_rows, tq), F32)
    for b in range(n_sel):
        row = st[b:b + 1, :]
        beats = (row > st) | ((row == st) & (blk > b))
        rank = rank + jnp.where(beats, 1.0, 0.0)
    sel_t = jnp.where((rank < float(min(SEL_TOPK, n_sel))) & (st > 0.5 * NEG_BIG), 1.0, 0.0)
    if n_rows < tq:
        sel_t = jnp.concatenate([sel_t, jnp.zeros((tq - n_rows, tq), F32)], axis=0)
    sel = sel_t.T.astype(BF16)

    def key_mask(t):
        off = pl.multiple_of(t * tq, tq)
        expand = _dot(sel, e_ref[:, pl.ds(off, tq)])
        return (expand - 1.0) * (-NEG_BIG)

    def scores(k_ref, t):
        off = pl.multiple_of(t * tq, tq)
        return _dot_nt(qb, k_ref[pl.ds(off, tq), :]).reshape(r3, tq, tq)

    t_prev = jnp.maximum(i - 1, 0)
    s0 = scores(ks_ref, i) + bsel_ref[:, 0] + (key_mask(i) + jnp.where(kk <= qq, 0.0, NEG_BIG))[None]
    s1 = scores(ks_ref, t_prev) + bsel_ref[:, 1] + (key_mask(t_prev) + jnp.where(i >= 1, 0.0, NEG_BIG))[None]
    sc_ref[0] = s0
    sc_ref[1] = s1

    def far_tile(t, mx):
        s = scores(ks_ref, t) + key_mask(t)[None]
        sc_ref[t + 2] = s
        return jnp.maximum(mx, s)

    mx = lax.fori_loop(0, jnp.maximum(i - 1, 0), far_tile, jnp.maximum(s0, s1))
    m_s = jnp.max(mx, axis=-1, keepdims=True)

    def accumulate(slot, carry):
        l, acc = carry
        t = jnp.where(slot == 0, i, jnp.where(slot == 1, t_prev, slot - 2))
        off = pl.multiple_of(t * tq, tq)
        p = jnp.exp(sc_ref[slot] - m_s)
        acc = acc + _dot(p.reshape(rows, tq).astype(BF16), vs_ref[pl.ds(off, tq), :])
        return l + p, acc

    l_s, acc_s = lax.fori_loop(0, jnp.maximum(i + 1, 2), accumulate,
                               (jnp.zeros((r3, tq, tq), F32), jnp.zeros((rows, HEAD_DIM), F32)))
    den_s = jnp.maximum(jnp.sum(l_s, axis=-1, keepdims=True), 1e-30)
    o_s = acc_s.reshape(r3, tq, HEAD_DIM) / den_s

    s_w = []
    for d in range(N_WIN_PREV + 1):
        t = jnp.maximum(i - d, 0)
        lo = (N_WIN_PREV - d) * tq
        s = scores(kw_ref, t) + bw_ref[:, :, lo:lo + tq]
        if d == 0:
            s = s + jnp.where(kk <= qq, 0.0, NEG_BIG)[None]
        else:
            if d == N_WIN_PREV:
                s = s + jnp.where(kk > qq, 0.0, NEG_BIG)[None]
            s = s + jnp.where(i >= d, 0.0, NEG_BIG)
        s_w.append(s)
    m_w = s_w[0]
    for s in s_w[1:]:
        m_w = jnp.maximum(m_w, s)
    m_w = jnp.max(m_w, axis=-1, keepdims=True)
    l_w = jnp.zeros((r3, tq, tq), F32)
    acc_w = jnp.zeros((rows, HEAD_DIM), F32)
    for d, s in enumerate(s_w):
        off = pl.multiple_of(jnp.maximum(i - d, 0) * tq, tq)
        p = jnp.exp(s - m_w)
        l_w = l_w + p
        acc_w = acc_w + _dot(p.reshape(rows, tq).astype(BF16), vw_ref[pl.ds(off, tq), :])
    den_w = jnp.maximum(jnp.sum(l_w, axis=-1, keepdims=True), 1e-30)
    o_w = acc_w.reshape(r3, tq, HEAD_DIM) / den_w

    gates = jax.nn.sigmoid(gate_ref[...])
    o_c = o_c.reshape(r3, tq, HEAD_DIM)
    for r in range(r3):
        col = lambda branch: gates[:, LANE_GATE + branch * r3 + r:LANE_GATE + branch * r3 + r + 1]
        o = col(0) * o_c[r] + col(1) * o_s[r] + col(2) * o_w[r]
        o_ref[:, r * HEAD_DIM:(r + 1) * HEAD_DIM] = o.astype(BF16)


def _nsa(p4, prep, bias_c, bias_sel, bias_win, overlap, expand, q_norm, layer, tq=128):
    _, b, s, _ = p4.shape
    kc, vc, ks, vs, kw, vw = prep
    nc = kc.shape[2]
    r3 = Q_PER_KV
    n_sel = s // SEL_LEN
    n_cmp = (s - CMP_LEN) // CMP_STRIDE + 1
    band = (N_WIN_PREV + 1) * tq
    kv_small = pl.BlockSpec((None, None, nc, LANES), lambda i, g, j: (i, g, 0, 0))
    kv_full = pl.BlockSpec((None, None, s, LANES), lambda i, g, j: (i, g, 0, 0))
    return pl.pallas_call(
        functools.partial(_nsa_kernel, n_cmp=n_cmp, n_sel=n_sel),
        out_shape=jax.ShapeDtypeStruct((b, s, N_HEADS_C * HEAD_DIM), BF16),
        grid=(b, KV_HEADS, s // tq),
        in_specs=[pl.BlockSpec((r3, None, tq, LANES), lambda i, g, j: (G_QC // r3 + g, i, j, 0)),
                  pl.BlockSpec((None, None, tq, LANES), lambda i, g, j: (G_SMALL + g, i, j, 0)),
                  kv_small, kv_small, kv_full, kv_full, kv_full, kv_full,
                  pl.BlockSpec((r3, tq, nc), lambda i, g, j: (g, j, 0)),
                  pl.BlockSpec((r3, 2, tq, tq), lambda i, g, j: (g, 0, 0, 0)),
                  pl.BlockSpec((r3, tq, band), lambda i, g, j: (g, 0, 0)),
                  pl.BlockSpec((nc, LANES), lambda i, g, j: (0, 0)),
                  pl.BlockSpec((LANES, s), lambda i, g, j: (0, 0)),
                  pl.BlockSpec((None, 1, LANES), lambda i, g, j: (layer, 0, 0))],
        out_specs=pl.BlockSpec((None, tq, r3 * HEAD_DIM), lambda i, g, j: (i, j, g)),
        scratch_shapes=[pltpu.VMEM((max(s // tq, 2), r3, tq, tq), F32)],
        compiler_params=_cparams(("parallel", "parallel", "arbitrary")),
        name="nsa",
    )(p4, p4, kc, vc, ks, vs, kw, vw, bias_c, bias_sel, bias_win, overlap, expand, q_norm)


def _t5_bucket(dist):
    n = jnp.maximum(dist, 0)
    max_exact = RPB_BUCKETS // 2
    log_ratio = jnp.log(jnp.maximum(n, 1).astype(F32) / max_exact) / math.log(RPB_MAX_DIST / max_exact)
    large = jnp.minimum(max_exact + (log_ratio * (RPB_BUCKETS - max_exact)).astype(jnp.int32), RPB_BUCKETS - 1)
    return jnp.where(n < max_exact, n, large)


def _bias_tables(rel_bias, s, tq):
    table = rel_bias.astype(F32)
    look = lambda dist: jnp.moveaxis(table[_t5_bucket(dist)], -1, 0)
    nc = s // CMP_STRIDE
    pos = jnp.arange(s, dtype=jnp.int32)
    cmp_end = jnp.arange(nc, dtype=jnp.int32) * CMP_STRIDE + (CMP_LEN - 1)
    bias_c = look(pos[:, None] - cmp_end[None, :])
    q = jnp.arange(tq, dtype=jnp.int32)[:, None]
    k = jnp.arange(tq, dtype=jnp.int32)[None, :]
    far = table[RPB_BUCKETS - 1][:, None, None, None]
    bias_sel = jnp.stack([look(q - k), look(tq + q - k)], axis=1) - far
    kb = jnp.arange((N_WIN_PREV + 1) * tq, dtype=jnp.int32)[None, :]
    bias_win = look(N_WIN_PREV * tq + q - kb)
    return bias_c, bias_sel, bias_win


def _selection_constants(s, tq):
    nc = s // CMP_STRIDE
    n_cmp = (s - CMP_LEN) // CMP_STRIDE + 1
    n_sel = s // SEL_LEN
    cmp_start = np.arange(nc) * CMP_STRIDE
    sel_start = np.arange(LANES) * SEL_LEN
    overlap = ((cmp_start[:, None] < sel_start[None, :] + SEL_LEN)
               & (cmp_start[:, None] + CMP_LEN > sel_start[None, :])
               & (np.arange(nc)[:, None] < n_cmp) & (np.arange(LANES)[None, :] < n_sel))
    expand = (np.arange(LANES)[:, None] == (np.arange(s)[None, :] // SEL_LEN))
    return jnp.asarray(overlap, F32), jnp.asarray(expand, BF16)


def _arrange_w_in(w_in):
    depth, d, _ = w_in.shape
    da, db, dc, dkv = N_HEADS_A * 128, N_HEADS_B * 128, N_HEADS_C * 128, KV_HEADS * 128
    o_ba = 4 * da
    o_aa = o_ba + N_HEADS_A
    o_ub = o_aa + N_HEADS_A
    o_qc = o_ub + 2 * db
    o_kc = o_qc + dc
    o_gc = o_kc + 6 * dkv
    sl = lambda lo, n: w_in[:, :, lo:lo + n]
    gate = w_in[:, :, o_gc:o_gc + 3 * N_HEADS_C].reshape(depth, d, 3, KV_HEADS, Q_PER_KV)
    zeros = lambda n: jnp.zeros((depth, d, n), w_in.dtype)
    small0 = jnp.concatenate([sl(o_ba, 2 * N_HEADS_A), gate[:, :, :, 0, :].reshape(depth, d, 3 * Q_PER_KV),
                              zeros(LANES - 2 * N_HEADS_A - 3 * Q_PER_KV)], axis=-1)
    small1 = jnp.concatenate([zeros(LANE_GATE), gate[:, :, :, 1, :].reshape(depth, d, 3 * Q_PER_KV),
                              zeros(LANES - LANE_GATE - 3 * Q_PER_KV)], axis=-1)
    parts = [sl(0, 4 * da),
             sl(o_qc, dc),
             sl(o_kc, dkv),
             sl(o_ub, 2 * db),
             sl(o_kc + dkv, 5 * dkv),
             small0, small1]
    return jnp.concatenate(parts, axis=-1).astype(BF16)


def kernel(x, attn_norm, w_in, conv_a, a_log, dt_bias, gdn_norm, sgu_ln_g, sgu_ln_b, sgu_w, sgu_b,
           nsa_q_norm, nsa_k_norm, cmp_pos, cmp_w1, cmp_w2, rel_bias, w_out, mlp_norm, w_up, w_down):
    b, s, d = x.shape
    depth = w_in.shape[0]
    t = b * s
    tq = 128

    w_in_r = _arrange_w_in(w_in)
    w_out_b = w_out.astype(BF16)
    w_up_b = w_up.astype(BF16)
    w_down_b = w_down.astype(BF16)
    attn_g = attn_norm.reshape(depth, 1, d)
    mlp_g = mlp_norm.reshape(depth, 1, d)
    conv_r = conv_a.reshape(depth, CONV_WIDTH * 3 * N_HEADS_A, LANES)
    pad_to = lambda v, lo: jnp.pad(v, ((0, 0), (lo, LANES - lo - v.shape[1])))
    ab = jnp.stack([pad_to(a_log, LANE_DECAY), pad_to(dt_bias, LANE_DECAY)], axis=1)
    gdn_g = gdn_norm.reshape(depth, 1, LANES)
    ln_g = sgu_ln_g.reshape(depth, N_HEADS_B, LANES)
    ln_b = sgu_ln_b.reshape(depth, N_HEADS_B, LANES)
    sgu_bias = jnp.broadcast_to(sgu_b[..., None], sgu_b.shape + (LANES,))
    q_g = nsa_q_norm.reshape(depth, 1, LANES)
    k_g = nsa_k_norm.reshape(depth, 1, LANES)
    w1_b = cmp_w1.astype(BF16)
    w2_b = cmp_w2.astype(BF16)
    pos_b = jnp.broadcast_to(cmp_pos.reshape(depth, 2, 1, CMP_LEN * LANES),
                             (depth, 2, 8, CMP_LEN * LANES)).astype(BF16)
    bias_c, bias_sel, bias_win = _bias_tables(rel_bias, s, tq)
    overlap, expand = _selection_constants(s, tq)

    x2 = x.reshape(t, d)
    for layer in range(depth):
        p = _inproj(x2, attn_g, w_in_r, layer)
        p4 = p.reshape(N_GROUPS, b, s, LANES)
        mix_a = _gdn(p4, conv_r, ab, gdn_g, layer)
        mix_b = _sgu(p4, ln_g, ln_b, sgu_w, sgu_bias, layer)
        prep = _nsa_prep(p4, w1_b, w2_b, pos_b, k_g, layer)
        mix_c = _nsa(p4, prep, bias_c, bias_sel, bias_win, overlap, expand, q_g, layer, tq=tq)
        x2 = _outproj(x2, mix_a.reshape(t, -1), mix_b.reshape(t, -1), mix_c.reshape(t, -1), w_out_b, layer)
        x2 = _mlp(x2, mlp_g, w_up_b, w_down_b, layer)
    return x2.reshape(b, s, d)
```

```python
import functools
import math

import numpy as np
import jax
import jax.numpy as jnp
from jax import lax
from jax.experimental import pallas as pl
from jax.experimental.pallas import tpu as pltpu

F32 = jnp.float32
BF16 = jnp.bfloat16

LANES = 128
HEAD_DIM = 128
N_HEADS_A = 6
N_HEADS_B = 4
N_HEADS_C = 6
KV_HEADS = 2
Q_PER_KV = N_HEADS_C // KV_HEADS
CONV_WIDTH = 4
GDN_CHUNK = 64
SGU_CHUNK = 128
CMP_LEN = 32
CMP_STRIDE = 16
SEL_LEN = 64
SEL_TOPK = 16
WINDOW = 512
WIN_BLOCK = 128
N_WIN_PREV = WINDOW // WIN_BLOCK
RPB_BUCKETS = 32
RPB_MAX_DIST = 128
NORM_EPS = 1e-6
NEG_BIG = -1e30
SEL_FORCE = 1e9
VMEM_LIMIT = 56 * 1024 * 1024

G_QA, G_KA, G_VA, G_ZA = 0, 6, 12, 18
G_QC = 24
G_KCMP = 30
G_UB, G_VB = 32, 36
G_VCMP = 40
G_KSLC, G_VSLC, G_KWIN, G_VWIN = 42, 44, 46, 48
G_SMALL = 50
N_GROUPS = 52
LANE_BETA, LANE_DECAY, LANE_GATE = 0, 6, 12


def _dot(a, b, precision=None):
    return jnp.dot(a, b, preferred_element_type=F32, precision=precision)


def _dot_nt(a, b):
    return lax.dot_general(a, b, (((1,), (1,)), ((), ())), preferred_element_type=F32)


def _dot_tn(a, b):
    return lax.dot_general(a, b, (((0,), (0,)), ((), ())), preferred_element_type=F32)


def _rms(x, gain):
    return x * lax.rsqrt(jnp.mean(x * x, axis=-1, keepdims=True) + NORM_EPS) * gain


def _cparams(sem):
    return pltpu.CompilerParams(dimension_semantics=sem, vmem_limit_bytes=VMEM_LIMIT)


def _inproj_kernel(x_ref, g_ref, w_ref, o_ref, h_ref):
    @pl.when(pl.program_id(1) == 0)
    def _():
        h_ref[...] = _rms(x_ref[...], g_ref[...]).astype(BF16)

    r = _dot(h_ref[...], w_ref[...])
    for c in range(o_ref.shape[0]):
        o_ref[c] = r[:, c * LANES:(c + 1) * LANES]


def _inproj(x2, gain, w, layer, tm=512, tn=13 * LANES):
    t, d = x2.shape
    n = w.shape[-1]
    return pl.pallas_call(
        _inproj_kernel,
        out_shape=jax.ShapeDtypeStruct((n // LANES, t, LANES), F32),
        grid=(t // tm, n // tn),
        in_specs=[pl.BlockSpec((tm, d), lambda i, j: (i, 0)),
                  pl.BlockSpec((None, 1, d), lambda i, j: (layer, 0, 0)),
                  pl.BlockSpec((None, d, tn), lambda i, j: (layer, 0, j))],
        out_specs=pl.BlockSpec((tn // LANES, tm, LANES), lambda i, j: (j, i, 0)),
        scratch_shapes=[pltpu.VMEM((tm, d), BF16)],
        compiler_params=_cparams(("parallel", "arbitrary")),
        name="inproj",
    )(x2, gain, w)


def _outproj_kernel(x_ref, a_ref, b_ref, c_ref, w_ref, o_ref):
    da, db = a_ref.shape[1], b_ref.shape[1]
    acc = _dot(a_ref[...], w_ref[0:da, :])
    acc += _dot(b_ref[...], w_ref[da:da + db, :])
    acc += _dot(c_ref[...], w_ref[da + db:, :])
    o_ref[...] = x_ref[...] + acc


def _outproj(x2, ma, mb, mc, w_out, layer, tm=512):
    t, d = x2.shape
    da, db, dc = ma.shape[1], mb.shape[1], mc.shape[1]
    return pl.pallas_call(
        _outproj_kernel,
        out_shape=jax.ShapeDtypeStruct((t, d), F32),
        grid=(t // tm,),
        in_specs=[pl.BlockSpec((tm, d), lambda i: (i, 0)),
                  pl.BlockSpec((tm, da), lambda i: (i, 0)),
                  pl.BlockSpec((tm, db), lambda i: (i, 0)),
                  pl.BlockSpec((tm, dc), lambda i: (i, 0)),
                  pl.BlockSpec((None, da + db + dc, d), lambda i: (layer, 0, 0))],
        out_specs=pl.BlockSpec((tm, d), lambda i: (i, 0)),
        compiler_params=_cparams(("parallel",)),
        name="outproj",
    )(x2, ma, mb, mc, w_out)


def _mlp_kernel(x_ref, g_ref, wu_ref, wd_ref, o_ref, h_ref, acc_ref):
    k = pl.program_id(1)

    @pl.when(k == 0)
    def _():
        h_ref[...] = _rms(x_ref[...], g_ref[...]).astype(BF16)
        acc_ref[...] = jnp.zeros_like(acc_ref)

    a = jnp.maximum(_dot(h_ref[...], wu_ref[...]), 0.0)
    acc_ref[...] += _dot((a * a).astype(BF16), wd_ref[...])

    @pl.when(k == pl.num_programs(1) - 1)
    def _():
        o_ref[...] = x_ref[...] + acc_ref[...]


def _mlp(x2, gain, w_up, w_down, layer, tm=512, tf=1024):
    t, d = x2.shape
    f = w_up.shape[-1]
    return pl.pallas_call(
        _mlp_kernel,
        out_shape=jax.ShapeDtypeStruct((t, d), F32),
        grid=(t // tm, f // tf),
        in_specs=[pl.BlockSpec((tm, d), lambda i, k: (i, 0)),
                  pl.BlockSpec((None, 1, d), lambda i, k: (layer, 0, 0)),
                  pl.BlockSpec((None, d, tf), lambda i, k: (layer, 0, k)),
                  pl.BlockSpec((None, tf, d), lambda i, k: (layer, k, 0))],
        out_specs=pl.BlockSpec((tm, d), lambda i, k: (i, 0)),
        scratch_shapes=[pltpu.VMEM((tm, d), BF16), pltpu.VMEM((tm, d), F32)],
        compiler_params=_cparams(("parallel", "arbitrary")),
        name="mlp",
    )(x2, gain, w_up, w_down)


def _sgu_kernel(p_ref, lg_ref, lb_ref, w_ref, b_ref, o_ref):
    ts = p_ref.shape[1]
    nb = N_HEADS_B
    v = [jax.nn.gelu(p_ref[nb + g]) for g in range(nb)]
    width = float(nb * LANES)
    mu = sum(jnp.sum(vg, axis=-1, keepdims=True) for vg in v) / width
    var = sum(jnp.sum(jnp.square(vg - mu), axis=-1, keepdims=True) for vg in v) / width
    inv = lax.rsqrt(var + NORM_EPS)
    row = lax.broadcasted_iota(jnp.int32, (SGU_CHUNK, SGU_CHUNK), 0)
    col = lax.broadcasted_iota(jnp.int32, (SGU_CHUNK, SGU_CHUNK), 1)
    causal = col <= row
    for g in range(nb):
        vn = ((v[g] - mu) * inv * lg_ref[g:g + 1, :] + lb_ref[g:g + 1, :]).astype(BF16)
        wg = jnp.where(causal, w_ref[g], 0.0).astype(BF16)
        for c in range(ts // SGU_CHUNK):
            rows = slice(c * SGU_CHUNK, (c + 1) * SGU_CHUNK)
            mixed = _dot(wg, vn[rows]) + b_ref[g]
            u = jax.nn.gelu(p_ref[g, rows, :])
            o_ref[rows, g * LANES:(g + 1) * LANES] = (u * mixed).astype(BF16)


def _sgu(p4, ln_g, ln_b, w_s, b_s, layer, ts=512):
    _, b, s, _ = p4.shape
    nb = N_HEADS_B
    return pl.pallas_call(
        _sgu_kernel,
        out_shape=jax.ShapeDtypeStruct((b, s, nb * LANES), BF16),
        grid=(b, s // ts),
        in_specs=[pl.BlockSpec((2 * nb, None, ts, LANES), lambda i, j: (G_UB // (2 * nb), i, j, 0)),
                  pl.BlockSpec((None, nb, LANES), lambda i, j: (layer, 0, 0)),
                  pl.BlockSpec((None, nb, LANES), lambda i, j: (layer, 0, 0)),
                  pl.BlockSpec((None, nb, SGU_CHUNK, SGU_CHUNK), lambda i, j: (layer, 0, 0, 0)),
                  pl.BlockSpec((None, nb, SGU_CHUNK, LANES), lambda i, j: (layer, 0, 0, 0))],
        out_specs=pl.BlockSpec((None, ts, nb * LANES), lambda i, j: (i, j, 0)),
        compiler_params=_cparams(("parallel", "parallel")),
        name="sgu",
    )(p4, ln_g, ln_b, w_s, b_s)


def _gdn_kernel(a_ref, sm_ref, cw_ref, ab_ref, ng_ref, o_ref,
                xbuf, q_s, k_s, v_s, beta_s, gcum_s, u_s, w_s, qg_s, kd_s, a_s, st_ref, *, batch):
    ts = a_ref.shape[1]
    ck = GDN_CHUNK
    nh = N_HEADS_A
    pad = 8

    @pl.when(pl.program_id(1) == 0)
    def _():
        xbuf[:, 0:pad, :] = jnp.zeros((3 * nh, pad, LANES), F32)
        st_ref[...] = jnp.zeros_like(st_ref)

    xbuf[:, pad:pad + ts, :] = a_ref[0:3 * nh]
    dests = (q_s, k_s, v_s)
    for grp in range(3 * nh):
        acc = None
        for j in range(CONV_WIDTH):
            lo = pad - (CONV_WIDTH - 1) + j
            term = xbuf[grp, lo:lo + ts, :] * cw_ref[j * 3 * nh + grp:j * 3 * nh + grp + 1, :]
            acc = term if acc is None else acc + term
        y = acc * jax.nn.sigmoid(acc)
        kind, h = divmod(grp, nh)
        if kind < 2:
            y = y * lax.rsqrt(jnp.sum(y * y, axis=-1, keepdims=True) + NORM_EPS)
        dests[kind][h] = y
    xbuf[:, 0:pad, :] = xbuf[:, ts:ts + pad, :]

    sm = sm_ref[...]
    beta_s[...] = jax.nn.sigmoid(sm)
    z = sm + ab_ref[1:2, :]
    softplus = jnp.maximum(z, 0.0) + jnp.log1p(jnp.exp(-jnp.abs(z)))
    g_all = -jnp.exp(ab_ref[0:1, :]) * softplus

    ii = lax.broadcasted_iota(jnp.int32, (ck, ck), 0)
    jj = lax.broadcasted_iota(jnp.int32, (ck, ck), 1)
    incl = jj <= ii
    strict = jj < ii
    tri = jnp.where(incl, 1.0, 0.0).astype(F32)
    for c in range(ts // ck):
        rows = slice(c * ck, (c + 1) * ck)
        gcum_s[rows, :] = _dot(tri, g_all[rows], precision=lax.Precision.HIGHEST)

    scale = HEAD_DIM ** -0.5
    gain = ng_ref[...]
    bf = lambda m: m.astype(BF16)
    same = lambda size: (ii // size) == (jj // size)
    n_chunks = ts // ck
    rows_of = lambda c: slice(c * ck, (c + 1) * ck)
    dcol = lambda h: slice(LANE_DECAY + h, LANE_DECAY + h + 1)

    for c0 in range(0, n_chunks, batch):
        gc = {c: gcum_s[rows_of(c), :] for c in range(c0, c0 + batch)}
        gct = {c: gc[c].T for c in gc}
        items = [(c, h) for c in range(c0, c0 + batch) for h in range(nh)]
        kk, qk = {}, {}
        for c, h in items:
            g_col = gc[c][:, dcol(h)]
            q = q_s[h, rows_of(c), :] * scale
            k = k_s[h, rows_of(c), :]
            kbf = bf(k)
            kk[c, h] = _dot_nt(bf(k * beta_s[rows_of(c), LANE_BETA + h:LANE_BETA + h + 1]), kbf)
            qk[c, h] = _dot_nt(bf(q), kbf)
            qg_s[h, rows_of(c), :] = bf(q * jnp.exp(g_col))
            kd_s[h, rows_of(c), :] = bf(k * jnp.exp(gc[c][ck - 1:ck, dcol(h)] - g_col))
        lower, diag, p1 = {}, {}, {}
        for it in items:
            c, h = it
            decay = jnp.exp(jnp.where(incl, gc[c][:, dcol(h)] - gct[c][dcol(h), :], NEG_BIG))
            lower[it] = jnp.where(strict, kk[it] * decay, 0.0)
            a_s[h, rows_of(c), :] = bf(jnp.where(incl, qk[it] * decay, 0.0))
            diag[it] = jnp.where(same(8), lower[it], 0.0)
            p1[it] = _dot(bf(diag[it]), bf(diag[it]))
        p2 = {it: _dot(bf(p1[it]), bf(p1[it])) for it in items}
        dp1 = {it: _dot(bf(diag[it]), bf(p1[it])) for it in items}
        na = {it: p1[it] - diag[it] - dp1[it] for it in items}
        nap2 = {it: _dot(bf(na[it]), bf(p2[it])) for it in items}
        n = {it: na[it] + p2[it] + nap2[it] for it in items}
        size = 8
        while size < ck:
            off = same(2 * size) & ~same(size)
            cm = {it: jnp.where(off, lower[it], 0.0) for it in items}
            y = {it: cm[it] + _dot(bf(n[it]), bf(cm[it])) for it in items}
            yn = {it: _dot(bf(y[it]), bf(n[it])) for it in items}
            n = {it: n[it] - (y[it] + yn[it]) for it in items}
            size *= 2
        for it in items:
            c, h = it
            beta = beta_s[rows_of(c), LANE_BETA + h:LANE_BETA + h + 1]
            nb = bf(n[it])
            vb = v_s[h, rows_of(c), :] * beta
            kw = k_s[h, rows_of(c), :] * (beta * jnp.exp(gc[c][:, dcol(h)]))
            u_s[h, rows_of(c), :] = vb + _dot(nb, bf(vb))
            w_s[h, rows_of(c), :] = bf(kw + _dot(nb, bf(kw)))

    heads = range(nh)
    for c in range(n_chunks):
        rows = rows_of(c)
        state = [st_ref[h] for h in heads]
        sb = [bf(s) for s in state]
        ws = [_dot(w_s[h, rows, :], sb[h]) for h in heads]
        qs = [_dot(qg_s[h, rows, :], sb[h]) for h in heads]
        vnb = [bf(u_s[h, rows, :] - ws[h]) for h in heads]
        kv = [_dot_tn(kd_s[h, rows, :], vnb[h]) for h in heads]
        av = [_dot(a_s[h, rows, :], vnb[h]) for h in heads]
        for h in heads:
            g_last = gcum_s[(c + 1) * ck - 1:(c + 1) * ck, dcol(h)]
            st_ref[h] = state[h] * jnp.exp(g_last) + kv[h]
            zg = a_ref[3 * nh + h, rows, :]
            out = _rms(qs[h] + av[h], gain) * (zg * jax.nn.sigmoid(zg))
            o_ref[rows, h * LANES:(h + 1) * LANES] = out.astype(BF16)


def _gdn(p4, conv_w, ab, norm_g, layer, ts=512, batch=4):
    _, b, s, _ = p4.shape
    nh = N_HEADS_A
    return pl.pallas_call(
        functools.partial(_gdn_kernel, batch=batch),
        out_shape=jax.ShapeDtypeStruct((b, s, nh * LANES), BF16),
        grid=(b, s // ts),
        in_specs=[pl.BlockSpec((4 * nh, None, ts, LANES), lambda i, j: (0, i, j, 0)),
                  pl.BlockSpec((None, None, ts, LANES), lambda i, j: (G_SMALL, i, j, 0)),
                  pl.BlockSpec((None, CONV_WIDTH * 3 * nh, LANES), lambda i, j: (layer, 0, 0)),
                  pl.BlockSpec((None, 2, LANES), lambda i, j: (layer, 0, 0)),
                  pl.BlockSpec((None, 1, LANES), lambda i, j: (layer, 0, 0))],
        out_specs=pl.BlockSpec((None, ts, nh * LANES), lambda i, j: (i, j, 0)),
        scratch_shapes=[pltpu.VMEM((3 * nh, ts + 8, LANES), F32),
                        pltpu.VMEM((nh, ts, LANES), F32),
                        pltpu.VMEM((nh, ts, LANES), F32),
                        pltpu.VMEM((nh, ts, LANES), F32),
                        pltpu.VMEM((ts, LANES), F32),
                        pltpu.VMEM((ts, LANES), F32),
                        pltpu.VMEM((nh, ts, HEAD_DIM), F32),
                        pltpu.VMEM((nh, ts, HEAD_DIM), BF16),
                        pltpu.VMEM((nh, ts, HEAD_DIM), BF16),
                        pltpu.VMEM((nh, ts, HEAD_DIM), BF16),
                        pltpu.VMEM((nh, ts, GDN_CHUNK), BF16),
                        pltpu.VMEM((nh, HEAD_DIM, HEAD_DIM), F32)],
        compiler_params=_cparams(("parallel", "arbitrary")),
        name="gdn",
    )(p4, p4, conv_w, ab, norm_g)


def _nsa_prep_kernel(kc2_ref, vc2_ref, ksl_ref, vsl_ref, kwn_ref, vwn_ref,
                     w1_ref, w2_ref, pos_ref, kg_ref,
                     kc_o, vc_o, ks_o, vs_o, kw_o, vw_o):
    nc = kc2_ref.shape[0]
    half = kc2_ref.shape[1]
    kg = kg_ref[...]
    for kind, (src, dst) in enumerate(((kc2_ref, kc_o), (vc2_ref, vc_o))):
        t2 = src[...].astype(BF16)
        first = _dot(t2, w1_ref[kind, 0:half, :])
        second = _dot(t2, w1_ref[kind, half:2 * half, :])
        const = _dot(pos_ref[kind], w1_ref[kind])[0:1, :]
        pre = first + pltpu.roll(second, nc - 1, 0) + const
        y = _dot(jax.nn.gelu(pre).astype(BF16), w2_ref[kind])
        if kind == 0:
            y = _rms(y, kg)
        dst[...] = y.astype(BF16)
    front = ks_o.shape[0] - ksl_ref.shape[0]
    for dst, val in ((ks_o, _rms(ksl_ref[...], kg)), (kw_o, _rms(kwn_ref[...], kg)),
                     (vs_o, vsl_ref[...]), (vw_o, vwn_ref[...])):
        dst[0:front, :] = jnp.zeros((front, LANES), BF16)
        dst[front:, :] = val.astype(BF16)


def _nsa_prep(p4, w1, w2, pos, k_norm, layer):
    ng, b, s, _ = p4.shape
    nc = s // CMP_STRIDE
    half = CMP_STRIDE * LANES
    kc2 = p4[G_KCMP:G_KCMP + KV_HEADS].reshape(KV_HEADS, b, nc, half)
    vc2 = p4[G_VCMP:G_VCMP + KV_HEADS].reshape(KV_HEADS, b, nc, half)
    grp = lambda base: (lambda i, g: (base + g, i, 0, 0))
    small = jax.ShapeDtypeStruct((b, KV_HEADS, nc, LANES), BF16)
    full = jax.ShapeDtypeStruct((b, KV_HEADS, s + WINDOW, LANES), BF16)
    out_small = pl.BlockSpec((None, None, nc, LANES), lambda i, g: (i, g, 0, 0))
    out_full = pl.BlockSpec((None, None, s + WINDOW, LANES), lambda i, g: (i, g, 0, 0))
    return pl.pallas_call(
        _nsa_prep_kernel,
        out_shape=(small, small, full, full, full, full),
        grid=(b, KV_HEADS),
        in_specs=[pl.BlockSpec((None, None, nc, half), grp(0)),
                  pl.BlockSpec((None, None, nc, half), grp(0)),
                  pl.BlockSpec((None, None, s, LANES), grp(G_KSLC)),
                  pl.BlockSpec((None, None, s, LANES), grp(G_VSLC)),
                  pl.BlockSpec((None, None, s, LANES), grp(G_KWIN)),
                  pl.BlockSpec((None, None, s, LANES), grp(G_VWIN)),
                  pl.BlockSpec((None, 2, 2 * half, LANES), lambda i, g: (layer, 0, 0, 0)),
                  pl.BlockSpec((None, 2, LANES, LANES), lambda i, g: (layer, 0, 0, 0)),
                  pl.BlockSpec((None, 2, 8, 2 * half), lambda i, g: (layer, 0, 0, 0)),
                  pl.BlockSpec((None, 1, LANES), lambda i, g: (layer, 0, 0))],
        out_specs=(out_small, out_small, out_full, out_full, out_full, out_full),
        compiler_params=_cparams(("parallel", "parallel")),
        name="nsa_prep",
    )(kc2, vc2, p4, p4, p4, p4, w1, w2, pos, k_norm)


def _nsa_kernel(q_ref, gate_ref, kc_ref, vc_ref, ks_ref, vs_ref, kw_ref, vw_ref,
                bc_ref, bs_ref, bw_ref, ov_ref, e_ref, qg_ref, o_ref, sc_ref, *, n_cmp, n_sel):
    tq = q_ref.shape[1]
    r3 = Q_PER_KV
    i = pl.program_id(2)
    nc = kc_ref.shape[0]
    rows = r3 * tq

    qg = qg_ref[...]
    qf = jnp.concatenate([_rms(q_ref[r], qg) * (HEAD_DIM ** -0.5) for r in range(r3)], axis=0)
    qb = qf.astype(BF16)

    qq = lax.broadcasted_iota(jnp.int32, (tq, tq), 0)
    kk = lax.broadcasted_iota(jnp.int32, (tq, tq), 1)

    s_c = _dot_nt(qb, kc_ref[...]).reshape(r3, tq, nc) + bc_ref[...]
    qpos_c = i * tq + lax.broadcasted_iota(jnp.int32, (tq, nc), 0)
    n_idx = lax.broadcasted_iota(jnp.int32, (tq, nc), 1)
    mask_c = ((qpos_c >= n_idx * CMP_STRIDE + (CMP_LEN - 1)) & (n_idx < n_cmp))[None]
    s_c = jnp.where(mask_c, s_c, NEG_BIG)
    m_c = jnp.max(s_c, axis=-1, keepdims=True)
    p_c = jnp.where(mask_c, jnp.exp(s_c - m_c), 0.0)
    p_c = p_c / jnp.maximum(jnp.sum(p_c, axis=-1, keepdims=True), 1e-30)
    o_c = _dot(p_c.reshape(rows, nc).astype(BF16), vc_ref[...])
    importance = _dot(p_c[0] + p_c[1] + p_c[2], ov_ref[...], precision=lax.Precision.HIGHEST)

    cur = (i * tq + qq) // SEL_LEN
    forced = (kk == 0) | (kk == cur) | (kk == cur - 1)
    score = jnp.where(forced, SEL_FORCE, jnp.where(kk <= cur, importance, NEG_BIG))
    n_rows = ((n_sel + 7) // 8) * 8
    st = score.T[0:n_rows, :]
    blk = lax.broadcasted_iota(jnp.int32, (n_rows, tq), 0)
    rank = jnp.zeros((n_rows, tq), F32)
    for b in range(n_sel):
        row = st[b:b + 1, :]
        beats = (row > st) | ((row == st) & (blk > b))
        rank = rank + jnp.where(beats, 1.0, 0.0)
    sel_t = jnp.where((rank < float(min(SEL_TOPK, n_sel))) & (st > 0.5 * NEG_BIG), 1.0, 0.0)
    if n_rows < tq:
        sel_t = jnp.concatenate([sel_t, jnp.zeros((tq - n_rows, tq), F32)], axis=0)
    sel = sel_t.T.astype(BF16)

    band = (N_WIN_PREV + 1) * tq
    front = N_WIN_PREV * tq
    far_keys = sc_ref.shape[-1]
    boff = pl.multiple_of(i * tq, tq)
    q_b = lax.broadcasted_iota(jnp.int32, (tq, band), 0)
    k_b = lax.broadcasted_iota(jnp.int32, (tq, band), 1)

    def lane_fold(x, op):
        out = x[..., 0:tq]
        for c in range(1, x.shape[-1] // tq):
            out = op(out, x[..., c * tq:(c + 1) * tq])
        return out

    n_far = jnp.maximum(i - 1, 0) // (far_keys // tq)
    first_near = front - (i * tq - n_far * far_keys)
    in_block = _dot(sel, e_ref[:, pl.ds(boff, band)])
    ok_s = (k_b <= q_b + front) & (k_b >= first_near)
    s_b = (_dot_nt(qb, ks_ref[pl.ds(boff, band), :]).reshape(r3, tq, band) + bs_ref[...]
           + jnp.where(ok_s, (in_block - 1.0) * (-NEG_BIG), NEG_BIG)[None])

    def far_scores(c, mx):
        off = pl.multiple_of(front + c * far_keys, tq)
        s = _dot_nt(qb, ks_ref[pl.ds(off, far_keys), :]).reshape(r3, tq, far_keys)
        s = s + ((_dot(sel, e_ref[:, pl.ds(off, far_keys)]) - 1.0) * (-NEG_BIG))[None]
        sc_ref[c] = s
        return jnp.maximum(mx, lane_fold(s, jnp.maximum))

    mx = lax.fori_loop(0, n_far, far_scores, lane_fold(s_b, jnp.maximum))
    m_s = jnp.max(mx, axis=-1, keepdims=True)

    def far_accumulate(c, carry):
        l, acc = carry
        off = pl.multiple_of(front + c * far_keys, tq)
        p = jnp.exp(sc_ref[c] - m_s)
        acc = acc + _dot(p.reshape(rows, far_keys).astype(BF16), vs_ref[pl.ds(off, far_keys), :])
        return l + lane_fold(p, jnp.add), acc

    p_b = jnp.exp(s_b - m_s)
    l_s, acc_s = lax.fori_loop(
        0, n_far, far_accumulate,
        (lane_fold(p_b, jnp.add), _dot(p_b.reshape(rows, band).astype(BF16), vs_ref[pl.ds(boff, band), :])))
    den_s = jnp.maximum(jnp.sum(l_s, axis=-1, keepdims=True), 1e-30)
    o_s = acc_s.reshape(r3, tq, HEAD_DIM) / den_s

    ok_w = (k_b > q_b) & (k_b <= q_b + front) & (k_b >= front - i * tq)
    s_w = (_dot_nt(qb, kw_ref[pl.ds(boff, band), :]).reshape(r3, tq, band) + bw_ref[...]
           + jnp.where(ok_w, 0.0, NEG_BIG)[None])
    p_w = jnp.exp(s_w - jnp.max(s_w, axis=-1, keepdims=True))
    den_w = jnp.maximum(jnp.sum(p_w, axis=-1, keepdims=True), 1e-30)
    o_w = _dot(p_w.reshape(rows, band).astype(BF16), vw_ref[pl.ds(boff, band), :]).reshape(r3, tq, HEAD_DIM) / den_w

    gates = jax.nn.sigmoid(gate_ref[...])
    o_c = o_c.reshape(r3, tq, HEAD_DIM)
    for r in range(r3):
        col = lambda branch: gates[:, LANE_GATE + branch * r3 + r:LANE_GATE + branch * r3 + r + 1]
        o = col(0) * o_c[r] + col(1) * o_s[r] + col(2) * o_w[r]
        o_ref[:, r * HEAD_DIM:(r + 1) * HEAD_DIM] = o.astype(BF16)


def _nsa(p4, prep, bias_c, bias_sel, bias_win, overlap, expand, q_norm, layer, tq=WIN_BLOCK, far_keys=512):
    _, b, s, _ = p4.shape
    kc, vc, ks, vs, kw, vw = prep
    nc = kc.shape[2]
    r3 = Q_PER_KV
    n_sel = s // SEL_LEN
    n_cmp = (s - CMP_LEN) // CMP_STRIDE + 1
    band = (N_WIN_PREV + 1) * tq
    padded = ks.shape[2]
    kv_small = pl.BlockSpec((None, None, nc, LANES), lambda i, g, j: (i, g, 0, 0))
    kv_full = pl.BlockSpec((None, None, padded, LANES), lambda i, g, j: (i, g, 0, 0))
    return pl.pallas_call(
        functools.partial(_nsa_kernel, n_cmp=n_cmp, n_sel=n_sel),
        out_shape=jax.ShapeDtypeStruct((b, s, N_HEADS_C * HEAD_DIM), BF16),
        grid=(b, KV_HEADS, s // tq),
        in_specs=[pl.BlockSpec((r3, None, tq, LANES), lambda i, g, j: (G_QC // r3 + g, i, j, 0)),
                  pl.BlockSpec((None, None, tq, LANES), lambda i, g, j: (G_SMALL + g, i, j, 0)),
                  kv_small, kv_small, kv_full, kv_full, kv_full, kv_full,
                  pl.BlockSpec((r3, tq, nc), lambda i, g, j: (g, j, 0)),
                  pl.BlockSpec((r3, tq, band), lambda i, g, j: (g, 0, 0)),
                  pl.BlockSpec((r3, tq, band), lambda i, g, j: (g, 0, 0)),
                  pl.BlockSpec((nc, LANES), lambda i, g, j: (0, 0)),
                  pl.BlockSpec((LANES, padded), lambda i, g, j: (0, 0)),
                  pl.BlockSpec((None, 1, LANES), lambda i, g, j: (layer, 0, 0))],
        out_specs=pl.BlockSpec((None, tq, r3 * HEAD_DIM), lambda i, g, j: (i, j, g)),
        scratch_shapes=[pltpu.VMEM((s // far_keys, r3, tq, far_keys), F32)],
        compiler_params=_cparams(("parallel", "parallel", "arbitrary")),
        name="nsa",
    )(p4, p4, kc, vc, ks, vs, kw, vw, bias_c, bias_sel, bias_win, overlap, expand, q_norm)


def _t5_bucket(dist):
    n = jnp.maximum(dist, 0)
    max_exact = RPB_BUCKETS // 2
    log_ratio = jnp.log(jnp.maximum(n, 1).astype(F32) / max_exact) / math.log(RPB_MAX_DIST / max_exact)
    large = jnp.minimum(max_exact + (log_ratio * (RPB_BUCKETS - max_exact)).astype(jnp.int32), RPB_BUCKETS - 1)
    return jnp.where(n < max_exact, n, large)


def _bias_tables(rel_bias, s, tq):
    table = rel_bias.astype(F32)
    buckets = jnp.arange(RPB_BUCKETS, dtype=jnp.int32)[:, None, None]

    def look(dist):
        onehot = (_t5_bucket(dist)[None] == buckets).astype(F32)
        return jnp.einsum("nh,nqk->hqk", table, onehot, precision=lax.Precision.HIGHEST)

    nc = s // CMP_STRIDE
    pos = jnp.arange(s, dtype=jnp.int32)
    cmp_end = jnp.arange(nc, dtype=jnp.int32) * CMP_STRIDE + (CMP_LEN - 1)
    bias_c = look(pos[:, None] - cmp_end[None, :])
    q = jnp.arange(tq, dtype=jnp.int32)[:, None]
    kb = jnp.arange((N_WIN_PREV + 1) * tq, dtype=jnp.int32)[None, :]
    bias_win = look(N_WIN_PREV * tq + q - kb)
    bias_sel = bias_win - table[RPB_BUCKETS - 1][:, None, None]
    return bias_c, bias_sel, bias_win


def _selection_constants(s, tq):
    nc = s // CMP_STRIDE
    n_cmp = (s - CMP_LEN) // CMP_STRIDE + 1
    n_sel = s // SEL_LEN
    cmp_start = np.arange(nc) * CMP_STRIDE
    sel_start = np.arange(LANES) * SEL_LEN
    overlap = ((cmp_start[:, None] < sel_start[None, :] + SEL_LEN)
               & (cmp_start[:, None] + CMP_LEN > sel_start[None, :])
               & (np.arange(nc)[:, None] < n_cmp) & (np.arange(LANES)[None, :] < n_sel))
    key_pos = np.arange(-WINDOW, s)[None, :]
    expand = (np.arange(LANES)[:, None] == (key_pos // SEL_LEN)) & (key_pos >= 0)
    return jnp.asarray(overlap, F32), jnp.asarray(expand, BF16)


def _arrange_w_in(w_in):
    depth, d, _ = w_in.shape
    da, db, dc, dkv = N_HEADS_A * 128, N_HEADS_B * 128, N_HEADS_C * 128, KV_HEADS * 128
    o_ba = 4 * da
    o_aa = o_ba + N_HEADS_A
    o_ub = o_aa + N_HEADS_A
    o_qc = o_ub + 2 * db
    o_kc = o_qc + dc
    o_gc = o_kc + 6 * dkv
    sl = lambda lo, n: w_in[:, :, lo:lo + n]
    gate = w_in[:, :, o_gc:o_gc + 3 * N_HEADS_C].reshape(depth, d, 3, KV_HEADS, Q_PER_KV)
    zeros = lambda n: jnp.zeros((depth, d, n), w_in.dtype)
    small0 = jnp.concatenate([sl(o_ba, 2 * N_HEADS_A), gate[:, :, :, 0, :].reshape(depth, d, 3 * Q_PER_KV),
                              zeros(LANES - 2 * N_HEADS_A - 3 * Q_PER_KV)], axis=-1)
    small1 = jnp.concatenate([zeros(LANE_GATE), gate[:, :, :, 1, :].reshape(depth, d, 3 * Q_PER_KV),
                              zeros(LANES - LANE_GATE - 3 * Q_PER_KV)], axis=-1)
    parts = [sl(0, 4 * da),
             sl(o_qc, dc),
             sl(o_kc, dkv),
             sl(o_ub, 2 * db),
             sl(o_kc + dkv, 5 * dkv),
             small0, small1]
    return jnp.concatenate(parts, axis=-1).astype(BF16)


def kernel(x, attn_norm, w_in, conv_a, a_log, dt_bias, gdn_norm, sgu_ln_g, sgu_ln_b, sgu_w, sgu_b,
           nsa_q_norm, nsa_k_norm, cmp_pos, cmp_w1, cmp_w2, rel_bias, w_out, mlp_norm, w_up, w_down):
    b, s, d = x.shape
    depth = w_in.shape[0]
    t = b * s
    tq = 128

    w_in_r = _arrange_w_in(w_in)
    w_out_b = w_out.astype(BF16)
    w_up_b = w_up.astype(BF16)
    w_down_b = w_down.astype(BF16)
    attn_g = attn_norm.reshape(depth, 1, d)
    mlp_g = mlp_norm.reshape(depth, 1, d)
    conv_r = conv_a.reshape(depth, CONV_WIDTH * 3 * N_HEADS_A, LANES)
    pad_to = lambda v, lo: jnp.pad(v, ((0, 0), (lo, LANES - lo - v.shape[1])))
    ab = jnp.stack([pad_to(a_log, LANE_DECAY), pad_to(dt_bias, LANE_DECAY)], axis=1)
    gdn_g = gdn_norm.reshape(depth, 1, LANES)
    ln_g = sgu_ln_g.reshape(depth, N_HEADS_B, LANES)
    ln_b = sgu_ln_b.reshape(depth, N_HEADS_B, LANES)
    sgu_bias = jnp.broadcast_to(sgu_b[..., None], sgu_b.shape + (LANES,))
    q_g = nsa_q_norm.reshape(depth, 1, LANES)
    k_g = nsa_k_norm.reshape(depth, 1, LANES)
    w1_b = cmp_w1.astype(BF16)
    w2_b = cmp_w2.astype(BF16)
    pos_b = jnp.broadcast_to(cmp_pos.reshape(depth, 2, 1, CMP_LEN * LANES),
                             (depth, 2, 8, CMP_LEN * LANES)).astype(BF16)
    bias_c, bias_sel, bias_win = _bias_tables(rel_bias, s, tq)
    overlap, expand = _selection_constants(s, tq)

    x2 = x.reshape(t, d)
    for layer in range(depth):
        p = _inproj(x2, attn_g, w_in_r, layer)
        p4 = p.reshape(N_GROUPS, b, s, LANES)
        mix_a = _gdn(p4, conv_r, ab, gdn_g, layer)
        mix_b = _sgu(p4, ln_g, ln_b, sgu_w, sgu_bias, layer)
        prep = _nsa_prep(p4, w1_b, w2_b, pos_b, k_g, layer)
        mix_c = _nsa(p4, prep, bias_c, bias_sel, bias_win, overlap, expand, q_g, layer, tq=tq)
        x2 = _outproj(x2, mix_a.reshape(t, -1), mix_b.reshape(t, -1), mix_c.reshape(t, -1), w_out_b, layer)
        x2 = _mlp(x2, mlp_g, w_up_b, w_down_b, layer)
    return x2.reshape(b, s, d)
```

```python
import functools
import math

import numpy as np
import jax
import jax.numpy as jnp
from jax import lax
from jax.experimental import pallas as pl
from jax.experimental.pallas import tpu as pltpu

F32 = jnp.float32
BF16 = jnp.bfloat16

LANES = 128
HEAD_DIM = 128
N_HEADS_A = 6
N_HEADS_B = 4
N_HEADS_C = 6
KV_HEADS = 2
Q_PER_KV = N_HEADS_C // KV_HEADS
CONV_WIDTH = 4
GDN_CHUNK = 64
SGU_CHUNK = 128
CMP_LEN = 32
CMP_STRIDE = 16
SEL_LEN = 64
SEL_TOPK = 16
WINDOW = 512
WIN_BLOCK = 128
N_WIN_PREV = WINDOW // WIN_BLOCK
RPB_BUCKETS = 32
RPB_MAX_DIST = 128
NORM_EPS = 1e-6
NEG_BIG = -1e30
SEL_FORCE = 1e9
VMEM_LIMIT = 56 * 1024 * 1024

G_QA, G_KA, G_VA, G_ZA = 0, 6, 12, 18
G_QC = 24
G_KCMP = 30
G_UB, G_VB = 32, 36
G_VCMP = 40
G_KSLC, G_VSLC, G_KWIN, G_VWIN = 42, 44, 46, 48
G_SMALL = 50
N_GROUPS = 52
LANE_BETA, LANE_DECAY, LANE_GATE = 0, 6, 12


def _dot(a, b, precision=None):
    return jnp.dot(a, b, preferred_element_type=F32, precision=precision)


def _dot_nt(a, b):
    return lax.dot_general(a, b, (((1,), (1,)), ((), ())), preferred_element_type=F32)


def _dot_tn(a, b):
    return lax.dot_general(a, b, (((0,), (0,)), ((), ())), preferred_element_type=F32)


def _rms(x, gain):
    return x * lax.rsqrt(jnp.mean(x * x, axis=-1, keepdims=True) + NORM_EPS) * gain


def _cparams(sem):
    return pltpu.CompilerParams(dimension_semantics=sem, vmem_limit_bytes=VMEM_LIMIT)


def _inproj_kernel(x_ref, g_ref, w_ref, o_ref, h_ref):
    @pl.when(pl.program_id(1) == 0)
    def _():
        h_ref[...] = _rms(x_ref[...], g_ref[...]).astype(BF16)

    r = _dot(h_ref[...], w_ref[...])
    for c in range(o_ref.shape[0]):
        o_ref[c] = r[:, c * LANES:(c + 1) * LANES]


def _inproj(x2, gain, w, layer, tm=512, tn=13 * LANES):
    t, d = x2.shape
    n = w.shape[-1]
    return pl.pallas_call(
        _inproj_kernel,
        out_shape=jax.ShapeDtypeStruct((n // LANES, t, LANES), F32),
        grid=(t // tm, n // tn),
        in_specs=[pl.BlockSpec((tm, d), lambda i, j: (i, 0)),
                  pl.BlockSpec((None, 1, d), lambda i, j: (layer, 0, 0)),
                  pl.BlockSpec((None, d, tn), lambda i, j: (layer, 0, j))],
        out_specs=pl.BlockSpec((tn // LANES, tm, LANES), lambda i, j: (j, i, 0)),
        scratch_shapes=[pltpu.VMEM((tm, d), BF16)],
        compiler_params=_cparams(("parallel", "arbitrary")),
        name="inproj",
    )(x2, gain, w)


def _outproj_kernel(x_ref, a_ref, b_ref, c_ref, w_ref, o_ref):
    da, db = a_ref.shape[1], b_ref.shape[1]
    acc = _dot(a_ref[...], w_ref[0:da, :])
    acc += _dot(b_ref[...], w_ref[da:da + db, :])
    acc += _dot(c_ref[...], w_ref[da + db:, :])
    o_ref[...] = x_ref[...] + acc


def _outproj(x2, ma, mb, mc, w_out, layer, tm=512):
    t, d = x2.shape
    da, db, dc = ma.shape[1], mb.shape[1], mc.shape[1]
    return pl.pallas_call(
        _outproj_kernel,
        out_shape=jax.ShapeDtypeStruct((t, d), F32),
        grid=(t // tm,),
        in_specs=[pl.BlockSpec((tm, d), lambda i: (i, 0)),
                  pl.BlockSpec((tm, da), lambda i: (i, 0)),
                  pl.BlockSpec((tm, db), lambda i: (i, 0)),
                  pl.BlockSpec((tm, dc), lambda i: (i, 0)),
                  pl.BlockSpec((None, da + db + dc, d), lambda i: (layer, 0, 0))],
        out_specs=pl.BlockSpec((tm, d), lambda i: (i, 0)),
        compiler_params=_cparams(("parallel",)),
        name="outproj",
    )(x2, ma, mb, mc, w_out)


def _mlp_kernel(x_ref, g_ref, wu_ref, wd_ref, o_ref, h_ref, acc_ref):
    k = pl.program_id(1)

    @pl.when(k == 0)
    def _():
        h_ref[...] = _rms(x_ref[...], g_ref[...]).astype(BF16)
        acc_ref[...] = jnp.zeros_like(acc_ref)

    a = jnp.maximum(_dot(h_ref[...], wu_ref[...]), 0.0)
    acc_ref[...] += _dot((a * a).astype(BF16), wd_ref[...])

    @pl.when(k == pl.num_programs(1) - 1)
    def _():
        o_ref[...] = x_ref[...] + acc_ref[...]


def _mlp(x2, gain, w_up, w_down, layer, tm=512, tf=1024):
    t, d = x2.shape
    f = w_up.shape[-1]
    return pl.pallas_call(
        _mlp_kernel,
        out_shape=jax.ShapeDtypeStruct((t, d), F32),
        grid=(t // tm, f // tf),
        in_specs=[pl.BlockSpec((tm, d), lambda i, k: (i, 0)),
                  pl.BlockSpec((None, 1, d), lambda i, k: (layer, 0, 0)),
                  pl.BlockSpec((None, d, tf), lambda i, k: (layer, 0, k)),
                  pl.BlockSpec((None, tf, d), lambda i, k: (layer, k, 0))],
        out_specs=pl.BlockSpec((tm, d), lambda i, k: (i, 0)),
        scratch_shapes=[pltpu.VMEM((tm, d), BF16), pltpu.VMEM((tm, d), F32)],
        compiler_params=_cparams(("parallel", "arbitrary")),
        name="mlp",
    )(x2, gain, w_up, w_down)


def _sgu_kernel(p_ref, lg_ref, lb_ref, w_ref, b_ref, o_ref):
    ts = p_ref.shape[1]
    nb = N_HEADS_B
    v = [jax.nn.gelu(p_ref[nb + g]) for g in range(nb)]
    width = float(nb * LANES)
    mu = sum(jnp.sum(vg, axis=-1, keepdims=True) for vg in v) / width
    var = sum(jnp.sum(jnp.square(vg - mu), axis=-1, keepdims=True) for vg in v) / width
    inv = lax.rsqrt(var + NORM_EPS)
    row = lax.broadcasted_iota(jnp.int32, (SGU_CHUNK, SGU_CHUNK), 0)
    col = lax.broadcasted_iota(jnp.int32, (SGU_CHUNK, SGU_CHUNK), 1)
    causal = col <= row
    for g in range(nb):
        vn = ((v[g] - mu) * inv * lg_ref[g:g + 1, :] + lb_ref[g:g + 1, :]).astype(BF16)
        wg = jnp.where(causal, w_ref[g], 0.0).astype(BF16)
        for c in range(ts // SGU_CHUNK):
            rows = slice(c * SGU_CHUNK, (c + 1) * SGU_CHUNK)
            mixed = _dot(wg, vn[rows]) + b_ref[g]
            u = jax.nn.gelu(p_ref[g, rows, :])
            o_ref[rows, g * LANES:(g + 1) * LANES] = (u * mixed).astype(BF16)


def _sgu(p4, ln_g, ln_b, w_s, b_s, layer, ts=512):
    _, b, s, _ = p4.shape
    nb = N_HEADS_B
    return pl.pallas_call(
        _sgu_kernel,
        out_shape=jax.ShapeDtypeStruct((b, s, nb * LANES), BF16),
        grid=(b, s // ts),
        in_specs=[pl.BlockSpec((2 * nb, None, ts, LANES), lambda i, j: (G_UB // (2 * nb), i, j, 0)),
                  pl.BlockSpec((None, nb, LANES), lambda i, j: (layer, 0, 0)),
                  pl.BlockSpec((None, nb, LANES), lambda i, j: (layer, 0, 0)),
                  pl.BlockSpec((None, nb, SGU_CHUNK, SGU_CHUNK), lambda i, j: (layer, 0, 0, 0)),
                  pl.BlockSpec((None, nb, SGU_CHUNK, LANES), lambda i, j: (layer, 0, 0, 0))],
        out_specs=pl.BlockSpec((None, ts, nb * LANES), lambda i, j: (i, j, 0)),
        compiler_params=_cparams(("parallel", "parallel")),
        name="sgu",
    )(p4, ln_g, ln_b, w_s, b_s)


def _gdn_kernel(a_ref, sm_ref, cw_ref, ab_ref, ng_ref, o_ref,
                xbuf, q_s, k_s, v_s, beta_s, gcum_s, u_s, w_s, qg_s, kd_s, a_s, st_ref, *, batch):
    ts = a_ref.shape[1]
    ck = GDN_CHUNK
    nh = N_HEADS_A
    pad = 8

    @pl.when(pl.program_id(1) == 0)
    def _():
        xbuf[:, 0:pad, :] = jnp.zeros((3 * nh, pad, LANES), F32)
        st_ref[...] = jnp.zeros_like(st_ref)

    xbuf[:, pad:pad + ts, :] = a_ref[0:3 * nh]
    dests = (q_s, k_s, v_s)
    for grp in range(3 * nh):
        acc = None
        for j in range(CONV_WIDTH):
            lo = pad - (CONV_WIDTH - 1) + j
            term = xbuf[grp, lo:lo + ts, :] * cw_ref[j * 3 * nh + grp:j * 3 * nh + grp + 1, :]
            acc = term if acc is None else acc + term
        y = acc * jax.nn.sigmoid(acc)
        kind, h = divmod(grp, nh)
        if kind < 2:
            y = y * lax.rsqrt(jnp.sum(y * y, axis=-1, keepdims=True) + NORM_EPS)
        dests[kind][h] = y
    xbuf[:, 0:pad, :] = xbuf[:, ts:ts + pad, :]

    sm = sm_ref[...]
    beta_s[...] = jax.nn.sigmoid(sm)
    z = sm + ab_ref[1:2, :]
    softplus = jnp.maximum(z, 0.0) + jnp.log1p(jnp.exp(-jnp.abs(z)))
    g_all = -jnp.exp(ab_ref[0:1, :]) * softplus

    ii = lax.broadcasted_iota(jnp.int32, (ck, ck), 0)
    jj = lax.broadcasted_iota(jnp.int32, (ck, ck), 1)
    incl = jj <= ii
    strict = jj < ii
    tri = jnp.where(incl, 1.0, 0.0).astype(F32)
    for c in range(ts // ck):
        rows = slice(c * ck, (c + 1) * ck)
        gcum_s[rows, :] = _dot(tri, g_all[rows], precision=lax.Precision.HIGHEST)

    scale = HEAD_DIM ** -0.5
    gain = ng_ref[...]
    bf = lambda m: m.astype(BF16)
    same = lambda size: (ii // size) == (jj // size)
    n_chunks = ts // ck
    rows_of = lambda c: slice(c * ck, (c + 1) * ck)
    dcol = lambda h: slice(LANE_DECAY + h, LANE_DECAY + h + 1)

    for c0 in range(0, n_chunks, batch):
        gc = {c: gcum_s[rows_of(c), :] for c in range(c0, c0 + batch)}
        gct = {c: gc[c].T for c in gc}
        items = [(c, h) for c in range(c0, c0 + batch) for h in range(nh)]
        kk, qk = {}, {}
        for c, h in items:
            g_col = gc[c][:, dcol(h)]
            q = q_s[h, rows_of(c), :] * scale
            k = k_s[h, rows_of(c), :]
            kbf = bf(k)
            kk[c, h] = _dot_nt(bf(k * beta_s[rows_of(c), LANE_BETA + h:LANE_BETA + h + 1]), kbf)
            qk[c, h] = _dot_nt(bf(q), kbf)
            qg_s[h, rows_of(c), :] = bf(q * jnp.exp(g_col))
            kd_s[h, rows_of(c), :] = bf(k * jnp.exp(gc[c][ck - 1:ck, dcol(h)] - g_col))
        lower, diag, p1 = {}, {}, {}
        for it in items:
            c, h = it
            decay = jnp.exp(jnp.where(incl, gc[c][:, dcol(h)] - gct[c][dcol(h), :], NEG_BIG))
            lower[it] = jnp.where(strict, kk[it] * decay, 0.0)
            a_s[h, rows_of(c), :] = bf(jnp.where(incl, qk[it] * decay, 0.0))
            diag[it] = jnp.where(same(8), lower[it], 0.0)
            p1[it] = _dot(bf(diag[it]), bf(diag[it]))
        p2 = {it: _dot(bf(p1[it]), bf(p1[it])) for it in items}
        dp1 = {it: _dot(bf(diag[it]), bf(p1[it])) for it in items}
        na = {it: p1[it] - diag[it] - dp1[it] for it in items}
        nap2 = {it: _dot(bf(na[it]), bf(p2[it])) for it in items}
        n = {it: na[it] + p2[it] + nap2[it] for it in items}
        size = 8
        while size < ck:
            off = same(2 * size) & ~same(size)
            cm = {it: jnp.where(off, lower[it], 0.0) for it in items}
            y = {it: cm[it] + _dot(bf(n[it]), bf(cm[it])) for it in items}
            yn = {it: _dot(bf(y[it]), bf(n[it])) for it in items}
            n = {it: n[it] - (y[it] + yn[it]) for it in items}
            size *= 2
        for it in items:
            c, h = it
            beta = beta_s[rows_of(c), LANE_BETA + h:LANE_BETA + h + 1]
            nb = bf(n[it])
            vb = v_s[h, rows_of(c), :] * beta
            kw = k_s[h, rows_of(c), :] * (beta * jnp.exp(gc[c][:, dcol(h)]))
            u_s[h, rows_of(c), :] = vb + _dot(nb, bf(vb))
            w_s[h, rows_of(c), :] = bf(kw + _dot(nb, bf(kw)))

    heads = range(nh)
    for c in range(n_chunks):
        rows = rows_of(c)
        state = [st_ref[h] for h in heads]
        sb = [bf(s) for s in state]
        ws = [_dot(w_s[h, rows, :], sb[h]) for h in heads]
        qs = [_dot(qg_s[h, rows, :], sb[h]) for h in heads]
        vnb = [bf(u_s[h, rows, :] - ws[h]) for h in heads]
        kv = [_dot_tn(kd_s[h, rows, :], vnb[h]) for h in heads]
        av = [_dot(a_s[h, rows, :], vnb[h]) for h in heads]
        for h in heads:
            g_last = gcum_s[(c + 1) * ck - 1:(c + 1) * ck, dcol(h)]
            st_ref[h] = state[h] * jnp.exp(g_last) + kv[h]
            zg = a_ref[3 * nh + h, rows, :]
            out = _rms(qs[h] + av[h], gain) * (zg * jax.nn.sigmoid(zg))
            o_ref[rows, h * LANES:(h + 1) * LANES] = out.astype(BF16)


def _gdn(p4, conv_w, ab, norm_g, layer, ts=512, batch=4):
    _, b, s, _ = p4.shape
    nh = N_HEADS_A
    return pl.pallas_call(
        functools.partial(_gdn_kernel, batch=batch),
        out_shape=jax.ShapeDtypeStruct((b, s, nh * LANES), BF16),
        grid=(b, s // ts),
        in_specs=[pl.BlockSpec((4 * nh, None, ts, LANES), lambda i, j: (0, i, j, 0)),
                  pl.BlockSpec((None, None, ts, LANES), lambda i, j: (G_SMALL, i, j, 0)),
                  pl.BlockSpec((None, CONV_WIDTH * 3 * nh, LANES), lambda i, j: (layer, 0, 0)),
                  pl.BlockSpec((None, 2, LANES), lambda i, j: (layer, 0, 0)),
                  pl.BlockSpec((None, 1, LANES), lambda i, j: (layer, 0, 0))],
        out_specs=pl.BlockSpec((None, ts, nh * LANES), lambda i, j: (i, j, 0)),
        scratch_shapes=[pltpu.VMEM((3 * nh, ts + 8, LANES), F32),
                        pltpu.VMEM((nh, ts, LANES), F32),
                        pltpu.VMEM((nh, ts, LANES), F32),
                        pltpu.VMEM((nh, ts, LANES), F32),
                        pltpu.VMEM((ts, LANES), F32),
                        pltpu.VMEM((ts, LANES), F32),
                        pltpu.VMEM((nh, ts, HEAD_DIM), F32),
                        pltpu.VMEM((nh, ts, HEAD_DIM), BF16),
                        pltpu.VMEM((nh, ts, HEAD_DIM), BF16),
                        pltpu.VMEM((nh, ts, HEAD_DIM), BF16),
                        pltpu.VMEM((nh, ts, GDN_CHUNK), BF16),
                        pltpu.VMEM((nh, HEAD_DIM, HEAD_DIM), F32)],
        compiler_params=_cparams(("parallel", "arbitrary")),
        name="gdn",
    )(p4, p4, conv_w, ab, norm_g)


def _nsa_prep_kernel(kc2_ref, vc2_ref, ksl_ref, vsl_ref, kwn_ref, vwn_ref,
                     w1_ref, w2_ref, pos_ref, kg_ref,
                     kc_o, vc_o, ks_o, vs_o, kw_o, vw_o):
    nc = kc2_ref.shape[0]
    half = kc2_ref.shape[1]
    kg = kg_ref[...]
    for kind, (src, dst) in enumerate(((kc2_ref, kc_o), (vc2_ref, vc_o))):
        t2 = src[...].astype(BF16)
        first = _dot(t2, w1_ref[kind, 0:half, :])
        second = _dot(t2, w1_ref[kind, half:2 * half, :])
        const = _dot(pos_ref[kind], w1_ref[kind])[0:1, :]
        pre = first + pltpu.roll(second, nc - 1, 0) + const
        y = _dot(jax.nn.gelu(pre).astype(BF16), w2_ref[kind])
        if kind == 0:
            y = _rms(y, kg)
        dst[...] = y.astype(BF16)
    front = ks_o.shape[0] - ksl_ref.shape[0]
    for dst, val in ((ks_o, _rms(ksl_ref[...], kg)), (kw_o, _rms(kwn_ref[...], kg)),
                     (vs_o, vsl_ref[...]), (vw_o, vwn_ref[...])):
        dst[0:front, :] = jnp.zeros((front, LANES), BF16)
        dst[front:, :] = val.astype(BF16)


def _nsa_prep(p4, w1, w2, pos, k_norm, layer):
    ng, b, s, _ = p4.shape
    nc = s // CMP_STRIDE
    half = CMP_STRIDE * LANES
    kc2 = p4[G_KCMP:G_KCMP + KV_HEADS].reshape(KV_HEADS, b, nc, half)
    vc2 = p4[G_VCMP:G_VCMP + KV_HEADS].reshape(KV_HEADS, b, nc, half)
    grp = lambda base: (lambda i, g: (base + g, i, 0, 0))
    small = jax.ShapeDtypeStruct((b, KV_HEADS, nc, LANES), BF16)
    full = jax.ShapeDtypeStruct((b, KV_HEADS, s + WINDOW, LANES), BF16)
    out_small = pl.BlockSpec((None, None, nc, LANES), lambda i, g: (i, g, 0, 0))
    out_full = pl.BlockSpec((None, None, s + WINDOW, LANES), lambda i, g: (i, g, 0, 0))
    return pl.pallas_call(
        _nsa_prep_kernel,
        out_shape=(small, small, full, full, full, full),
        grid=(b, KV_HEADS),
        in_specs=[pl.BlockSpec((None, None, nc, half), grp(0)),
                  pl.BlockSpec((None, None, nc, half), grp(0)),
                  pl.BlockSpec((None, None, s, LANES), grp(G_KSLC)),
                  pl.BlockSpec((None, None, s, LANES), grp(G_VSLC)),
                  pl.BlockSpec((None, None, s, LANES), grp(G_KWIN)),
                  pl.BlockSpec((None, None, s, LANES), grp(G_VWIN)),
                  pl.BlockSpec((None, 2, 2 * half, LANES), lambda i, g: (layer, 0, 0, 0)),
                  pl.BlockSpec((None, 2, LANES, LANES), lambda i, g: (layer, 0, 0, 0)),
                  pl.BlockSpec((None, 2, 8, 2 * half), lambda i, g: (layer, 0, 0, 0)),
                  pl.BlockSpec((None, 1, LANES), lambda i, g: (layer, 0, 0))],
        out_specs=(out_small, out_small, out_full, out_full, out_full, out_full),
        compiler_params=_cparams(("parallel", "parallel")),
        name="nsa_prep",
    )(kc2, vc2, p4, p4, p4, p4, w1, w2, pos, k_norm)


def _nsa_kernel(q_ref, gate_ref, kc_ref, vc_ref, ks_ref, vs_ref, kw_ref, vw_ref,
                bc_ref, bs_ref, bw_ref, ov_ref, e_ref, qg_ref, o_ref, sc_ref, *, n_cmp, n_sel):
    tq = q_ref.shape[1]
    r3 = Q_PER_KV
    i = pl.program_id(2)
    nc = kc_ref.shape[0]
    rows = r3 * tq

    qg = qg_ref[...]
    qf = jnp.concatenate([_rms(q_ref[r], qg) * (HEAD_DIM ** -0.5) for r in range(r3)], axis=0)
    qb = qf.astype(BF16)

    qq = lax.broadcasted_iota(jnp.int32, (tq, tq), 0)
    kk = lax.broadcasted_iota(jnp.int32, (tq, tq), 1)

    s_c = _dot_nt(qb, kc_ref[...]).reshape(r3, tq, nc) + bc_ref[...]
    qpos_c = i * tq + lax.broadcasted_iota(jnp.int32, (tq, nc), 0)
    n_idx = lax.broadcasted_iota(jnp.int32, (tq, nc), 1)
    mask_c = ((qpos_c >= n_idx * CMP_STRIDE + (CMP_LEN - 1)) & (n_idx < n_cmp))[None]
    s_c = jnp.where(mask_c, s_c, NEG_BIG)
    m_c = jnp.max(s_c, axis=-1, keepdims=True)
    p_c = jnp.where(mask_c, jnp.exp(s_c - m_c), 0.0)
    p_c = p_c / jnp.maximum(jnp.sum(p_c, axis=-1, keepdims=True), 1e-30)
    o_c = _dot(p_c.reshape(rows, nc).astype(BF16), vc_ref[...])
    importance = _dot(p_c[0] + p_c[1] + p_c[2], ov_ref[...], precision=lax.Precision.HIGHEST)

    cur = (i * tq + qq) // SEL_LEN
    forced = (kk == 0) | (kk == cur) | (kk == cur - 1)
    score = jnp.where(forced, SEL_FORCE, jnp.where(kk <= cur, importance, NEG_BIG))
    n_rows = ((n_sel + 7) // 8) * 8
    st = score.T[0:n_rows, :]
    sub = 8
    slabs = [st[v * sub:(v + 1) * sub, :] for v in range(n_rows // sub)]
    blk = lax.broadcasted_iota(jnp.int32, (sub, tq), 0)
    ranks = [jnp.zeros((sub, tq), F32) for _ in slabs]
    for b in range(n_sel):
        row = st[b:b + 1, :]
        for v, slab in enumerate(slabs):
            if v * sub > b:
                beats = row >= slab
            elif (v + 1) * sub - 1 <= b:
                beats = row > slab
            else:
                beats = (row > slab) | ((row == slab) & (blk + v * sub > b))
            ranks[v] = ranks[v] + jnp.where(beats, 1.0, 0.0)
    rank = jnp.concatenate(ranks, axis=0)
    sel_t = jnp.where((rank < float(min(SEL_TOPK, n_sel))) & (st > 0.5 * NEG_BIG), 1.0, 0.0)
    if n_rows < tq:
        sel_t = jnp.concatenate([sel_t, jnp.zeros((tq - n_rows, tq), F32)], axis=0)
    sel = sel_t.T.astype(BF16)

    band = (N_WIN_PREV + 1) * tq
    front = N_WIN_PREV * tq
    far_keys = sc_ref.shape[-1]
    boff = pl.multiple_of(i * tq, tq)
    q_b = lax.broadcasted_iota(jnp.int32, (tq, band), 0)
    k_b = lax.broadcasted_iota(jnp.int32, (tq, band), 1)

    def lane_fold(x, op):
        out = x[..., 0:tq]
        for c in range(1, x.shape[-1] // tq):
            out = op(out, x[..., c * tq:(c + 1) * tq])
        return out

    n_far = jnp.maximum(i - 1, 0) // (far_keys // tq)

    def selected(nf):
        first_near = front - (i * tq - nf * far_keys)
        in_block = _dot(sel, e_ref[:, pl.ds(boff, band)])
        ok_s = (k_b <= q_b + front) & (k_b >= first_near)
        s_b = (_dot_nt(qb, ks_ref[pl.ds(boff, band), :]).reshape(r3, tq, band) + bs_ref[...]
               + jnp.where(ok_s, (in_block - 1.0) * (-NEG_BIG), NEG_BIG)[None])
        mx = lane_fold(s_b, jnp.maximum)
        for c in range(nf):
            keys = slice(front + c * far_keys, front + (c + 1) * far_keys)
            s = _dot_nt(qb, ks_ref[keys, :]).reshape(r3, tq, far_keys)
            s = s + ((_dot(sel, e_ref[:, keys]) - 1.0) * (-NEG_BIG))[None]
            sc_ref[c] = s
            mx = jnp.maximum(mx, lane_fold(s, jnp.maximum))
        m_s = jnp.max(mx, axis=-1, keepdims=True)
        p_b = jnp.exp(s_b - m_s)
        l_s = lane_fold(p_b, jnp.add)
        acc_s = _dot(p_b.reshape(rows, band).astype(BF16), vs_ref[pl.ds(boff, band), :])
        for c in range(nf):
            keys = slice(front + c * far_keys, front + (c + 1) * far_keys)
            p = jnp.exp(sc_ref[c] - m_s)
            l_s = l_s + lane_fold(p, jnp.add)
            acc_s = acc_s + _dot(p.reshape(rows, far_keys).astype(BF16), vs_ref[keys, :])
        den_s = jnp.maximum(jnp.sum(l_s, axis=-1, keepdims=True), 1e-30)
        return acc_s.reshape(r3, tq, HEAD_DIM) / den_s

    ok_w = (k_b > q_b) & (k_b <= q_b + front) & (k_b >= front - i * tq)
    s_w = (_dot_nt(qb, kw_ref[pl.ds(boff, band), :]).reshape(r3, tq, band) + bw_ref[...]
           + jnp.where(ok_w, 0.0, NEG_BIG)[None])
    p_w = jnp.exp(s_w - jnp.max(s_w, axis=-1, keepdims=True))
    den_w = jnp.maximum(jnp.sum(p_w, axis=-1, keepdims=True), 1e-30)
    o_w = _dot(p_w.reshape(rows, band).astype(BF16), vw_ref[pl.ds(boff, band), :]).reshape(r3, tq, HEAD_DIM) / den_w

    gates = jax.nn.sigmoid(gate_ref[...])
    o_c = o_c.reshape(r3, tq, HEAD_DIM)
    col = lambda branch, r: gates[:, LANE_GATE + branch * r3 + r:LANE_GATE + branch * r3 + r + 1]
    partial = [col(0, r) * o_c[r] + col(2, r) * o_w[r] for r in range(r3)]
    gate_s = [col(1, r) for r in range(r3)]

    for nf in range(sc_ref.shape[0]):
        @pl.when(n_far == nf)
        def _(nf=nf):
            o_s = selected(nf)
            for r in range(r3):
                o_ref[:, r * HEAD_DIM:(r + 1) * HEAD_DIM] = (partial[r] + gate_s[r] * o_s[r]).astype(BF16)


def _nsa(p4, prep, bias_c, bias_sel, bias_win, overlap, expand, q_norm, layer, tq=WIN_BLOCK, far_keys=512):
    _, b, s, _ = p4.shape
    kc, vc, ks, vs, kw, vw = prep
    nc = kc.shape[2]
    r3 = Q_PER_KV
    n_sel = s // SEL_LEN
    n_cmp = (s - CMP_LEN) // CMP_STRIDE + 1
    band = (N_WIN_PREV + 1) * tq
    padded = ks.shape[2]
    kv_small = pl.BlockSpec((None, None, nc, LANES), lambda i, g, j: (i, g, 0, 0))
    kv_full = pl.BlockSpec((None, None, padded, LANES), lambda i, g, j: (i, g, 0, 0))
    return pl.pallas_call(
        functools.partial(_nsa_kernel, n_cmp=n_cmp, n_sel=n_sel),
        out_shape=jax.ShapeDtypeStruct((b, s, N_HEADS_C * HEAD_DIM), BF16),
        grid=(b, KV_HEADS, s // tq),
        in_specs=[pl.BlockSpec((r3, None, tq, LANES), lambda i, g, j: (G_QC // r3 + g, i, j, 0)),
                  pl.BlockSpec((None, None, tq, LANES), lambda i, g, j: (G_SMALL + g, i, j, 0)),
                  kv_small, kv_small, kv_full, kv_full, kv_full, kv_full,
                  pl.BlockSpec((r3, tq, nc), lambda i, g, j: (g, j, 0)),
                  pl.BlockSpec((r3, tq, band), lambda i, g, j: (g, 0, 0)),
                  pl.BlockSpec((r3, tq, band), lambda i, g, j: (g, 0, 0)),
                  pl.BlockSpec((nc, LANES), lambda i, g, j: (0, 0)),
                  pl.BlockSpec((LANES, padded), lambda i, g, j: (0, 0)),
                  pl.BlockSpec((None, 1, LANES), lambda i, g, j: (layer, 0, 0))],
        out_specs=pl.BlockSpec((None, tq, r3 * HEAD_DIM), lambda i, g, j: (i, j, g)),
        scratch_shapes=[pltpu.VMEM((s // far_keys, r3, tq, far_keys), F32)],
        compiler_params=_cparams(("parallel", "parallel", "arbitrary")),
        name="nsa",
    )(p4, p4, kc, vc, ks, vs, kw, vw, bias_c, bias_sel, bias_win, overlap, expand, q_norm)


def _t5_bucket(dist):
    n = jnp.maximum(dist, 0)
    max_exact = RPB_BUCKETS // 2
    log_ratio = jnp.log(jnp.maximum(n, 1).astype(F32) / max_exact) / math.log(RPB_MAX_DIST / max_exact)
    large = jnp.minimum(max_exact + (log_ratio * (RPB_BUCKETS - max_exact)).astype(jnp.int32), RPB_BUCKETS - 1)
    return jnp.where(n < max_exact, n, large)


def _bias_tables(rel_bias, s, tq):
    table = rel_bias.astype(F32)
    buckets = jnp.arange(RPB_BUCKETS, dtype=jnp.int32)[:, None, None]

    def look(dist):
        onehot = (_t5_bucket(dist)[None] == buckets).astype(F32)
        return jnp.einsum("nh,nqk->hqk", table, onehot, precision=lax.Precision.HIGHEST)

    nc = s // CMP_STRIDE
    pos = jnp.arange(s, dtype=jnp.int32)
    cmp_end = jnp.arange(nc, dtype=jnp.int32) * CMP_STRIDE + (CMP_LEN - 1)
    bias_c = look(pos[:, None] - cmp_end[None, :])
    q = jnp.arange(tq, dtype=jnp.int32)[:, None]
    kb = jnp.arange((N_WIN_PREV + 1) * tq, dtype=jnp.int32)[None, :]
    bias_win = look(N_WIN_PREV * tq + q - kb)
    bias_sel = bias_win - table[RPB_BUCKETS - 1][:, None, None]
    return bias_c, bias_sel, bias_win


def _selection_constants(s, tq):
    nc = s // CMP_STRIDE
    n_cmp = (s - CMP_LEN) // CMP_STRIDE + 1
    n_sel = s // SEL_LEN
    cmp_start = np.arange(nc) * CMP_STRIDE
    sel_start = np.arange(LANES) * SEL_LEN
    overlap = ((cmp_start[:, None] < sel_start[None, :] + SEL_LEN)
               & (cmp_start[:, None] + CMP_LEN > sel_start[None, :])
               & (np.arange(nc)[:, None] < n_cmp) & (np.arange(LANES)[None, :] < n_sel))
    key_pos = np.arange(-WINDOW, s)[None, :]
    expand = (np.arange(LANES)[:, None] == (key_pos // SEL_LEN)) & (key_pos >= 0)
    return jnp.asarray(overlap, F32), jnp.asarray(expand, BF16)


def _arrange_w_in(w_in):
    depth, d, _ = w_in.shape
    da, db, dc, dkv = N_HEADS_A * 128, N_HEADS_B * 128, N_HEADS_C * 128, KV_HEADS * 128
    o_ba = 4 * da
    o_aa = o_ba + N_HEADS_A
    o_ub = o_aa + N_HEADS_A
    o_qc = o_ub + 2 * db
    o_kc = o_qc + dc
    o_gc = o_kc + 6 * dkv
    sl = lambda lo, n: w_in[:, :, lo:lo + n]
    gate = w_in[:, :, o_gc:o_gc + 3 * N_HEADS_C].reshape(depth, d, 3, KV_HEADS, Q_PER_KV)
    zeros = lambda n: jnp.zeros((depth, d, n), w_in.dtype)
    small0 = jnp.concatenate([sl(o_ba, 2 * N_HEADS_A), gate[:, :, :, 0, :].reshape(depth, d, 3 * Q_PER_KV),
                              zeros(LANES - 2 * N_HEADS_A - 3 * Q_PER_KV)], axis=-1)
    small1 = jnp.concatenate([zeros(LANE_GATE), gate[:, :, :, 1, :].reshape(depth, d, 3 * Q_PER_KV),
                              zeros(LANES - LANE_GATE - 3 * Q_PER_KV)], axis=-1)
    parts = [sl(0, 4 * da),
             sl(o_qc, dc),
             sl(o_kc, dkv),
             sl(o_ub, 2 * db),
             sl(o_kc + dkv, 5 * dkv),
             small0, small1]
    return jnp.concatenate(parts, axis=-1).astype(BF16)


def kernel(x, attn_norm, w_in, conv_a, a_log, dt_bias, gdn_norm, sgu_ln_g, sgu_ln_b, sgu_w, sgu_b,
           nsa_q_norm, nsa_k_norm, cmp_pos, cmp_w1, cmp_w2, rel_bias, w_out, mlp_norm, w_up, w_down):
    b, s, d = x.shape
    depth = w_in.shape[0]
    t = b * s
    tq = 128

    w_in_r = _arrange_w_in(w_in)
    w_out_b = w_out.astype(BF16)
    w_up_b = w_up.astype(BF16)
    w_down_b = w_down.astype(BF16)
    attn_g = attn_norm.reshape(depth, 1, d)
    mlp_g = mlp_norm.reshape(depth, 1, d)
    conv_r = conv_a.reshape(depth, CONV_WIDTH * 3 * N_HEADS_A, LANES)
    pad_to = lambda v, lo: jnp.pad(v, ((0, 0), (lo, LANES - lo - v.shape[1])))
    ab = jnp.stack([pad_to(a_log, LANE_DECAY), pad_to(dt_bias, LANE_DECAY)], axis=1)
    gdn_g = gdn_norm.reshape(depth, 1, LANES)
    ln_g = sgu_ln_g.reshape(depth, N_HEADS_B, LANES)
    ln_b = sgu_ln_b.reshape(depth, N_HEADS_B, LANES)
    sgu_bias = jnp.broadcast_to(sgu_b[..., None], sgu_b.shape + (LANES,))
    q_g = nsa_q_norm.reshape(depth, 1, LANES)
    k_g = nsa_k_norm.reshape(depth, 1, LANES)
    w1_b = cmp_w1.astype(BF16)
    w2_b = cmp_w2.astype(BF16)
    pos_b = jnp.broadcast_to(cmp_pos.reshape(depth, 2, 1, CMP_LEN * LANES),
                             (depth, 2, 8, CMP_LEN * LANES)).astype(BF16)
    bias_c, bias_sel, bias_win = _bias_tables(rel_bias, s, tq)
    overlap, expand = _selection_constants(s, tq)

    x2 = x.reshape(t, d)
    for layer in range(depth):
        p = _inproj(x2, attn_g, w_in_r, layer)
        p4 = p.reshape(N_GROUPS, b, s, LANES)
        mix_a = _gdn(p4, conv_r, ab, gdn_g, layer)
        mix_b = _sgu(p4, ln_g, ln_b, sgu_w, sgu_bias, layer)
        prep = _nsa_prep(p4, w1_b, w2_b, pos_b, k_g, layer)
        mix_c = _nsa(p4, prep, bias_c, bias_sel, bias_win, overlap, expand, q_g, layer, tq=tq)
        x2 = _outproj(x2, mix_a.reshape(t, -1), mix_b.reshape(t, -1), mix_c.reshape(t, -1), w_out_b, layer)
        x2 = _mlp(x2, mlp_g, w_up_b, w_down_b, layer)
    return x2.reshape(b, s, d)
```

```python
import functools
import math

import numpy as np
import jax
import jax.numpy as jnp
from jax import lax
from jax.experimental import pallas as pl
from jax.experimental.pallas import tpu as pltpu

F32 = jnp.float32
BF16 = jnp.bfloat16

LANES = 128
HEAD_DIM = 128
N_HEADS_A = 6
N_HEADS_B = 4
N_HEADS_C = 6
KV_HEADS = 2
Q_PER_KV = N_HEADS_C // KV_HEADS
CONV_WIDTH = 4
GDN_CHUNK = 64
SGU_CHUNK = 128
CMP_LEN = 32
CMP_STRIDE = 16
SEL_LEN = 64
SEL_TOPK = 16
WINDOW = 512
WIN_BLOCK = 128
N_WIN_PREV = WINDOW // WIN_BLOCK
RPB_BUCKETS = 32
RPB_MAX_DIST = 128
NORM_EPS = 1e-6
NEG_BIG = -1e30
SEL_FORCE = 1e9
VMEM_LIMIT = 56 * 1024 * 1024

G_QA, G_KA, G_VA, G_ZA = 0, 6, 12, 18
G_QC = 24
G_KCMP = 30
G_UB, G_VB = 32, 36
G_VCMP = 40
G_KSLC, G_VSLC, G_KWIN, G_VWIN = 42, 44, 46, 48
G_SMALL = 50
N_GROUPS = 52
LANE_BETA, LANE_DECAY, LANE_GATE = 0, 6, 12


def _dot(a, b, precision=None):
    return jnp.dot(a, b, preferred_element_type=F32, precision=precision)


def _dot_nt(a, b):
    return lax.dot_general(a, b, (((1,), (1,)), ((), ())), preferred_element_type=F32)


def _dot_tn(a, b):
    return lax.dot_general(a, b, (((0,), (0,)), ((), ())), preferred_element_type=F32)


def _rms(x, gain):
    return x * lax.rsqrt(jnp.mean(x * x, axis=-1, keepdims=True) + NORM_EPS) * gain


def _cparams(sem):
    return pltpu.CompilerParams(dimension_semantics=sem, vmem_limit_bytes=VMEM_LIMIT)


def _inproj_kernel(x_ref, g_ref, w_ref, o_ref, h_ref):
    @pl.when(pl.program_id(1) == 0)
    def _():
        h_ref[...] = _rms(x_ref[...], g_ref[...]).astype(BF16)

    r = _dot(h_ref[...], w_ref[...])
    for c in range(o_ref.shape[0]):
        o_ref[c] = r[:, c * LANES:(c + 1) * LANES]


def _inproj(x2, gain, w, layer, tm=512, tn=13 * LANES):
    t, d = x2.shape
    n = w.shape[-1]
    return pl.pallas_call(
        _inproj_kernel,
        out_shape=jax.ShapeDtypeStruct((n // LANES, t, LANES), F32),
        grid=(t // tm, n // tn),
        in_specs=[pl.BlockSpec((tm, d), lambda i, j: (i, 0)),
                  pl.BlockSpec((None, 1, d), lambda i, j: (layer, 0, 0)),
                  pl.BlockSpec((None, d, tn), lambda i, j: (layer, 0, j))],
        out_specs=pl.BlockSpec((tn // LANES, tm, LANES), lambda i, j: (j, i, 0)),
        scratch_shapes=[pltpu.VMEM((tm, d), BF16)],
        compiler_params=_cparams(("parallel", "arbitrary")),
        name="inproj",
    )(x2, gain, w)


def _outproj_kernel(x_ref, a_ref, b_ref, c_ref, w_ref, o_ref):
    da, db = a_ref.shape[1], b_ref.shape[1]
    acc = _dot(a_ref[...], w_ref[0:da, :])
    acc += _dot(b_ref[...], w_ref[da:da + db, :])
    acc += _dot(c_ref[...], w_ref[da + db:, :])
    o_ref[...] = x_ref[...] + acc


def _outproj(x2, ma, mb, mc, w_out, layer, tm=512):
    t, d = x2.shape
    da, db, dc = ma.shape[1], mb.shape[1], mc.shape[1]
    return pl.pallas_call(
        _outproj_kernel,
        out_shape=jax.ShapeDtypeStruct((t, d), F32),
        grid=(t // tm,),
        in_specs=[pl.BlockSpec((tm, d), lambda i: (i, 0)),
                  pl.BlockSpec((tm, da), lambda i: (i, 0)),
                  pl.BlockSpec((tm, db), lambda i: (i, 0)),
                  pl.BlockSpec((tm, dc), lambda i: (i, 0)),
                  pl.BlockSpec((None, da + db + dc, d), lambda i: (layer, 0, 0))],
        out_specs=pl.BlockSpec((tm, d), lambda i: (i, 0)),
        compiler_params=_cparams(("parallel",)),
        name="outproj",
    )(x2, ma, mb, mc, w_out)


def _mlp_kernel(x_ref, g_ref, wu_ref, wd_ref, o_ref, h_ref, acc_ref):
    k = pl.program_id(1)

    @pl.when(k == 0)
    def _():
        h_ref[...] = _rms(x_ref[...], g_ref[...]).astype(BF16)
        acc_ref[...] = jnp.zeros_like(acc_ref)

    a = jnp.maximum(_dot(h_ref[...], wu_ref[...]), 0.0)
    acc_ref[...] += _dot((a * a).astype(BF16), wd_ref[...])

    @pl.when(k == pl.num_programs(1) - 1)
    def _():
        o_ref[...] = x_ref[...] + acc_ref[...]


def _mlp(x2, gain, w_up, w_down, layer, tm=512, tf=1024):
    t, d = x2.shape
    f = w_up.shape[-1]
    return pl.pallas_call(
        _mlp_kernel,
        out_shape=jax.ShapeDtypeStruct((t, d), F32),
        grid=(t // tm, f // tf),
        in_specs=[pl.BlockSpec((tm, d), lambda i, k: (i, 0)),
                  pl.BlockSpec((None, 1, d), lambda i, k: (layer, 0, 0)),
                  pl.BlockSpec((None, d, tf), lambda i, k: (layer, 0, k)),
                  pl.BlockSpec((None, tf, d), lambda i, k: (layer, k, 0))],
        out_specs=pl.BlockSpec((tm, d), lambda i, k: (i, 0)),
        scratch_shapes=[pltpu.VMEM((tm, d), BF16), pltpu.VMEM((tm, d), F32)],
        compiler_params=_cparams(("parallel", "arbitrary")),
        name="mlp",
    )(x2, gain, w_up, w_down)


def _sgu_kernel(p_ref, lg_ref, lb_ref, w_ref, b_ref, o_ref):
    ts = p_ref.shape[1]
    nb = N_HEADS_B
    v = [jax.nn.gelu(p_ref[nb + g]) for g in range(nb)]
    width = float(nb * LANES)
    mu = sum(jnp.sum(vg, axis=-1, keepdims=True) for vg in v) / width
    var = sum(jnp.sum(jnp.square(vg - mu), axis=-1, keepdims=True) for vg in v) / width
    inv = lax.rsqrt(var + NORM_EPS)
    row = lax.broadcasted_iota(jnp.int32, (SGU_CHUNK, SGU_CHUNK), 0)
    col = lax.broadcasted_iota(jnp.int32, (SGU_CHUNK, SGU_CHUNK), 1)
    causal = col <= row
    for g in range(nb):
        vn = ((v[g] - mu) * inv * lg_ref[g:g + 1, :] + lb_ref[g:g + 1, :]).astype(BF16)
        wg = jnp.where(causal, w_ref[g], 0.0).astype(BF16)
        for c in range(ts // SGU_CHUNK):
            rows = slice(c * SGU_CHUNK, (c + 1) * SGU_CHUNK)
            mixed = _dot(wg, vn[rows]) + b_ref[g]
            u = jax.nn.gelu(p_ref[g, rows, :])
            o_ref[rows, g * LANES:(g + 1) * LANES] = (u * mixed).astype(BF16)


def _sgu(p4, ln_g, ln_b, w_s, b_s, layer, ts=512):
    _, b, s, _ = p4.shape
    nb = N_HEADS_B
    return pl.pallas_call(
        _sgu_kernel,
        out_shape=jax.ShapeDtypeStruct((b, s, nb * LANES), BF16),
        grid=(b, s // ts),
        in_specs=[pl.BlockSpec((2 * nb, None, ts, LANES), lambda i, j: (G_UB // (2 * nb), i, j, 0)),
                  pl.BlockSpec((None, nb, LANES), lambda i, j: (layer, 0, 0)),
                  pl.BlockSpec((None, nb, LANES), lambda i, j: (layer, 0, 0)),
                  pl.BlockSpec((None, nb, SGU_CHUNK, SGU_CHUNK), lambda i, j: (layer, 0, 0, 0)),
                  pl.BlockSpec((None, nb, SGU_CHUNK, LANES), lambda i, j: (layer, 0, 0, 0))],
        out_specs=pl.BlockSpec((None, ts, nb * LANES), lambda i, j: (i, j, 0)),
        compiler_params=_cparams(("parallel", "parallel")),
        name="sgu",
    )(p4, ln_g, ln_b, w_s, b_s)


def _gdn_kernel(a_ref, sm_ref, cw_ref, ab_ref, ng_ref, o_ref,
                xbuf, q_s, k_s, v_s, beta_s, gcum_s, u_s, w_s, qg_s, kd_s, a_s, st_ref, *, batch):
    ts = a_ref.shape[1]
    ck = GDN_CHUNK
    nh = N_HEADS_A
    pad = 8

    @pl.when(pl.program_id(1) == 0)
    def _():
        xbuf[:, 0:pad, :] = jnp.zeros((3 * nh, pad, LANES), F32)
        st_ref[...] = jnp.zeros_like(st_ref)

    xbuf[:, pad:pad + ts, :] = a_ref[0:3 * nh]
    dests = (q_s, k_s, v_s)
    for grp in range(3 * nh):
        acc = None
        for j in range(CONV_WIDTH):
            lo = pad - (CONV_WIDTH - 1) + j
            term = xbuf[grp, lo:lo + ts, :] * cw_ref[j * 3 * nh + grp:j * 3 * nh + grp + 1, :]
            acc = term if acc is None else acc + term
        y = acc * jax.nn.sigmoid(acc)
        kind, h = divmod(grp, nh)
        if kind < 2:
            y = y * lax.rsqrt(jnp.sum(y * y, axis=-1, keepdims=True) + NORM_EPS)
        dests[kind][h] = y
    xbuf[:, 0:pad, :] = xbuf[:, ts:ts + pad, :]

    sm = sm_ref[...]
    beta_s[...] = jax.nn.sigmoid(sm)
    z = sm + ab_ref[1:2, :]
    softplus = jnp.maximum(z, 0.0) + jnp.log1p(jnp.exp(-jnp.abs(z)))
    g_all = -jnp.exp(ab_ref[0:1, :]) * softplus

    ii = lax.broadcasted_iota(jnp.int32, (ck, ck), 0)
    jj = lax.broadcasted_iota(jnp.int32, (ck, ck), 1)
    incl = jj <= ii
    strict = jj < ii
    tri = jnp.where(incl, 1.0, 0.0).astype(F32)
    for c in range(ts // ck):
        rows = slice(c * ck, (c + 1) * ck)
        gcum_s[rows, :] = _dot(tri, g_all[rows], precision=lax.Precision.HIGHEST)

    scale = HEAD_DIM ** -0.5
    gain = ng_ref[...]
    bf = lambda m: m.astype(BF16)
    same = lambda size: (ii // size) == (jj // size)
    n_chunks = ts // ck
    rows_of = lambda c: slice(c * ck, (c + 1) * ck)
    dcol = lambda h: slice(LANE_DECAY + h, LANE_DECAY + h + 1)

    for c0 in range(0, n_chunks, batch):
        gc = {c: gcum_s[rows_of(c), :] for c in range(c0, c0 + batch)}
        gct = {c: gc[c].T for c in gc}
        items = [(c, h) for c in range(c0, c0 + batch) for h in range(nh)]
        kk, qk = {}, {}
        for c, h in items:
            g_col = gc[c][:, dcol(h)]
            q = q_s[h, rows_of(c), :] * scale
            k = k_s[h, rows_of(c), :]
            kbf = bf(k)
            kk[c, h] = _dot_nt(bf(k * beta_s[rows_of(c), LANE_BETA + h:LANE_BETA + h + 1]), kbf)
            qk[c, h] = _dot_nt(bf(q), kbf)
            qg_s[h, rows_of(c), :] = bf(q * jnp.exp(g_col))
            kd_s[h, rows_of(c), :] = bf(k * jnp.exp(gc[c][ck - 1:ck, dcol(h)] - g_col))
        lower, diag, p1 = {}, {}, {}
        for it in items:
            c, h = it
            decay = jnp.exp(jnp.where(incl, gc[c][:, dcol(h)] - gct[c][dcol(h), :], NEG_BIG))
            lower[it] = jnp.where(strict, kk[it] * decay, 0.0)
            a_s[h, rows_of(c), :] = bf(jnp.where(incl, qk[it] * decay, 0.0))
            diag[it] = jnp.where(same(8), lower[it], 0.0)
            p1[it] = _dot(bf(diag[it]), bf(diag[it]))
        p2 = {it: _dot(bf(p1[it]), bf(p1[it])) for it in items}
        dp1 = {it: _dot(bf(diag[it]), bf(p1[it])) for it in items}
        na = {it: p1[it] - diag[it] - dp1[it] for it in items}
        nap2 = {it: _dot(bf(na[it]), bf(p2[it])) for it in items}
        n = {it: na[it] + p2[it] + nap2[it] for it in items}
        size = 8
        while size < ck:
            off = same(2 * size) & ~same(size)
            cm = {it: jnp.where(off, lower[it], 0.0) for it in items}
            y = {it: cm[it] + _dot(bf(n[it]), bf(cm[it])) for it in items}
            yn = {it: _dot(bf(y[it]), bf(n[it])) for it in items}
            n = {it: n[it] - (y[it] + yn[it]) for it in items}
            size *= 2
        for it in items:
            c, h = it
            beta = beta_s[rows_of(c), LANE_BETA + h:LANE_BETA + h + 1]
            nb = bf(n[it])
            vb = v_s[h, rows_of(c), :] * beta
            kw = k_s[h, rows_of(c), :] * (beta * jnp.exp(gc[c][:, dcol(h)]))
            u_s[h, rows_of(c), :] = vb + _dot(nb, bf(vb))
            w_s[h, rows_of(c), :] = bf(kw + _dot(nb, bf(kw)))

    heads = range(nh)
    for c in range(n_chunks):
        rows = rows_of(c)
        state = [st_ref[h] for h in heads]
        sb = [bf(s) for s in state]
        ws = [_dot(w_s[h, rows, :], sb[h]) for h in heads]
        qs = [_dot(qg_s[h, rows, :], sb[h]) for h in heads]
        vnb = [bf(u_s[h, rows, :] - ws[h]) for h in heads]
        kv = [_dot_tn(kd_s[h, rows, :], vnb[h]) for h in heads]
        av = [_dot(a_s[h, rows, :], vnb[h]) for h in heads]
        for h in heads:
            g_last = gcum_s[(c + 1) * ck - 1:(c + 1) * ck, dcol(h)]
            st_ref[h] = state[h] * jnp.exp(g_last) + kv[h]
            zg = a_ref[3 * nh + h, rows, :]
            out = _rms(qs[h] + av[h], gain) * (zg * jax.nn.sigmoid(zg))
            o_ref[rows, h * LANES:(h + 1) * LANES] = out.astype(BF16)


def _gdn(p4, conv_w, ab, norm_g, layer, ts=512, batch=4):
    _, b, s, _ = p4.shape
    nh = N_HEADS_A
    return pl.pallas_call(
        functools.partial(_gdn_kernel, batch=batch),
        out_shape=jax.ShapeDtypeStruct((b, s, nh * LANES), BF16),
        grid=(b, s // ts),
        in_specs=[pl.BlockSpec((4 * nh, None, ts, LANES), lambda i, j: (0, i, j, 0)),
                  pl.BlockSpec((None, None, ts, LANES), lambda i, j: (G_SMALL, i, j, 0)),
                  pl.BlockSpec((None, CONV_WIDTH * 3 * nh, LANES), lambda i, j: (layer, 0, 0)),
                  pl.BlockSpec((None, 2, LANES), lambda i, j: (layer, 0, 0)),
                  pl.BlockSpec((None, 1, LANES), lambda i, j: (layer, 0, 0))],
        out_specs=pl.BlockSpec((None, ts, nh * LANES), lambda i, j: (i, j, 0)),
        scratch_shapes=[pltpu.VMEM((3 * nh, ts + 8, LANES), F32),
                        pltpu.VMEM((nh, ts, LANES), F32),
                        pltpu.VMEM((nh, ts, LANES), F32),
                        pltpu.VMEM((nh, ts, LANES), F32),
                        pltpu.VMEM((ts, LANES), F32),
                        pltpu.VMEM((ts, LANES), F32),
                        pltpu.VMEM((nh, ts, HEAD_DIM), F32),
                        pltpu.VMEM((nh, ts, HEAD_DIM), BF16),
                        pltpu.VMEM((nh, ts, HEAD_DIM), BF16),
                        pltpu.VMEM((nh, ts, HEAD_DIM), BF16),
                        pltpu.VMEM((nh, ts, GDN_CHUNK), BF16),
                        pltpu.VMEM((nh, HEAD_DIM, HEAD_DIM), F32)],
        compiler_params=_cparams(("parallel", "arbitrary")),
        name="gdn",
    )(p4, p4, conv_w, ab, norm_g)


def _nsa_prep_kernel(kc2_ref, vc2_ref, ksl_ref, vsl_ref, kwn_ref, vwn_ref,
                     w1_ref, w2_ref, pos_ref, kg_ref, bm_ref, one_ref,
                     kc_o, vc_o, ks_o, vs_o, kw_o, vw_o):
    nc = kc2_ref.shape[0]
    half = kc2_ref.shape[1]
    kg = kg_ref[...]
    for kind, (src, dst) in enumerate(((kc2_ref, kc_o), (vc2_ref, vc_o))):
        t2 = src[...].astype(BF16)
        first = _dot(t2, w1_ref[kind, 0:half, :])
        second = _dot(t2, w1_ref[kind, half:2 * half, :])
        const = _dot(pos_ref[kind], w1_ref[kind])[0:1, :]
        pre = first + pltpu.roll(second, nc - 1, 0) + const
        y = _dot(jax.nn.gelu(pre).astype(BF16), w2_ref[kind])
        if kind == 0:
            y = _rms(y, kg)
        dst[...] = y.astype(BF16)
    front = ks_o.shape[0] - ksl_ref.shape[0]
    for dst, val in ((ks_o, _rms(ksl_ref[...], kg)), (kw_o, _rms(kwn_ref[...], kg)),
                     (vs_o, vsl_ref[...]), (vw_o, vwn_ref[...])):
        dst[0:front, 0:LANES] = jnp.zeros((front, LANES), BF16)
        dst[front:, 0:LANES] = val.astype(BF16)
    ks_o[:, LANES:] = bm_ref[...]
    vs_o[:, LANES:] = one_ref[...]
    vw_o[:, LANES:] = one_ref[...]


def _nsa_prep(p4, w1, w2, pos, k_norm, block_mask, ones_col, layer):
    ng, b, s, _ = p4.shape
    padded = s + WINDOW
    nc = s // CMP_STRIDE
    half = CMP_STRIDE * LANES
    kc2 = p4[G_KCMP:G_KCMP + KV_HEADS].reshape(KV_HEADS, b, nc, half)
    vc2 = p4[G_VCMP:G_VCMP + KV_HEADS].reshape(KV_HEADS, b, nc, half)
    grp = lambda base: (lambda i, g: (base + g, i, 0, 0))
    small = jax.ShapeDtypeStruct((b, KV_HEADS, nc, LANES), BF16)
    full = jax.ShapeDtypeStruct((b, KV_HEADS, padded, LANES), BF16)
    wide = jax.ShapeDtypeStruct((b, KV_HEADS, padded, 2 * LANES), BF16)
    out_small = pl.BlockSpec((None, None, nc, LANES), lambda i, g: (i, g, 0, 0))
    out_full = pl.BlockSpec((None, None, padded, LANES), lambda i, g: (i, g, 0, 0))
    out_wide = pl.BlockSpec((None, None, padded, 2 * LANES), lambda i, g: (i, g, 0, 0))
    const = pl.BlockSpec((padded, LANES), lambda i, g: (0, 0))
    return pl.pallas_call(
        _nsa_prep_kernel,
        out_shape=(small, small, wide, wide, full, wide),
        grid=(b, KV_HEADS),
        in_specs=[pl.BlockSpec((None, None, nc, half), grp(0)),
                  pl.BlockSpec((None, None, nc, half), grp(0)),
                  pl.BlockSpec((None, None, s, LANES), grp(G_KSLC)),
                  pl.BlockSpec((None, None, s, LANES), grp(G_VSLC)),
                  pl.BlockSpec((None, None, s, LANES), grp(G_KWIN)),
                  pl.BlockSpec((None, None, s, LANES), grp(G_VWIN)),
                  pl.BlockSpec((None, 2, 2 * half, LANES), lambda i, g: (layer, 0, 0, 0)),
                  pl.BlockSpec((None, 2, LANES, LANES), lambda i, g: (layer, 0, 0, 0)),
                  pl.BlockSpec((None, 2, 8, 2 * half), lambda i, g: (layer, 0, 0, 0)),
                  pl.BlockSpec((None, 1, LANES), lambda i, g: (layer, 0, 0)),
                  const, const],
        out_specs=(out_small, out_small, out_wide, out_wide, out_full, out_wide),
        compiler_params=_cparams(("parallel", "parallel")),
        name="nsa_prep",
    )(kc2, vc2, p4, p4, p4, p4, w1, w2, pos, k_norm, block_mask, ones_col)


def _nsa_kernel(q_ref, gate_ref, kc_ref, vc_ref, ks_ref, vs_ref, kw_ref, vw_ref,
                bc_ref, bs_ref, bw_ref, ov_ref, qg_ref, o_ref, sc_ref, *, n_cmp, n_sel):
    tq = q_ref.shape[1]
    r3 = Q_PER_KV
    i = pl.program_id(2)
    nc = kc_ref.shape[0]
    rows = r3 * tq

    qg = qg_ref[...]
    qf = jnp.concatenate([_rms(q_ref[r], qg) * (HEAD_DIM ** -0.5) for r in range(r3)], axis=0)
    qb = qf.astype(BF16)

    qq = lax.broadcasted_iota(jnp.int32, (tq, tq), 0)
    kk = lax.broadcasted_iota(jnp.int32, (tq, tq), 1)

    s_c = _dot_nt(qb, kc_ref[...]).reshape(r3, tq, nc) + bc_ref[...]
    qpos_c = i * tq + lax.broadcasted_iota(jnp.int32, (tq, nc), 0)
    n_idx = lax.broadcasted_iota(jnp.int32, (tq, nc), 1)
    mask_c = ((qpos_c >= n_idx * CMP_STRIDE + (CMP_LEN - 1)) & (n_idx < n_cmp))[None]
    s_c = jnp.where(mask_c, s_c, NEG_BIG)
    m_c = jnp.max(s_c, axis=-1, keepdims=True)
    p_c = jnp.where(mask_c, jnp.exp(s_c - m_c), 0.0)
    p_c = p_c / jnp.maximum(jnp.sum(p_c, axis=-1, keepdims=True), 1e-30)
    o_c = _dot(p_c.reshape(rows, nc).astype(BF16), vc_ref[...])
    p_sum = p_c[0] + p_c[1] + p_c[2]
    p_hi = p_sum.astype(BF16)
    p_lo = (p_sum - p_hi.astype(F32)).astype(BF16)
    importance = _dot(p_hi, ov_ref[...]) + _dot(p_lo, ov_ref[...])

    cur = (i * tq + qq) // SEL_LEN
    forced = (kk == 0) | (kk == cur) | (kk == cur - 1)
    score = jnp.where(forced, SEL_FORCE, jnp.where(kk <= cur, importance, NEG_BIG))
    n_rows = ((n_sel + 7) // 8) * 8
    st = score.T[0:n_rows, :]
    sub = 8
    slabs = [st[v * sub:(v + 1) * sub, :] for v in range(n_rows // sub)]
    blk = lax.broadcasted_iota(jnp.int32, (sub, tq), 0)
    ranks = [jnp.zeros((sub, tq), F32) for _ in slabs]
    for b in range(n_sel):
        row = st[b:b + 1, :]
        for v, slab in enumerate(slabs):
            if v * sub > b:
                beats = row >= slab
            elif (v + 1) * sub - 1 <= b:
                beats = row > slab
            else:
                beats = (row > slab) | ((row == slab) & (blk + v * sub > b))
            ranks[v] = ranks[v] + jnp.where(beats, 1.0, 0.0)
    rank = jnp.concatenate(ranks, axis=0)
    drop_t = jnp.where((rank < float(min(SEL_TOPK, n_sel))) & (st > 0.5 * NEG_BIG), 0.0, 1.0)
    if n_rows < tq:
        drop_t = jnp.concatenate([drop_t, jnp.zeros((tq - n_rows, tq), F32)], axis=0)
    drop = drop_t.T.astype(BF16)
    q_sel = jnp.concatenate([qb, jnp.concatenate([drop] * r3, axis=0)], axis=1)

    def normalise(acc):
        den = jnp.maximum(acc[:, HEAD_DIM:HEAD_DIM + 1], 1e-30)
        return (acc[:, 0:HEAD_DIM] / den).reshape(r3, tq, HEAD_DIM)

    band = (N_WIN_PREV + 1) * tq
    front = N_WIN_PREV * tq
    far_keys = sc_ref.shape[-1]
    boff = pl.multiple_of(i * tq, tq)
    q_b = lax.broadcasted_iota(jnp.int32, (tq, band), 0)
    k_b = lax.broadcasted_iota(jnp.int32, (tq, band), 1)

    def lane_fold(x, op):
        out = x[..., 0:tq]
        for c in range(1, x.shape[-1] // tq):
            out = op(out, x[..., c * tq:(c + 1) * tq])
        return out

    n_far = jnp.maximum(i - 1, 0) // (far_keys // tq)

    def selected(nf):
        first_near = front - (i * tq - nf * far_keys)
        ok_s = (k_b <= q_b + front) & (k_b >= first_near)
        def group(pieces, with_band):
            keys_of = lambda c, lo, hi: slice(front + c * far_keys + lo, front + c * far_keys + hi)
            mx = None
            if with_band:
                s_b = (_dot_nt(q_sel, ks_ref[pl.ds(boff, band), :]).reshape(r3, tq, band) + bs_ref[...]
                       + jnp.where(ok_s, 0.0, NEG_BIG)[None])
                mx = lane_fold(s_b, jnp.maximum)
            for c, lo, hi in pieces:
                s = _dot_nt(q_sel, ks_ref[keys_of(c, lo, hi), :]).reshape(r3, tq, hi - lo)
                sc_ref[c, :, :, lo:hi] = s
                fold = lane_fold(s, jnp.maximum)
                mx = fold if mx is None else jnp.maximum(mx, fold)
            m = jnp.max(mx, axis=-1, keepdims=True)
            acc = None
            if with_band:
                acc = _dot(jnp.exp(s_b - m).reshape(rows, band).astype(BF16), vs_ref[pl.ds(boff, band), :])
            for c, lo, hi in pieces:
                p = jnp.exp(sc_ref[c, :, :, lo:hi] - m).reshape(rows, hi - lo).astype(BF16)
                part = _dot(p, vs_ref[keys_of(c, lo, hi), :])
                acc = part if acc is None else acc + part
            return m.reshape(rows, 1), acc

        if nf == 0:
            return normalise(group([], True)[1])
        whole = lambda cs: [(c, 0, far_keys) for c in cs]
        if nf % 2:
            first, second = whole(range((nf + 1) // 2)), whole(range((nf + 1) // 2, nf))
        else:
            mid = nf // 2
            first = whole(range(mid)) + [(mid, 0, far_keys // 2)]
            second = [(mid, far_keys // 2, far_keys)] + whole(range(mid + 1, nf))
        m_b, acc_b = group(second, True)
        m_a, acc_a = group(first, False)
        m_s = jnp.maximum(m_a, m_b)
        return normalise(acc_a * jnp.exp(m_a - m_s) + acc_b * jnp.exp(m_b - m_s))

    ok_w = (k_b > q_b) & (k_b <= q_b + front) & (k_b >= front - i * tq)
    s_w = (_dot_nt(qb, kw_ref[pl.ds(boff, band), :]).reshape(r3, tq, band) + bw_ref[...]
           + jnp.where(ok_w, 0.0, NEG_BIG)[None])
    p_w = jnp.exp(s_w - jnp.max(s_w, axis=-1, keepdims=True))
    o_w = normalise(_dot(p_w.reshape(rows, band).astype(BF16), vw_ref[pl.ds(boff, band), :]))

    gates = jax.nn.sigmoid(gate_ref[...])
    o_c = o_c.reshape(r3, tq, HEAD_DIM)
    col = lambda branch, r: gates[:, LANE_GATE + branch * r3 + r:LANE_GATE + branch * r3 + r + 1]
    partial = [col(0, r) * o_c[r] + col(2, r) * o_w[r] for r in range(r3)]
    gate_s = [col(1, r) for r in range(r3)]

    for nf in range(sc_ref.shape[0]):
        @pl.when(n_far == nf)
        def _(nf=nf):
            o_s = selected(nf)
            for r in range(r3):
                o_ref[:, r * HEAD_DIM:(r + 1) * HEAD_DIM] = (partial[r] + gate_s[r] * o_s[r]).astype(BF16)


def _nsa(p4, prep, bias_c, bias_sel, bias_win, overlap, q_norm, layer, tq=WIN_BLOCK, far_keys=512):
    _, b, s, _ = p4.shape
    kc, vc, ks, vs, kw, vw = prep
    nc = kc.shape[2]
    r3 = Q_PER_KV
    n_sel = s // SEL_LEN
    n_cmp = (s - CMP_LEN) // CMP_STRIDE + 1
    band = (N_WIN_PREV + 1) * tq
    padded = ks.shape[2]
    kv_small = pl.BlockSpec((None, None, nc, LANES), lambda i, g, j: (i, g, 0, 0))
    kv_full = pl.BlockSpec((None, None, padded, LANES), lambda i, g, j: (i, g, 0, 0))
    kv_wide = pl.BlockSpec((None, None, padded, 2 * LANES), lambda i, g, j: (i, g, 0, 0))
    return pl.pallas_call(
        functools.partial(_nsa_kernel, n_cmp=n_cmp, n_sel=n_sel),
        out_shape=jax.ShapeDtypeStruct((b, s, N_HEADS_C * HEAD_DIM), BF16),
        grid=(b, KV_HEADS, s // tq),
        in_specs=[pl.BlockSpec((r3, None, tq, LANES), lambda i, g, j: (G_QC // r3 + g, i, j, 0)),
                  pl.BlockSpec((None, None, tq, LANES), lambda i, g, j: (G_SMALL + g, i, j, 0)),
                  kv_small, kv_small, kv_wide, kv_wide, kv_full, kv_wide,
                  pl.BlockSpec((r3, tq, nc), lambda i, g, j: (g, j, 0)),
                  pl.BlockSpec((r3, tq, band), lambda i, g, j: (g, 0, 0)),
                  pl.BlockSpec((r3, tq, band), lambda i, g, j: (g, 0, 0)),
                  pl.BlockSpec((nc, LANES), lambda i, g, j: (0, 0)),
                  pl.BlockSpec((None, 1, LANES), lambda i, g, j: (layer, 0, 0))],
        out_specs=pl.BlockSpec((None, tq, r3 * HEAD_DIM), lambda i, g, j: (i, j, g)),
        scratch_shapes=[pltpu.VMEM((s // far_keys, r3, tq, far_keys), F32)],
        compiler_params=_cparams(("parallel", "parallel", "arbitrary")),
        name="nsa",
    )(p4, p4, kc, vc, ks, vs, kw, vw, bias_c, bias_sel, bias_win, overlap, q_norm)


def _t5_bucket(dist):
    n = jnp.maximum(dist, 0)
    max_exact = RPB_BUCKETS // 2
    log_ratio = jnp.log(jnp.maximum(n, 1).astype(F32) / max_exact) / math.log(RPB_MAX_DIST / max_exact)
    large = jnp.minimum(max_exact + (log_ratio * (RPB_BUCKETS - max_exact)).astype(jnp.int32), RPB_BUCKETS - 1)
    return jnp.where(n < max_exact, n, large)


def _bias_tables(rel_bias, s, tq):
    table = rel_bias.astype(F32)
    buckets = jnp.arange(RPB_BUCKETS, dtype=jnp.int32)[:, None, None]

    def look(dist):
        onehot = (_t5_bucket(dist)[None] == buckets).astype(F32)
        return jnp.einsum("nh,nqk->hqk", table, onehot, precision=lax.Precision.HIGHEST)

    nc = s // CMP_STRIDE
    pos = jnp.arange(s, dtype=jnp.int32)
    cmp_end = jnp.arange(nc, dtype=jnp.int32) * CMP_STRIDE + (CMP_LEN - 1)
    bias_c = look(pos[:, None] - cmp_end[None, :])
    q = jnp.arange(tq, dtype=jnp.int32)[:, None]
    kb = jnp.arange((N_WIN_PREV + 1) * tq, dtype=jnp.int32)[None, :]
    bias_win = look(N_WIN_PREV * tq + q - kb)
    bias_sel = bias_win - table[RPB_BUCKETS - 1][:, None, None]
    return bias_c, bias_sel, bias_win


def _selection_constants(s, tq):
    nc = s // CMP_STRIDE
    n_cmp = (s - CMP_LEN) // CMP_STRIDE + 1
    n_sel = s // SEL_LEN
    cmp_start = np.arange(nc) * CMP_STRIDE
    sel_start = np.arange(LANES) * SEL_LEN
    overlap = ((cmp_start[:, None] < sel_start[None, :] + SEL_LEN)
               & (cmp_start[:, None] + CMP_LEN > sel_start[None, :])
               & (np.arange(nc)[:, None] < n_cmp) & (np.arange(LANES)[None, :] < n_sel))
    key_pos = np.arange(-WINDOW, s)[:, None]
    in_block = (np.arange(LANES)[None, :] == (key_pos // SEL_LEN)) & (key_pos >= 0)
    block_mask = jnp.where(jnp.asarray(in_block), NEG_BIG, 0.0).astype(BF16)
    ones_col = jnp.asarray((np.arange(LANES)[None, :] == 0) & (key_pos >= 0), BF16)
    return jnp.asarray(overlap, BF16), block_mask, ones_col


def _arrange_w_in(w_in):
    depth, d, _ = w_in.shape
    da, db, dc, dkv = N_HEADS_A * 128, N_HEADS_B * 128, N_HEADS_C * 128, KV_HEADS * 128
    o_ba = 4 * da
    o_aa = o_ba + N_HEADS_A
    o_ub = o_aa + N_HEADS_A
    o_qc = o_ub + 2 * db
    o_kc = o_qc + dc
    o_gc = o_kc + 6 * dkv
    sl = lambda lo, n: w_in[:, :, lo:lo + n]
    gate = w_in[:, :, o_gc:o_gc + 3 * N_HEADS_C].reshape(depth, d, 3, KV_HEADS, Q_PER_KV)
    zeros = lambda n: jnp.zeros((depth, d, n), w_in.dtype)
    small0 = jnp.concatenate([sl(o_ba, 2 * N_HEADS_A), gate[:, :, :, 0, :].reshape(depth, d, 3 * Q_PER_KV),
                              zeros(LANES - 2 * N_HEADS_A - 3 * Q_PER_KV)], axis=-1)
    small1 = jnp.concatenate([zeros(LANE_GATE), gate[:, :, :, 1, :].reshape(depth, d, 3 * Q_PER_KV),
                              zeros(LANES - LANE_GATE - 3 * Q_PER_KV)], axis=-1)
    parts = [sl(0, 4 * da),
             sl(o_qc, dc),
             sl(o_kc, dkv),
             sl(o_ub, 2 * db),
             sl(o_kc + dkv, 5 * dkv),
             small0, small1]
    return jnp.concatenate(parts, axis=-1).astype(BF16)


def kernel(x, attn_norm, w_in, conv_a, a_log, dt_bias, gdn_norm, sgu_ln_g, sgu_ln_b, sgu_w, sgu_b,
           nsa_q_norm, nsa_k_norm, cmp_pos, cmp_w1, cmp_w2, rel_bias, w_out, mlp_norm, w_up, w_down):
    b, s, d = x.shape
    depth = w_in.shape[0]
    t = b * s
    tq = 128

    w_in_r = _arrange_w_in(w_in)
    w_out_b = w_out.astype(BF16)
    w_up_b = w_up.astype(BF16)
    w_down_b = w_down.astype(BF16)
    attn_g = attn_norm.reshape(depth, 1, d)
    mlp_g = mlp_norm.reshape(depth, 1, d)
    conv_r = conv_a.reshape(depth, CONV_WIDTH * 3 * N_HEADS_A, LANES)
    pad_to = lambda v, lo: jnp.pad(v, ((0, 0), (lo, LANES - lo - v.shape[1])))
    ab = jnp.stack([pad_to(a_log, LANE_DECAY), pad_to(dt_bias, LANE_DECAY)], axis=1)
    gdn_g = gdn_norm.reshape(depth, 1, LANES)
    ln_g = sgu_ln_g.reshape(depth, N_HEADS_B, LANES)
    ln_b = sgu_ln_b.reshape(depth, N_HEADS_B, LANES)
    sgu_bias = jnp.broadcast_to(sgu_b[..., None], sgu_b.shape + (LANES,))
    q_g = nsa_q_norm.reshape(depth, 1, LANES)
    k_g = nsa_k_norm.reshape(depth, 1, LANES)
    w1_b = cmp_w1.astype(BF16)
    w2_b = cmp_w2.astype(BF16)
    pos_b = jnp.broadcast_to(cmp_pos.reshape(depth, 2, 1, CMP_LEN * LANES),
                             (depth, 2, 8, CMP_LEN * LANES)).astype(BF16)
    bias_c, bias_sel, bias_win = _bias_tables(rel_bias, s, tq)
    overlap, block_mask, ones_col = _selection_constants(s, tq)

    x2 = x.reshape(t, d)
    for layer in range(depth):
        p = _inproj(x2, attn_g, w_in_r, layer)
        p4 = p.reshape(N_GROUPS, b, s, LANES)
        mix_a = _gdn(p4, conv_r, ab, gdn_g, layer)
        mix_b = _sgu(p4, ln_g, ln_b, sgu_w, sgu_bias, layer)
        prep = _nsa_prep(p4, w1_b, w2_b, pos_b, k_g, block_mask, ones_col, layer)
        mix_c = _nsa(p4, prep, bias_c, bias_sel, bias_win, overlap, q_g, layer, tq=tq)
        x2 = _outproj(x2, mix_a.reshape(t, -1), mix_b.reshape(t, -1), mix_c.reshape(t, -1), w_out_b, layer)
        x2 = _mlp(x2, mlp_g, w_up_b, w_down_b, layer)
    return x2.reshape(b, s, d)
```

```python
import functools
import math

import numpy as np
import jax
import jax.numpy as jnp
from jax import lax
from jax.experimental import pallas as pl
from jax.experimental.pallas import tpu as pltpu

F32 = jnp.float32
BF16 = jnp.bfloat16

LANES = 128
HEAD_DIM = 128
N_HEADS_A = 6
N_HEADS_B = 4
N_HEADS_C = 6
KV_HEADS = 2
Q_PER_KV = N_HEADS_C // KV_HEADS
CONV_WIDTH = 4
GDN_CHUNK = 64
SGU_CHUNK = 128
CMP_LEN = 32
CMP_STRIDE = 16
SEL_LEN = 64
SEL_TOPK = 16
WINDOW = 512
WIN_BLOCK = 128
N_WIN_PREV = WINDOW // WIN_BLOCK
RPB_BUCKETS = 32
RPB_MAX_DIST = 128
NORM_EPS = 1e-6
NEG_BIG = -1e30
SEL_FORCE = 1e9
VMEM_LIMIT = 56 * 1024 * 1024

G_QA, G_KA, G_VA, G_ZA = 0, 6, 12, 18
G_QC = 24
G_KCMP = 30
G_UB, G_VB = 32, 36
G_VCMP = 40
G_KSLC, G_VSLC, G_KWIN, G_VWIN = 42, 44, 46, 48
G_SMALL = 50
N_GROUPS = 52
LANE_BETA, LANE_DECAY, LANE_GATE = 0, 6, 12


def _dot(a, b, precision=None):
    return jnp.dot(a, b, preferred_element_type=F32, precision=precision)


def _dot_nt(a, b):
    return lax.dot_general(a, b, (((1,), (1,)), ((), ())), preferred_element_type=F32)


def _dot_tn(a, b):
    return lax.dot_general(a, b, (((0,), (0,)), ((), ())), preferred_element_type=F32)


def _rms(x, gain):
    return x * lax.rsqrt(jnp.mean(x * x, axis=-1, keepdims=True) + NORM_EPS) * gain


def _cparams(sem):
    return pltpu.CompilerParams(dimension_semantics=sem, vmem_limit_bytes=VMEM_LIMIT)


def _inproj_kernel(x_ref, g_ref, w_ref, o_ref, h_ref):
    @pl.when(pl.program_id(1) == 0)
    def _():
        h_ref[...] = _rms(x_ref[...], g_ref[...]).astype(BF16)

    r = _dot(h_ref[...], w_ref[...])
    for c in range(o_ref.shape[0]):
        o_ref[c] = r[:, c * LANES:(c + 1) * LANES]


def _inproj(x2, gain, w, layer, tm=512, tn=13 * LANES):
    t, d = x2.shape
    n = w.shape[-1]
    return pl.pallas_call(
        _inproj_kernel,
        out_shape=jax.ShapeDtypeStruct((n // LANES, t, LANES), F32),
        grid=(t // tm, n // tn),
        in_specs=[pl.BlockSpec((tm, d), lambda i, j: (i, 0)),
                  pl.BlockSpec((None, 1, d), lambda i, j: (layer, 0, 0)),
                  pl.BlockSpec((None, d, tn), lambda i, j: (layer, 0, j))],
        out_specs=pl.BlockSpec((tn // LANES, tm, LANES), lambda i, j: (j, i, 0)),
        scratch_shapes=[pltpu.VMEM((tm, d), BF16)],
        compiler_params=_cparams(("parallel", "arbitrary")),
        name="inproj",
    )(x2, gain, w)


def _outproj_kernel(x_ref, a_ref, b_ref, c_ref, w_ref, o_ref):
    da, db = a_ref.shape[1], b_ref.shape[1]
    acc = _dot(a_ref[...], w_ref[0:da, :])
    acc += _dot(b_ref[...], w_ref[da:da + db, :])
    acc += _dot(c_ref[...], w_ref[da + db:, :])
    o_ref[...] = x_ref[...] + acc


def _outproj(x2, ma, mb, mc, w_out, layer, tm=512):
    t, d = x2.shape
    da, db, dc = ma.shape[1], mb.shape[1], mc.shape[1]
    return pl.pallas_call(
        _outproj_kernel,
        out_shape=jax.ShapeDtypeStruct((t, d), F32),
        grid=(t // tm,),
        in_specs=[pl.BlockSpec((tm, d), lambda i: (i, 0)),
                  pl.BlockSpec((tm, da), lambda i: (i, 0)),
                  pl.BlockSpec((tm, db), lambda i: (i, 0)),
                  pl.BlockSpec((tm, dc), lambda i: (i, 0)),
                  pl.BlockSpec((None, da + db + dc, d), lambda i: (layer, 0, 0))],
        out_specs=pl.BlockSpec((tm, d), lambda i: (i, 0)),
        compiler_params=_cparams(("parallel",)),
        name="outproj",
    )(x2, ma, mb, mc, w_out)


def _mlp_kernel(x_ref, g_ref, wu_ref, wd_ref, o_ref, h_ref, acc_ref):
    k = pl.program_id(1)

    @pl.when(k == 0)
    def _():
        h_ref[...] = _rms(x_ref[...], g_ref[...]).astype(BF16)
        acc_ref[...] = jnp.zeros_like(acc_ref)

    a = jnp.maximum(_dot(h_ref[...], wu_ref[...]), 0.0)
    acc_ref[...] += _dot((a * a).astype(BF16), wd_ref[...])

    @pl.when(k == pl.num_programs(1) - 1)
    def _():
        o_ref[...] = x_ref[...] + acc_ref[...]


def _mlp(x2, gain, w_up, w_down, layer, tm=512, tf=1024):
    t, d = x2.shape
    f = w_up.shape[-1]
    return pl.pallas_call(
        _mlp_kernel,
        out_shape=jax.ShapeDtypeStruct((t, d), F32),
        grid=(t // tm, f // tf),
        in_specs=[pl.BlockSpec((tm, d), lambda i, k: (i, 0)),
                  pl.BlockSpec((None, 1, d), lambda i, k: (layer, 0, 0)),
                  pl.BlockSpec((None, d, tf), lambda i, k: (layer, 0, k)),
                  pl.BlockSpec((None, tf, d), lambda i, k: (layer, k, 0))],
        out_specs=pl.BlockSpec((tm, d), lambda i, k: (i, 0)),
        scratch_shapes=[pltpu.VMEM((tm, d), BF16), pltpu.VMEM((tm, d), F32)],
        compiler_params=_cparams(("parallel", "arbitrary")),
        name="mlp",
    )(x2, gain, w_up, w_down)


def _sgu_kernel(p_ref, lg_ref, lb_ref, w_ref, b_ref, o_ref):
    ts = p_ref.shape[1]
    nb = N_HEADS_B
    v = [jax.nn.gelu(p_ref[nb + g]) for g in range(nb)]
    width = float(nb * LANES)
    mu = sum(jnp.sum(vg, axis=-1, keepdims=True) for vg in v) / width
    var = sum(jnp.sum(jnp.square(vg - mu), axis=-1, keepdims=True) for vg in v) / width
    inv = lax.rsqrt(var + NORM_EPS)
    row = lax.broadcasted_iota(jnp.int32, (SGU_CHUNK, SGU_CHUNK), 0)
    col = lax.broadcasted_iota(jnp.int32, (SGU_CHUNK, SGU_CHUNK), 1)
    causal = col <= row
    for g in range(nb):
        vn = ((v[g] - mu) * inv * lg_ref[g:g + 1, :] + lb_ref[g:g + 1, :]).astype(BF16)
        wg = jnp.where(causal, w_ref[g], 0.0).astype(BF16)
        for c in range(ts // SGU_CHUNK):
            rows = slice(c * SGU_CHUNK, (c + 1) * SGU_CHUNK)
            mixed = _dot(wg, vn[rows]) + b_ref[g]
            u = jax.nn.gelu(p_ref[g, rows, :])
            o_ref[rows, g * LANES:(g + 1) * LANES] = (u * mixed).astype(BF16)


def _sgu(p4, ln_g, ln_b, w_s, b_s, layer, ts=512):
    _, b, s, _ = p4.shape
    nb = N_HEADS_B
    return pl.pallas_call(
        _sgu_kernel,
        out_shape=jax.ShapeDtypeStruct((b, s, nb * LANES), BF16),
        grid=(b, s // ts),
        in_specs=[pl.BlockSpec((2 * nb, None, ts, LANES), lambda i, j: (G_UB // (2 * nb), i, j, 0)),
                  pl.BlockSpec((None, nb, LANES), lambda i, j: (layer, 0, 0)),
                  pl.BlockSpec((None, nb, LANES), lambda i, j: (layer, 0, 0)),
                  pl.BlockSpec((None, nb, SGU_CHUNK, SGU_CHUNK), lambda i, j: (layer, 0, 0, 0)),
                  pl.BlockSpec((None, nb, SGU_CHUNK, LANES), lambda i, j: (layer, 0, 0, 0))],
        out_specs=pl.BlockSpec((None, ts, nb * LANES), lambda i, j: (i, j, 0)),
        compiler_params=_cparams(("parallel", "parallel")),
        name="sgu",
    )(p4, ln_g, ln_b, w_s, b_s)


def _gdn_kernel(a_ref, sm_ref, cw_ref, ab_ref, ng_ref, o_ref,
                xbuf, q_s, k_s, v_s, beta_s, gcum_s, u_s, w_s, qg_s, kd_s, a_s, st_ref, *, batch):
    ts = a_ref.shape[1]
    ck = GDN_CHUNK
    nh = N_HEADS_A
    pad = 8

    @pl.when(pl.program_id(1) == 0)
    def _():
        xbuf[:, 0:pad, :] = jnp.zeros((3 * nh, pad, LANES), F32)
        st_ref[...] = jnp.zeros_like(st_ref)

    xbuf[:, pad:pad + ts, :] = a_ref[0:3 * nh]
    dests = (q_s, k_s, v_s)
    for grp in range(3 * nh):
        acc = None
        for j in range(CONV_WIDTH):
            lo = pad - (CONV_WIDTH - 1) + j
            term = xbuf[grp, lo:lo + ts, :] * cw_ref[j * 3 * nh + grp:j * 3 * nh + grp + 1, :]
            acc = term if acc is None else acc + term
        y = acc * jax.nn.sigmoid(acc)
        kind, h = divmod(grp, nh)
        if kind < 2:
            y = y * lax.rsqrt(jnp.sum(y * y, axis=-1, keepdims=True) + NORM_EPS)
        dests[kind][h] = y
    xbuf[:, 0:pad, :] = xbuf[:, ts:ts + pad, :]

    sm = sm_ref[...]
    beta_s[...] = jax.nn.sigmoid(sm)
    z = sm + ab_ref[1:2, :]
    softplus = jnp.maximum(z, 0.0) + jnp.log1p(jnp.exp(-jnp.abs(z)))
    g_all = -jnp.exp(ab_ref[0:1, :]) * softplus

    ii = lax.broadcasted_iota(jnp.int32, (ck, ck), 0)
    jj = lax.broadcasted_iota(jnp.int32, (ck, ck), 1)
    incl = jj <= ii
    strict = jj < ii
    tri = jnp.where(incl, 1.0, 0.0).astype(F32)
    for c in range(ts // ck):
        rows = slice(c * ck, (c + 1) * ck)
        gcum_s[rows, :] = _dot(tri, g_all[rows], precision=lax.Precision.HIGHEST)

    scale = HEAD_DIM ** -0.5
    gain = ng_ref[...]
    bf = lambda m: m.astype(BF16)
    same = lambda size: (ii // size) == (jj // size)
    n_chunks = ts // ck
    rows_of = lambda c: slice(c * ck, (c + 1) * ck)
    dcol = lambda h: slice(LANE_DECAY + h, LANE_DECAY + h + 1)

    for c0 in range(0, n_chunks, batch):
        gc = {c: gcum_s[rows_of(c), :] for c in range(c0, c0 + batch)}
        gct = {c: gc[c].T for c in gc}
        items = [(c, h) for c in range(c0, c0 + batch) for h in range(nh)]
        kk, qk = {}, {}
        for c, h in items:
            g_col = gc[c][:, dcol(h)]
            q = q_s[h, rows_of(c), :] * scale
            k = k_s[h, rows_of(c), :]
            kbf = bf(k)
            kk[c, h] = _dot_nt(bf(k * beta_s[rows_of(c), LANE_BETA + h:LANE_BETA + h + 1]), kbf)
            qk[c, h] = _dot_nt(bf(q), kbf)
            qg_s[h, rows_of(c), :] = bf(q * jnp.exp(g_col))
            kd_s[h, rows_of(c), :] = bf(k * jnp.exp(gc[c][ck - 1:ck, dcol(h)] - g_col))
        lower, diag, p1 = {}, {}, {}
        for it in items:
            c, h = it
            decay = jnp.exp(jnp.where(incl, gc[c][:, dcol(h)] - gct[c][dcol(h), :], NEG_BIG))
            lower[it] = jnp.where(strict, kk[it] * decay, 0.0)
            a_s[h, rows_of(c), :] = bf(jnp.where(incl, qk[it] * decay, 0.0))
            diag[it] = jnp.where(same(8), lower[it], 0.0)
            p1[it] = _dot(bf(diag[it]), bf(diag[it]))
        p2 = {it: _dot(bf(p1[it]), bf(p1[it])) for it in items}
        dp1 = {it: _dot(bf(diag[it]), bf(p1[it])) for it in items}
        na = {it: p1[it] - diag[it] - dp1[it] for it in items}
        nap2 = {it: _dot(bf(na[it]), bf(p2[it])) for it in items}
        n = {it: na[it] + p2[it] + nap2[it] for it in items}
        size = 8
        while size < ck:
            off = same(2 * size) & ~same(size)
            cm = {it: jnp.where(off, lower[it], 0.0) for it in items}
            y = {it: cm[it] + _dot(bf(n[it]), bf(cm[it])) for it in items}
            yn = {it: _dot(bf(y[it]), bf(n[it])) for it in items}
            n = {it: n[it] - (y[it] + yn[it]) for it in items}
            size *= 2
        for it in items:
            c, h = it
            beta = beta_s[rows_of(c), LANE_BETA + h:LANE_BETA + h + 1]
            nb = bf(n[it])
            vb = v_s[h, rows_of(c), :] * beta
            kw = k_s[h, rows_of(c), :] * (beta * jnp.exp(gc[c][:, dcol(h)]))
            u_s[h, rows_of(c), :] = vb + _dot(nb, bf(vb))
            w_s[h, rows_of(c), :] = bf(kw + _dot(nb, bf(kw)))

    heads = range(nh)
    for c in range(n_chunks):
        rows = rows_of(c)
        state = [st_ref[h] for h in heads]
        sb = [bf(s) for s in state]
        ws = [_dot(w_s[h, rows, :], sb[h]) for h in heads]
        qs = [_dot(qg_s[h, rows, :], sb[h]) for h in heads]
        vnb = [bf(u_s[h, rows, :] - ws[h]) for h in heads]
        kv = [_dot_tn(kd_s[h, rows, :], vnb[h]) for h in heads]
        av = [_dot(a_s[h, rows, :], vnb[h]) for h in heads]
        for h in heads:
            g_last = gcum_s[(c + 1) * ck - 1:(c + 1) * ck, dcol(h)]
            st_ref[h] = state[h] * jnp.exp(g_last) + kv[h]
            zg = a_ref[3 * nh + h, rows, :]
            out = _rms(qs[h] + av[h], gain) * (zg * jax.nn.sigmoid(zg))
            o_ref[rows, h * LANES:(h + 1) * LANES] = out.astype(BF16)


def _gdn(p4, conv_w, ab, norm_g, layer, ts=512, batch=4):
    _, b, s, _ = p4.shape
    nh = N_HEADS_A
    return pl.pallas_call(
        functools.partial(_gdn_kernel, batch=batch),
        out_shape=jax.ShapeDtypeStruct((b, s, nh * LANES), BF16),
        grid=(b, s // ts),
        in_specs=[pl.BlockSpec((4 * nh, None, ts, LANES), lambda i, j: (0, i, j, 0)),
                  pl.BlockSpec((None, None, ts, LANES), lambda i, j: (G_SMALL, i, j, 0)),
                  pl.BlockSpec((None, CONV_WIDTH * 3 * nh, LANES), lambda i, j: (layer, 0, 0)),
                  pl.BlockSpec((None, 2, LANES), lambda i, j: (layer, 0, 0)),
                  pl.BlockSpec((None, 1, LANES), lambda i, j: (layer, 0, 0))],
        out_specs=pl.BlockSpec((None, ts, nh * LANES), lambda i, j: (i, j, 0)),
        scratch_shapes=[pltpu.VMEM((3 * nh, ts + 8, LANES), F32),
                        pltpu.VMEM((nh, ts, LANES), F32),
                        pltpu.VMEM((nh, ts, LANES), F32),
                        pltpu.VMEM((nh, ts, LANES), F32),
                        pltpu.VMEM((ts, LANES), F32),
                        pltpu.VMEM((ts, LANES), F32),
                        pltpu.VMEM((nh, ts, HEAD_DIM), F32),
                        pltpu.VMEM((nh, ts, HEAD_DIM), BF16),
                        pltpu.VMEM((nh, ts, HEAD_DIM), BF16),
                        pltpu.VMEM((nh, ts, HEAD_DIM), BF16),
                        pltpu.VMEM((nh, ts, GDN_CHUNK), BF16),
                        pltpu.VMEM((nh, HEAD_DIM, HEAD_DIM), F32)],
        compiler_params=_cparams(("parallel", "arbitrary")),
        name="gdn",
    )(p4, p4, conv_w, ab, norm_g)


def _nsa_prep_kernel(kc2_ref, vc2_ref, ksl_ref, vsl_ref, kwn_ref, vwn_ref,
                     w1_ref, w2_ref, pos_ref, kg_ref, bm_ref, one_ref,
                     kc_o, vc_o, ks_o, vs_o, kw_o, vw_o):
    nc = kc2_ref.shape[0]
    half = kc2_ref.shape[1]
    kg = kg_ref[...]
    for kind, (src, dst) in enumerate(((kc2_ref, kc_o), (vc2_ref, vc_o))):
        t2 = src[...].astype(BF16)
        first = _dot(t2, w1_ref[kind, 0:half, :])
        second = _dot(t2, w1_ref[kind, half:2 * half, :])
        const = _dot(pos_ref[kind], w1_ref[kind])[0:1, :]
        pre = first + pltpu.roll(second, nc - 1, 0) + const
        y = _dot(jax.nn.gelu(pre).astype(BF16), w2_ref[kind])
        if kind == 0:
            y = _rms(y, kg)
        dst[...] = y.astype(BF16)
    front = ks_o.shape[0] - ksl_ref.shape[0]
    for dst, val in ((ks_o, _rms(ksl_ref[...], kg)), (kw_o, _rms(kwn_ref[...], kg)),
                     (vs_o, vsl_ref[...]), (vw_o, vwn_ref[...])):
        dst[0:front, 0:LANES] = jnp.zeros((front, LANES), BF16)
        dst[front:, 0:LANES] = val.astype(BF16)
    ks_o[:, LANES:] = bm_ref[...]
    vs_o[:, LANES:] = one_ref[...]
    vw_o[:, LANES:] = one_ref[...]


def _nsa_prep(p4, w1, w2, pos, k_norm, block_mask, ones_col, layer):
    ng, b, s, _ = p4.shape
    padded = s + WINDOW
    nc = s // CMP_STRIDE
    half = CMP_STRIDE * LANES
    kc2 = p4[G_KCMP:G_KCMP + KV_HEADS].reshape(KV_HEADS, b, nc, half)
    vc2 = p4[G_VCMP:G_VCMP + KV_HEADS].reshape(KV_HEADS, b, nc, half)
    grp = lambda base: (lambda i, g: (base + g, i, 0, 0))
    small = jax.ShapeDtypeStruct((b, KV_HEADS, nc, LANES), BF16)
    full = jax.ShapeDtypeStruct((b, KV_HEADS, padded, LANES), BF16)
    wide = jax.ShapeDtypeStruct((b, KV_HEADS, padded, 2 * LANES), BF16)
    out_small = pl.BlockSpec((None, None, nc, LANES), lambda i, g: (i, g, 0, 0))
    out_full = pl.BlockSpec((None, None, padded, LANES), lambda i, g: (i, g, 0, 0))
    out_wide = pl.BlockSpec((None, None, padded, 2 * LANES), lambda i, g: (i, g, 0, 0))
    const = pl.BlockSpec((padded, LANES), lambda i, g: (0, 0))
    return pl.pallas_call(
        _nsa_prep_kernel,
        out_shape=(small, small, wide, wide, full, wide),
        grid=(b, KV_HEADS),
        in_specs=[pl.BlockSpec((None, None, nc, half), grp(0)),
                  pl.BlockSpec((None, None, nc, half), grp(0)),
                  pl.BlockSpec((None, None, s, LANES), grp(G_KSLC)),
                  pl.BlockSpec((None, None, s, LANES), grp(G_VSLC)),
                  pl.BlockSpec((None, None, s, LANES), grp(G_KWIN)),
                  pl.BlockSpec((None, None, s, LANES), grp(G_VWIN)),
                  pl.BlockSpec((None, 2, 2 * half, LANES), lambda i, g: (layer, 0, 0, 0)),
                  pl.BlockSpec((None, 2, LANES, LANES), lambda i, g: (layer, 0, 0, 0)),
                  pl.BlockSpec((None, 2, 8, 2 * half), lambda i, g: (layer, 0, 0, 0)),
                  pl.BlockSpec((None, 1, LANES), lambda i, g: (layer, 0, 0)),
                  const, const],
        out_specs=(out_small, out_small, out_wide, out_wide, out_full, out_wide),
        compiler_params=_cparams(("parallel", "parallel")),
        name="nsa_prep",
    )(kc2, vc2, p4, p4, p4, p4, w1, w2, pos, k_norm, block_mask, ones_col)


def _nsa_kernel(q_ref, gate_ref, kc_ref, vc_ref, ks_ref, vs_ref, kw_ref, vw_ref,
                bc_ref, bs_ref, bw_ref, ov_ref, qg_ref, o_ref, sc_ref, *, n_cmp, n_sel):
    tq = q_ref.shape[1]
    r3 = Q_PER_KV
    i = pl.program_id(2)
    nc = kc_ref.shape[0]
    rows = r3 * tq

    qg = qg_ref[...]
    qf = jnp.concatenate([_rms(q_ref[r], qg) * (HEAD_DIM ** -0.5) for r in range(r3)], axis=0)
    qb = qf.astype(BF16)

    qq = lax.broadcasted_iota(jnp.int32, (tq, tq), 0)
    kk = lax.broadcasted_iota(jnp.int32, (tq, tq), 1)

    s_c = _dot_nt(qb, kc_ref[...]).reshape(r3, tq, nc) + bc_ref[...]
    qpos_c = i * tq + lax.broadcasted_iota(jnp.int32, (tq, nc), 0)
    n_idx = lax.broadcasted_iota(jnp.int32, (tq, nc), 1)
    mask_c = ((qpos_c >= n_idx * CMP_STRIDE + (CMP_LEN - 1)) & (n_idx < n_cmp))[None]
    s_c = jnp.where(mask_c, s_c, NEG_BIG)
    m_c = jnp.max(s_c, axis=-1, keepdims=True)
    p_c = jnp.where(mask_c, jnp.exp(s_c - m_c), 0.0)
    p_c = p_c / jnp.maximum(jnp.sum(p_c, axis=-1, keepdims=True), 1e-30)
    o_c = _dot(p_c.reshape(rows, nc).astype(BF16), vc_ref[...])
    p_sum = p_c[0] + p_c[1] + p_c[2]
    p_hi = p_sum.astype(BF16)
    p_lo = (p_sum - p_hi.astype(F32)).astype(BF16)
    importance = _dot(p_hi, ov_ref[...]) + _dot(p_lo, ov_ref[...])

    cur = (i * tq + qq) // SEL_LEN
    forced = (kk == 0) | (kk == cur) | (kk == cur - 1)
    score = jnp.where(forced, SEL_FORCE, jnp.where(kk <= cur, importance, NEG_BIG))
    st = score.T

    def selection_queries(n_blocks):
        sub = 8
        n_blocks = min(n_blocks, n_sel)
        slabs = [st[v * sub:(v + 1) * sub, :] for v in range((n_blocks + sub - 1) // sub)]
        keep = [slab > 0.5 * NEG_BIG for slab in slabs]
        if n_blocks > SEL_TOPK:
            blk = lax.broadcasted_iota(jnp.int32, (sub, tq), 0)
            ranks = [jnp.zeros((sub, tq), F32) for _ in slabs]
            for b in range(n_blocks):
                row = st[b:b + 1, :]
                for v, slab in enumerate(slabs):
                    if v * sub > b:
                        beats = row >= slab
                    elif (v + 1) * sub - 1 <= b:
                        beats = row > slab
                    else:
                        beats = (row > slab) | ((row == slab) & (blk + v * sub > b))
                    ranks[v] = ranks[v] + jnp.where(beats, 1.0, 0.0)
            keep = [k & (r < float(SEL_TOPK)) for k, r in zip(keep, ranks)]
        drop_t = jnp.concatenate([jnp.where(k, 0.0, 1.0) for k in keep]
                                 + [jnp.ones((tq - sub * len(slabs), tq), F32)], axis=0)
        drop = drop_t.T.astype(BF16)
        return jnp.concatenate([qb, jnp.concatenate([drop] * r3, axis=0)], axis=1)

    def normalise(acc):
        den = jnp.maximum(acc[:, HEAD_DIM:HEAD_DIM + 1], 1e-30)
        return (acc[:, 0:HEAD_DIM] / den).reshape(r3, tq, HEAD_DIM)

    band = (N_WIN_PREV + 1) * tq
    front = N_WIN_PREV * tq
    far_keys = sc_ref.shape[-1]
    boff = pl.multiple_of(i * tq, tq)
    q_b = lax.broadcasted_iota(jnp.int32, (tq, band), 0)
    k_b = lax.broadcasted_iota(jnp.int32, (tq, band), 1)

    def lane_fold(x, op):
        out = x[..., 0:tq]
        for c in range(1, x.shape[-1] // tq):
            out = op(out, x[..., c * tq:(c + 1) * tq])
        return out

    n_far = jnp.maximum(i - 1, 0) // (far_keys // tq)

    def selected(nf):
        last_tile = (nf + 1) * (far_keys // tq)
        q_sel = selection_queries((last_tile + 1) * tq // SEL_LEN)
        first_near = front - (i * tq - nf * far_keys)
        ok_s = (k_b <= q_b + front) & (k_b >= first_near)
        def group(pieces, with_band):
            keys_of = lambda c, lo, hi: slice(front + c * far_keys + lo, front + c * far_keys + hi)
            mx = None
            if with_band:
                s_b = (_dot_nt(q_sel, ks_ref[pl.ds(boff, band), :]).reshape(r3, tq, band) + bs_ref[...]
                       + jnp.where(ok_s, 0.0, NEG_BIG)[None])
                mx = lane_fold(s_b, jnp.maximum)
            for c, lo, hi in pieces:
                s = _dot_nt(q_sel, ks_ref[keys_of(c, lo, hi), :]).reshape(r3, tq, hi - lo)
                sc_ref[c, :, :, lo:hi] = s
                fold = lane_fold(s, jnp.maximum)
                mx = fold if mx is None else jnp.maximum(mx, fold)
            m = jnp.max(mx, axis=-1, keepdims=True)
            acc = None
            if with_band:
                acc = _dot(jnp.exp(s_b - m).reshape(rows, band).astype(BF16), vs_ref[pl.ds(boff, band), :])
            for c, lo, hi in pieces:
                p = jnp.exp(sc_ref[c, :, :, lo:hi] - m).reshape(rows, hi - lo).astype(BF16)
                part = _dot(p, vs_ref[keys_of(c, lo, hi), :])
                acc = part if acc is None else acc + part
            return m.reshape(rows, 1), acc

        if nf == 0:
            return normalise(group([], True)[1])
        whole = lambda cs: [(c, 0, far_keys) for c in cs]
        if nf % 2:
            first, second = whole(range((nf + 1) // 2)), whole(range((nf + 1) // 2, nf))
        else:
            mid = nf // 2
            first = whole(range(mid)) + [(mid, 0, far_keys // 2)]
            second = [(mid, far_keys // 2, far_keys)] + whole(range(mid + 1, nf))
        m_b, acc_b = group(second, True)
        m_a, acc_a = group(first, False)
        m_s = jnp.maximum(m_a, m_b)
        return normalise(acc_a * jnp.exp(m_a - m_s) + acc_b * jnp.exp(m_b - m_s))

    ok_w = (k_b > q_b) & (k_b <= q_b + front) & (k_b >= front - i * tq)
    s_w = (_dot_nt(qb, kw_ref[pl.ds(boff, band), :]).reshape(r3, tq, band) + bw_ref[...]
           + jnp.where(ok_w, 0.0, NEG_BIG)[None])
    p_w = jnp.exp(s_w - jnp.max(s_w, axis=-1, keepdims=True))
    o_w = normalise(_dot(p_w.reshape(rows, band).astype(BF16), vw_ref[pl.ds(boff, band), :]))

    gates = jax.nn.sigmoid(gate_ref[...])
    o_c = o_c.reshape(r3, tq, HEAD_DIM)
    col = lambda branch, r: gates[:, LANE_GATE + branch * r3 + r:LANE_GATE + branch * r3 + r + 1]
    partial = [col(0, r) * o_c[r] + col(2, r) * o_w[r] for r in range(r3)]
    gate_s = [col(1, r) for r in range(r3)]

    for nf in range(sc_ref.shape[0]):
        @pl.when(n_far == nf)
        def _(nf=nf):
            o_s = selected(nf)
            for r in range(r3):
                o_ref[:, r * HEAD_DIM:(r + 1) * HEAD_DIM] = (partial[r] + gate_s[r] * o_s[r]).astype(BF16)


def _nsa(p4, prep, bias_c, bias_sel, bias_win, overlap, q_norm, layer, tq=WIN_BLOCK, far_keys=512):
    _, b, s, _ = p4.shape
    kc, vc, ks, vs, kw, vw = prep
    nc = kc.shape[2]
    r3 = Q_PER_KV
    n_sel = s // SEL_LEN
    n_cmp = (s - CMP_LEN) // CMP_STRIDE + 1
    band = (N_WIN_PREV + 1) * tq
    padded = ks.shape[2]
    kv_small = pl.BlockSpec((None, None, nc, LANES), lambda i, g, j: (i, g, 0, 0))
    kv_full = pl.BlockSpec((None, None, padded, LANES), lambda i, g, j: (i, g, 0, 0))
    kv_wide = pl.BlockSpec((None, None, padded, 2 * LANES), lambda i, g, j: (i, g, 0, 0))
    return pl.pallas_call(
        functools.partial(_nsa_kernel, n_cmp=n_cmp, n_sel=n_sel),
        out_shape=jax.ShapeDtypeStruct((b, s, N_HEADS_C * HEAD_DIM), BF16),
        grid=(b, KV_HEADS, s // tq),
        in_specs=[pl.BlockSpec((r3, None, tq, LANES), lambda i, g, j: (G_QC // r3 + g, i, j, 0)),
                  pl.BlockSpec((None, None, tq, LANES), lambda i, g, j: (G_SMALL + g, i, j, 0)),
                  kv_small, kv_small, kv_wide, kv_wide, kv_full, kv_wide,
                  pl.BlockSpec((r3, tq, nc), lambda i, g, j: (g, j, 0)),
                  pl.BlockSpec((r3, tq, band), lambda i, g, j: (g, 0, 0)),
                  pl.BlockSpec((r3, tq, band), lambda i, g, j: (g, 0, 0)),
                  pl.BlockSpec((nc, LANES), lambda i, g, j: (0, 0)),
                  pl.BlockSpec((None, 1, LANES), lambda i, g, j: (layer, 0, 0))],
        out_specs=pl.BlockSpec((None, tq, r3 * HEAD_DIM), lambda i, g, j: (i, j, g)),
        scratch_shapes=[pltpu.VMEM((s // far_keys, r3, tq, far_keys), F32)],
        compiler_params=_cparams(("parallel", "parallel", "arbitrary")),
        name="nsa",
    )(p4, p4, kc, vc, ks, vs, kw, vw, bias_c, bias_sel, bias_win, overlap, q_norm)


def _interleave(fragments):
    live = list(fragments)
    while live:
        for fragment in list(live):
            try:
                next(fragment)
            except StopIteration:
                live.remove(fragment)


def _nsa_pair_kernel(q_ref, gate_ref, kc_ref, vc_ref, ks_ref, vs_ref, kw_ref, vw_ref,
                     bc_ref, bs_ref, bw_ref, ov_ref, qg_ref, o_ref, sc_ref, *, n_cmp, n_sel):
    tq = q_ref.shape[1]
    r3 = Q_PER_KV
    i = pl.program_id(1)
    nc = kc_ref.shape[1]
    rows = r3 * tq
    band = (N_WIN_PREV + 1) * tq
    front = N_WIN_PREV * tq
    far_keys = sc_ref.shape[-1]
    boff = pl.multiple_of(i * tq, tq)
    n_far = jnp.maximum(i - 1, 0) // (far_keys // tq)

    qq = lax.broadcasted_iota(jnp.int32, (tq, tq), 0)
    kk = lax.broadcasted_iota(jnp.int32, (tq, tq), 1)
    q_b = lax.broadcasted_iota(jnp.int32, (tq, band), 0)
    k_b = lax.broadcasted_iota(jnp.int32, (tq, band), 1)
    qpos_c = i * tq + lax.broadcasted_iota(jnp.int32, (tq, nc), 0)
    n_idx = lax.broadcasted_iota(jnp.int32, (tq, nc), 1)
    mask_c = ((qpos_c >= n_idx * CMP_STRIDE + (CMP_LEN - 1)) & (n_idx < n_cmp))[None]
    ok_w = (k_b > q_b) & (k_b <= q_b + front) & (k_b >= front - i * tq)
    cur = (i * tq + qq) // SEL_LEN
    forced = (kk == 0) | (kk == cur) | (kk == cur - 1)
    qg = qg_ref[...]

    def lane_fold(x, op):
        out = x[..., 0:tq]
        for c in range(1, x.shape[-1] // tq):
            out = op(out, x[..., c * tq:(c + 1) * tq])
        return out

    def normalise(acc):
        den = jnp.maximum(acc[:, HEAD_DIM:HEAD_DIM + 1], 1e-30)
        return (acc[:, 0:HEAD_DIM] / den).reshape(r3, tq, HEAD_DIM)

    kept = [{} for _ in range(KV_HEADS)]

    def before_selection(g):
        heads = slice(g * r3, (g + 1) * r3)
        qf = jnp.concatenate([_rms(q_ref[g * r3 + r], qg) * (HEAD_DIM ** -0.5) for r in range(r3)], axis=0)
        qb = qf.astype(BF16)
        yield
        s_c = _dot_nt(qb, kc_ref[g]).reshape(r3, tq, nc) + bc_ref[heads]
        s_w = (_dot_nt(qb, kw_ref[g, pl.ds(boff, band), :]).reshape(r3, tq, band) + bw_ref[heads]
               + jnp.where(ok_w, 0.0, NEG_BIG)[None])
        yield
        s_c = jnp.where(mask_c, s_c, NEG_BIG)
        m_c = jnp.max(s_c, axis=-1, keepdims=True)
        p_c = jnp.where(mask_c, jnp.exp(s_c - m_c), 0.0)
        p_c = p_c / jnp.maximum(jnp.sum(p_c, axis=-1, keepdims=True), 1e-30)
        yield
        o_c = _dot(p_c.reshape(rows, nc).astype(BF16), vc_ref[g]).reshape(r3, tq, HEAD_DIM)
        p_sum = p_c[0] + p_c[1] + p_c[2]
        p_hi = p_sum.astype(BF16)
        p_lo = (p_sum - p_hi.astype(F32)).astype(BF16)
        importance = _dot(p_hi, ov_ref[...]) + _dot(p_lo, ov_ref[...])
        yield
        p_w = jnp.exp(s_w - jnp.max(s_w, axis=-1, keepdims=True))
        o_w = normalise(_dot(p_w.reshape(rows, band).astype(BF16), vw_ref[g, pl.ds(boff, band), :]))
        yield
        score = jnp.where(forced, SEL_FORCE, jnp.where(kk <= cur, importance, NEG_BIG))
        n_rows = ((n_sel + 7) // 8) * 8
        st = score.T[0:n_rows, :]
        sub = 8
        slabs = [st[v * sub:(v + 1) * sub, :] for v in range(n_rows // sub)]
        blk = lax.broadcasted_iota(jnp.int32, (sub, tq), 0)
        ranks = [jnp.zeros((sub, tq), F32) for _ in slabs]
        for b in range(n_sel):
            row = st[b:b + 1, :]
            for v, slab in enumerate(slabs):
                if v * sub > b:
                    beats = row >= slab
                elif (v + 1) * sub - 1 <= b:
                    beats = row > slab
                else:
                    beats = (row > slab) | ((row == slab) & (blk + v * sub > b))
                ranks[v] = ranks[v] + jnp.where(beats, 1.0, 0.0)
            if b % 8 == 7:
                yield
        rank = jnp.concatenate(ranks, axis=0)
        drop_t = jnp.where((rank < float(min(SEL_TOPK, n_sel))) & (st > 0.5 * NEG_BIG), 0.0, 1.0)
        if n_rows < tq:
            drop_t = jnp.concatenate([drop_t, jnp.zeros((tq - n_rows, tq), F32)], axis=0)
        drop = drop_t.T.astype(BF16)
        q_sel = jnp.concatenate([qb, jnp.concatenate([drop] * r3, axis=0)], axis=1)
        gates = jax.nn.sigmoid(gate_ref[g])
        col = lambda branch, r: gates[:, LANE_GATE + branch * r3 + r:LANE_GATE + branch * r3 + r + 1]
        kept[g].update(q_sel=q_sel, gate_s=[col(1, r) for r in range(r3)],
                       partial=[col(0, r) * o_c[r] + col(2, r) * o_w[r] for r in range(r3)])

    _interleave(before_selection(g) for g in range(KV_HEADS))

    def half_softmax(g, pieces, with_band, nf, result):
        q_sel = kept[g]["q_sel"]
        keys_of = lambda c, lo, hi: slice(front + c * far_keys + lo, front + c * far_keys + hi)
        mx = None
        if with_band:
            first_near = front - (i * tq - nf * far_keys)
            ok_s = (k_b <= q_b + front) & (k_b >= first_near)
            s_b = (_dot_nt(q_sel, ks_ref[g, pl.ds(boff, band), :]).reshape(r3, tq, band)
                   + bs_ref[g * r3:(g + 1) * r3] + jnp.where(ok_s, 0.0, NEG_BIG)[None])
            mx = lane_fold(s_b, jnp.maximum)
            yield
        for c, lo, hi in pieces:
            s = _dot_nt(q_sel, ks_ref[g, keys_of(c, lo, hi), :]).reshape(r3, tq, hi - lo)
            sc_ref[g, c, :, :, lo:hi] = s
            fold = lane_fold(s, jnp.maximum)
            mx = fold if mx is None else jnp.maximum(mx, fold)
            yield
        m = jnp.max(mx, axis=-1, keepdims=True)
        acc = None
        if with_band:
            acc = _dot(jnp.exp(s_b - m).reshape(rows, band).astype(BF16), vs_ref[g, pl.ds(boff, band), :])
            yield
        for c, lo, hi in pieces:
            p = jnp.exp(sc_ref[g, c, :, :, lo:hi] - m).reshape(rows, hi - lo).astype(BF16)
            part = _dot(p, vs_ref[g, keys_of(c, lo, hi), :])
            acc = part if acc is None else acc + part
            yield
        result.append((m.reshape(rows, 1), acc))

    def selected(nf):
        whole = lambda cs: [(c, 0, far_keys) for c in cs]
        if nf == 0:
            first, second = None, []
        elif nf % 2:
            first, second = whole(range((nf + 1) // 2)), whole(range((nf + 1) // 2, nf))
        else:
            mid = nf // 2
            first = whole(range(mid)) + [(mid, 0, far_keys // 2)]
            second = [(mid, far_keys // 2, far_keys)] + whole(range(mid + 1, nf))
        halves = [([], []) for _ in range(KV_HEADS)]
        fragments = []
        for g in range(KV_HEADS):
            fragments.append(half_softmax(g, second, True, nf, halves[g][1]))
            if first is not None:
                fragments.append(half_softmax(g, first, False, nf, halves[g][0]))
        _interleave(fragments)
        for g in range(KV_HEADS):
            (m_b, acc_b), = halves[g][1]
            if first is None:
                o_s = normalise(acc_b)
            else:
                (m_a, acc_a), = halves[g][0]
                m_s = jnp.maximum(m_a, m_b)
                o_s = normalise(acc_a * jnp.exp(m_a - m_s) + acc_b * jnp.exp(m_b - m_s))
            for r in range(r3):
                out = kept[g]["partial"][r] + kept[g]["gate_s"][r] * o_s[r]
                o_ref[:, (g * r3 + r) * HEAD_DIM:(g * r3 + r + 1) * HEAD_DIM] = out.astype(BF16)

    for nf in range(sc_ref.shape[1]):
        pl.when(n_far == nf)(functools.partial(selected, nf))


def _nsa_pair(p4, prep, bias_c, bias_sel, bias_win, overlap, q_norm, layer, tq=WIN_BLOCK, far_keys=512):
    _, b, s, _ = p4.shape
    kc, vc, ks, vs, kw, vw = prep
    nc = kc.shape[2]
    nq = N_HEADS_C
    n_sel = s // SEL_LEN
    n_cmp = (s - CMP_LEN) // CMP_STRIDE + 1
    band = (N_WIN_PREV + 1) * tq
    padded = ks.shape[2]
    per_batch = lambda *tail: pl.BlockSpec((None, KV_HEADS) + tail, lambda i, j: (i, 0, 0, 0))
    resident = lambda *tail: pl.BlockSpec((None, KV_HEADS) + tail, lambda i, j: (i, 0, 0, 0),
                                          pipeline_mode=pl.Buffered(1))
    fixed = lambda *shape: pl.BlockSpec(shape, lambda i, j: (0,) * len(shape), pipeline_mode=pl.Buffered(1))
    return pl.pallas_call(
        functools.partial(_nsa_pair_kernel, n_cmp=n_cmp, n_sel=n_sel),
        out_shape=jax.ShapeDtypeStruct((b, s, nq * HEAD_DIM), BF16),
        grid=(b, s // tq),
        in_specs=[pl.BlockSpec((nq, None, tq, LANES), lambda i, j: (G_QC // nq, i, j, 0)),
                  pl.BlockSpec((KV_HEADS, None, tq, LANES), lambda i, j: (G_SMALL // KV_HEADS, i, j, 0)),
                  per_batch(nc, LANES), per_batch(nc, LANES),
                  resident(padded, 2 * LANES), resident(padded, 2 * LANES),
                  resident(padded, LANES), resident(padded, 2 * LANES),
                  pl.BlockSpec((nq, tq, nc), lambda i, j: (0, j, 0)),
                  fixed(nq, tq, band), fixed(nq, tq, band), fixed(nc, LANES),
                  pl.BlockSpec((None, 1, LANES), lambda i, j: (layer, 0, 0))],
        out_specs=pl.BlockSpec((None, tq, nq * HEAD_DIM), lambda i, j: (i, j, 0)),
        scratch_shapes=[pltpu.VMEM((KV_HEADS, s // far_keys, Q_PER_KV, tq, far_keys), F32)],
        compiler_params=_cparams(("parallel", "arbitrary")),
        name="nsa",
    )(p4, p4, kc, vc, ks, vs, kw, vw, bias_c, bias_sel, bias_win, overlap, q_norm)


def _t5_bucket(dist):
    n = jnp.maximum(dist, 0)
    max_exact = RPB_BUCKETS // 2
    log_ratio = jnp.log(jnp.maximum(n, 1).astype(F32) / max_exact) / math.log(RPB_MAX_DIST / max_exact)
    large = jnp.minimum(max_exact + (log_ratio * (RPB_BUCKETS - max_exact)).astype(jnp.int32), RPB_BUCKETS - 1)
    return jnp.where(n < max_exact, n, large)


def _bias_tables(rel_bias, s, tq):
    table = rel_bias.astype(F32)
    buckets = jnp.arange(RPB_BUCKETS, dtype=jnp.int32)[:, None, None]

    def look(dist):
        onehot = (_t5_bucket(dist)[None] == buckets).astype(F32)
        return jnp.einsum("nh,nqk->hqk", table, onehot, precision=lax.Precision.HIGHEST)

    nc = s // CMP_STRIDE
    pos = jnp.arange(s, dtype=jnp.int32)
    cmp_end = jnp.arange(nc, dtype=jnp.int32) * CMP_STRIDE + (CMP_LEN - 1)
    bias_c = look(pos[:, None] - cmp_end[None, :])
    q = jnp.arange(tq, dtype=jnp.int32)[:, None]
    kb = jnp.arange((N_WIN_PREV + 1) * tq, dtype=jnp.int32)[None, :]
    bias_win = look(N_WIN_PREV * tq + q - kb)
    bias_sel = bias_win - table[RPB_BUCKETS - 1][:, None, None]
    return bias_c, bias_sel, bias_win


def _selection_constants(s, tq):
    nc = s // CMP_STRIDE
    n_cmp = (s - CMP_LEN) // CMP_STRIDE + 1
    n_sel = s // SEL_LEN
    cmp_start = np.arange(nc) * CMP_STRIDE
    sel_start = np.arange(LANES) * SEL_LEN
    overlap = ((cmp_start[:, None] < sel_start[None, :] + SEL_LEN)
               & (cmp_start[:, None] + CMP_LEN > sel_start[None, :])
               & (np.arange(nc)[:, None] < n_cmp) & (np.arange(LANES)[None, :] < n_sel))
    key_pos = np.arange(-WINDOW, s)[:, None]
    in_block = (np.arange(LANES)[None, :] == (key_pos // SEL_LEN)) & (key_pos >= 0)
    block_mask = jnp.where(jnp.asarray(in_block), NEG_BIG, 0.0).astype(BF16)
    ones_col = jnp.asarray((np.arange(LANES)[None, :] == 0) & (key_pos >= 0), BF16)
    return jnp.asarray(overlap, BF16), block_mask, ones_col


def _arrange_w_in(w_in):
    depth, d, _ = w_in.shape
    w_in = w_in.astype(BF16)
    da, db, dc, dkv = N_HEADS_A * 128, N_HEADS_B * 128, N_HEADS_C * 128, KV_HEADS * 128
    o_ba = 4 * da
    o_aa = o_ba + N_HEADS_A
    o_ub = o_aa + N_HEADS_A
    o_qc = o_ub + 2 * db
    o_kc = o_qc + dc
    o_gc = o_kc + 6 * dkv
    sl = lambda lo, n: w_in[:, :, lo:lo + n]
    gate = w_in[:, :, o_gc:o_gc + 3 * N_HEADS_C].reshape(depth, d, 3, KV_HEADS, Q_PER_KV)
    zeros = lambda n: jnp.zeros((depth, d, n), w_in.dtype)
    small0 = jnp.concatenate([sl(o_ba, 2 * N_HEADS_A), gate[:, :, :, 0, :].reshape(depth, d, 3 * Q_PER_KV),
                              zeros(LANES - 2 * N_HEADS_A - 3 * Q_PER_KV)], axis=-1)
    small1 = jnp.concatenate([zeros(LANE_GATE), gate[:, :, :, 1, :].reshape(depth, d, 3 * Q_PER_KV),
                              zeros(LANES - LANE_GATE - 3 * Q_PER_KV)], axis=-1)
    parts = [sl(0, 4 * da),
             sl(o_qc, dc),
             sl(o_kc, dkv),
             sl(o_ub, 2 * db),
             sl(o_kc + dkv, 5 * dkv),
             small0, small1]
    return jnp.concatenate(parts, axis=-1).astype(BF16)


def kernel(x, attn_norm, w_in, conv_a, a_log, dt_bias, gdn_norm, sgu_ln_g, sgu_ln_b, sgu_w, sgu_b,
           nsa_q_norm, nsa_k_norm, cmp_pos, cmp_w1, cmp_w2, rel_bias, w_out, mlp_norm, w_up, w_down):
    b, s, d = x.shape
    depth = w_in.shape[0]
    t = b * s
    tq = 128

    w_in_r = _arrange_w_in(w_in)
    w_out_b = w_out.astype(BF16)
    w_up_b = w_up.astype(BF16)
    w_down_b = w_down.astype(BF16)
    attn_g = attn_norm.reshape(depth, 1, d)
    mlp_g = mlp_norm.reshape(depth, 1, d)
    conv_r = conv_a.reshape(depth, CONV_WIDTH * 3 * N_HEADS_A, LANES)
    pad_to = lambda v, lo: jnp.pad(v, ((0, 0), (lo, LANES - lo - v.shape[1])))
    ab = jnp.stack([pad_to(a_log, LANE_DECAY), pad_to(dt_bias, LANE_DECAY)], axis=1)
    gdn_g = gdn_norm.reshape(depth, 1, LANES)
    ln_g = sgu_ln_g.reshape(depth, N_HEADS_B, LANES)
    ln_b = sgu_ln_b.reshape(depth, N_HEADS_B, LANES)
    sgu_bias = jnp.broadcast_to(sgu_b[..., None], sgu_b.shape + (LANES,))
    q_g = nsa_q_norm.reshape(depth, 1, LANES)
    k_g = nsa_k_norm.reshape(depth, 1, LANES)
    w1_b = cmp_w1.astype(BF16)
    w2_b = cmp_w2.astype(BF16)
    pos_b = jnp.broadcast_to(cmp_pos.reshape(depth, 2, 1, CMP_LEN * LANES),
                             (depth, 2, 8, CMP_LEN * LANES)).astype(BF16)
    bias_c, bias_sel, bias_win = _bias_tables(rel_bias, s, tq)
    overlap, block_mask, ones_col = _selection_constants(s, tq)

    x2 = x.reshape(t, d)
    for layer in range(depth):
        p = _inproj(x2, attn_g, w_in_r, layer)
        p4 = p.reshape(N_GROUPS, b, s, LANES)
        mix_a = _gdn(p4, conv_r, ab, gdn_g, layer)
        mix_b = _sgu(p4, ln_g, ln_b, sgu_w, sgu_bias, layer)
        prep = _nsa_prep(p4, w1_b, w2_b, pos_b, k_g, block_mask, ones_col, layer)
        mix_c = _nsa_pair(p4, prep, bias_c, bias_sel, bias_win, overlap, q_g, layer, tq=tq)
        x2 = _outproj(x2, mix_a.reshape(t, -1), mix_b.reshape(t, -1), mix_c.reshape(t, -1), w_out_b, layer)
        x2 = _mlp(x2, mlp_g, w_up_b, w_down_b, layer)
    return x2.reshape(b, s, d)
```

```python
import functools
import math

import numpy as np
import jax
import jax.numpy as jnp
from jax import lax
from jax.experimental import pallas as pl
from jax.experimental.pallas import tpu as pltpu

F32 = jnp.float32
BF16 = jnp.bfloat16

LANES = 128
HEAD_DIM = 128
N_HEADS_A = 6
N_HEADS_B = 4
N_HEADS_C = 6
KV_HEADS = 2
Q_PER_KV = N_HEADS_C // KV_HEADS
CONV_WIDTH = 4
GDN_CHUNK = 64
SGU_CHUNK = 128
CMP_LEN = 32
CMP_STRIDE = 16
SEL_LEN = 64
SEL_TOPK = 16
WINDOW = 512
WIN_BLOCK = 128
N_WIN_PREV = WINDOW // WIN_BLOCK
RPB_BUCKETS = 32
RPB_MAX_DIST = 128
NORM_EPS = 1e-6
NEG_BIG = -1e30
SEL_FORCE = 1e9
VMEM_LIMIT = 56 * 1024 * 1024

G_QA, G_KA, G_VA, G_ZA = 0, 6, 12, 18
G_QC = 24
G_KCMP = 30
G_UB, G_VB = 32, 36
G_VCMP = 40
G_KSLC, G_VSLC, G_KWIN, G_VWIN = 42, 44, 46, 48
G_SMALL = 50
N_GROUPS = 52
LANE_BETA, LANE_DECAY, LANE_GATE = 0, 6, 12


def _dot(a, b, precision=None):
    return jnp.dot(a, b, preferred_element_type=F32, precision=precision)


def _dot_nt(a, b):
    return lax.dot_general(a, b, (((1,), (1,)), ((), ())), preferred_element_type=F32)


def _dot_tn(a, b):
    return lax.dot_general(a, b, (((0,), (0,)), ((), ())), preferred_element_type=F32)


def _rms(x, gain):
    return x * lax.rsqrt(jnp.mean(x * x, axis=-1, keepdims=True) + NORM_EPS) * gain


def _cparams(sem):
    return pltpu.CompilerParams(dimension_semantics=sem, vmem_limit_bytes=VMEM_LIMIT)


def _inproj_kernel(x0_ref, xn_ref, g_ref, w_ref, o_ref, h_ref):
    i, j = pl.program_id(0), pl.program_id(1)
    part = xn_ref.shape[0] // pl.num_programs(1)

    @pl.when((i == 0) & (j == 0))
    def _():
        h_ref[0] = _rms(x0_ref[...], g_ref[...]).astype(BF16)

    cur = i % 2
    rows = pl.ds(pl.multiple_of(j * part, part), part)
    h_ref[1 - cur, rows, :] = _rms(xn_ref[rows, :], g_ref[...]).astype(BF16)

    r = _dot(h_ref[cur], w_ref[...])
    for c in range(o_ref.shape[0]):
        o_ref[c] = r[:, c * LANES:(c + 1) * LANES]


def _inproj(x2, gain, w, layer, tm=512, tn=13 * LANES):
    t, d = x2.shape
    n = w.shape[-1]
    last = t // tm - 1
    return pl.pallas_call(
        _inproj_kernel,
        out_shape=jax.ShapeDtypeStruct((n // LANES, t, LANES), F32),
        grid=(t // tm, n // tn),
        in_specs=[pl.BlockSpec((tm, d), lambda i, j: (0, 0), pipeline_mode=pl.Buffered(1)),
                  pl.BlockSpec((tm, d), lambda i, j: (jnp.minimum(i + 1, last), 0)),
                  pl.BlockSpec((None, 1, d), lambda i, j: (layer, 0, 0)),
                  pl.BlockSpec((None, d, tn), lambda i, j: (layer, 0, j))],
        out_specs=pl.BlockSpec((tn // LANES, tm, LANES), lambda i, j: (j, i, 0)),
        scratch_shapes=[pltpu.VMEM((2, tm, d), BF16)],
        compiler_params=_cparams(("arbitrary", "arbitrary")),
        name="inproj",
    )(x2, x2, gain, w)


def _outproj_kernel(x_ref, a_ref, b_ref, c_ref, w_ref, o_ref):
    da, db = a_ref.shape[1], b_ref.shape[1]
    acc = _dot(a_ref[...], w_ref[0:da, :])
    acc += _dot(b_ref[...], w_ref[da:da + db, :])
    acc += _dot(c_ref[...], w_ref[da + db:, :])
    o_ref[...] = x_ref[...] + acc


def _outproj(x2, ma, mb, mc, w_out, layer, tm=512):
    t, d = x2.shape
    da, db, dc = ma.shape[1], mb.shape[1], mc.shape[1]
    return pl.pallas_call(
        _outproj_kernel,
        out_shape=jax.ShapeDtypeStruct((t, d), F32),
        grid=(t // tm,),
        in_specs=[pl.BlockSpec((tm, d), lambda i: (i, 0)),
                  pl.BlockSpec((tm, da), lambda i: (i, 0)),
                  pl.BlockSpec((tm, db), lambda i: (i, 0)),
                  pl.BlockSpec((tm, dc), lambda i: (i, 0)),
                  pl.BlockSpec((None, da + db + dc, d), lambda i: (layer, 0, 0))],
        out_specs=pl.BlockSpec((tm, d), lambda i: (i, 0)),
        compiler_params=_cparams(("parallel",)),
        name="outproj",
    )(x2, ma, mb, mc, w_out)


def _mlp_kernel(x_ref, xn_ref, g_ref, wu_ref, wd_ref, o_ref, h_ref):
    i, k = pl.program_id(0), pl.program_id(1)
    part = xn_ref.shape[0] // pl.num_programs(1)

    @pl.when((i == 0) & (k == 0))
    def _():
        h_ref[0] = _rms(x_ref[...], g_ref[...]).astype(BF16)

    @pl.when(k == 0)
    def _():
        o_ref[...] = x_ref[...]

    cur = i % 2
    rows = pl.ds(pl.multiple_of(k * part, part), part)
    h_ref[1 - cur, rows, :] = _rms(xn_ref[rows, :], g_ref[...]).astype(BF16)

    a = jnp.maximum(_dot(h_ref[cur], wu_ref[...]), 0.0)
    o_ref[...] += _dot((a * a).astype(BF16), wd_ref[...])


def _mlp(x2, gain, w_up, w_down, layer, tm=512, tf=1024):
    t, d = x2.shape
    f = w_up.shape[-1]
    last = t // tm - 1
    return pl.pallas_call(
        _mlp_kernel,
        out_shape=jax.ShapeDtypeStruct((t, d), F32),
        grid=(t // tm, f // tf),
        in_specs=[pl.BlockSpec((tm, d), lambda i, k: (i, 0)),
                  pl.BlockSpec((tm, d), lambda i, k: (jnp.minimum(i + 1, last), 0)),
                  pl.BlockSpec((None, 1, d), lambda i, k: (layer, 0, 0)),
                  pl.BlockSpec((None, d, tf), lambda i, k: (layer, 0, k)),
                  pl.BlockSpec((None, tf, d), lambda i, k: (layer, k, 0))],
        out_specs=pl.BlockSpec((tm, d), lambda i, k: (i, 0)),
        scratch_shapes=[pltpu.VMEM((2, tm, d), BF16)],
        compiler_params=_cparams(("arbitrary", "arbitrary")),
        name="mlp",
    )(x2, x2, gain, w_up, w_down)


def _sgu_kernel(p_ref, lg_ref, lb_ref, w_ref, b_ref, o_ref):
    ts = p_ref.shape[1]
    nb = N_HEADS_B
    v = [jax.nn.gelu(p_ref[nb + g]) for g in range(nb)]
    width = float(nb * LANES)
    mu = sum(jnp.sum(vg, axis=-1, keepdims=True) for vg in v) / width
    var = sum(jnp.sum(jnp.square(vg - mu), axis=-1, keepdims=True) for vg in v) / width
    inv = lax.rsqrt(var + NORM_EPS)
    row = lax.broadcasted_iota(jnp.int32, (SGU_CHUNK, SGU_CHUNK), 0)
    col = lax.broadcasted_iota(jnp.int32, (SGU_CHUNK, SGU_CHUNK), 1)
    causal = col <= row
    for g in range(nb):
        vn = ((v[g] - mu) * inv * lg_ref[g:g + 1, :] + lb_ref[g:g + 1, :]).astype(BF16)
        wg = jnp.where(causal, w_ref[g], 0.0).astype(BF16)
        for c in range(ts // SGU_CHUNK):
            rows = slice(c * SGU_CHUNK, (c + 1) * SGU_CHUNK)
            mixed = _dot(wg, vn[rows]) + b_ref[g]
            u = jax.nn.gelu(p_ref[g, rows, :])
            o_ref[rows, g * LANES:(g + 1) * LANES] = (u * mixed).astype(BF16)


def _sgu(p4, ln_g, ln_b, w_s, b_s, layer, ts=512):
    _, b, s, _ = p4.shape
    nb = N_HEADS_B
    return pl.pallas_call(
        _sgu_kernel,
        out_shape=jax.ShapeDtypeStruct((b, s, nb * LANES), BF16),
        grid=(b, s // ts),
        in_specs=[pl.BlockSpec((2 * nb, None, ts, LANES), lambda i, j: (G_UB // (2 * nb), i, j, 0)),
                  pl.BlockSpec((None, nb, LANES), lambda i, j: (layer, 0, 0)),
                  pl.BlockSpec((None, nb, LANES), lambda i, j: (layer, 0, 0)),
                  pl.BlockSpec((None, nb, SGU_CHUNK, SGU_CHUNK), lambda i, j: (layer, 0, 0, 0)),
                  pl.BlockSpec((None, nb, SGU_CHUNK, LANES), lambda i, j: (layer, 0, 0, 0))],
        out_specs=pl.BlockSpec((None, ts, nb * LANES), lambda i, j: (i, j, 0)),
        compiler_params=_cparams(("parallel", "parallel")),
        name="sgu",
    )(p4, ln_g, ln_b, w_s, b_s)


def _gdn_kernel(a_ref, sm_ref, cw_ref, ab_ref, ng_ref, o_ref,
                xbuf, q_s, k_s, v_s, beta_s, gcum_s, u_s, w_s, qg_s, kd_s, a_s, st_ref, *, batch):
    ts = a_ref.shape[1]
    ck = GDN_CHUNK
    nh = N_HEADS_A
    pad = 8

    @pl.when(pl.program_id(1) == 0)
    def _():
        xbuf[:, 0:pad, :] = jnp.zeros((3 * nh, pad, LANES), F32)
        st_ref[...] = jnp.zeros_like(st_ref)

    xbuf[:, pad:pad + ts, :] = a_ref[0:3 * nh]
    dests = (q_s, k_s, v_s)
    for grp in range(3 * nh):
        acc = None
        for j in range(CONV_WIDTH):
            lo = pad - (CONV_WIDTH - 1) + j
            term = xbuf[grp, lo:lo + ts, :] * cw_ref[j * 3 * nh + grp:j * 3 * nh + grp + 1, :]
            acc = term if acc is None else acc + term
        y = acc * jax.nn.sigmoid(acc)
        kind, h = divmod(grp, nh)
        if kind < 2:
            y = y * lax.rsqrt(jnp.sum(y * y, axis=-1, keepdims=True) + NORM_EPS)
        dests[kind][h] = y
    xbuf[:, 0:pad, :] = xbuf[:, ts:ts + pad, :]

    sm = sm_ref[...]
    beta_s[...] = jax.nn.sigmoid(sm)
    z = sm + ab_ref[1:2, :]
    softplus = jnp.maximum(z, 0.0) + jnp.log1p(jnp.exp(-jnp.abs(z)))
    g_all = -jnp.exp(ab_ref[0:1, :]) * softplus

    ii = lax.broadcasted_iota(jnp.int32, (ck, ck), 0)
    jj = lax.broadcasted_iota(jnp.int32, (ck, ck), 1)
    incl = jj <= ii
    strict = jj < ii
    tri = jnp.where(incl, 1.0, 0.0).astype(F32)
    for c in range(ts // ck):
        rows = slice(c * ck, (c + 1) * ck)
        gcum_s[rows, :] = _dot(tri, g_all[rows], precision=lax.Precision.HIGHEST)

    scale = HEAD_DIM ** -0.5
    gain = ng_ref[...]
    bf = lambda m: m.astype(BF16)
    same = lambda size: (ii // size) == (jj // size)
    n_chunks = ts // ck
    rows_of = lambda c: slice(c * ck, (c + 1) * ck)
    dcol = lambda h: slice(LANE_DECAY + h, LANE_DECAY + h + 1)

    for c0 in range(0, n_chunks, batch):
        gc = {c: gcum_s[rows_of(c), :] for c in range(c0, c0 + batch)}
        gct = {c: gc[c].T for c in gc}
        items = [(c, h) for c in range(c0, c0 + batch) for h in range(nh)]
        kk, qk = {}, {}
        for c, h in items:
            g_col = gc[c][:, dcol(h)]
            q = q_s[h, rows_of(c), :] * scale
            k = k_s[h, rows_of(c), :]
            kbf = bf(k)
            kk[c, h] = _dot_nt(bf(k * beta_s[rows_of(c), LANE_BETA + h:LANE_BETA + h + 1]), kbf)
            qk[c, h] = _dot_nt(bf(q), kbf)
            qg_s[h, rows_of(c), :] = bf(q * jnp.exp(g_col))
            kd_s[h, rows_of(c), :] = bf(k * jnp.exp(gc[c][ck - 1:ck, dcol(h)] - g_col))
        lower, diag, p1 = {}, {}, {}
        for it in items:
            c, h = it
            decay = jnp.exp(jnp.where(incl, gc[c][:, dcol(h)] - gct[c][dcol(h), :], NEG_BIG))
            lower[it] = jnp.where(strict, kk[it] * decay, 0.0)
            a_s[h, rows_of(c), :] = bf(jnp.where(incl, qk[it] * decay, 0.0))
            diag[it] = jnp.where(same(8), lower[it], 0.0)
            p1[it] = _dot(bf(diag[it]), bf(diag[it]))
        p2 = {it: _dot(bf(p1[it]), bf(p1[it])) for it in items}
        dp1 = {it: _dot(bf(diag[it]), bf(p1[it])) for it in items}
        na = {it: p1[it] - diag[it] - dp1[it] for it in items}
        nap2 = {it: _dot(bf(na[it]), bf(p2[it])) for it in items}
        n = {it: na[it] + p2[it] + nap2[it] for it in items}
        size = 8
        while size < ck:
            off = same(2 * size) & ~same(size)
            cm = {it: jnp.where(off, lower[it], 0.0) for it in items}
            y = {it: cm[it] + _dot(bf(n[it]), bf(cm[it])) for it in items}
            yn = {it: _dot(bf(y[it]), bf(n[it])) for it in items}
            n = {it: n[it] - (y[it] + yn[it]) for it in items}
            size *= 2
        for it in items:
            c, h = it
            beta = beta_s[rows_of(c), LANE_BETA + h:LANE_BETA + h + 1]
            nb = bf(n[it])
            vb = v_s[h, rows_of(c), :] * beta
            kw = k_s[h, rows_of(c), :] * (beta * jnp.exp(gc[c][:, dcol(h)]))
            u_s[h, rows_of(c), :] = vb + _dot(nb, bf(vb))
            w_s[h, rows_of(c), :] = bf(kw + _dot(nb, bf(kw)))

    heads = range(nh)
    for c in range(n_chunks):
        rows = rows_of(c)
        state = [st_ref[h] for h in heads]
        sb = [bf(s) for s in state]
        ws = [_dot(w_s[h, rows, :], sb[h]) for h in heads]
        qs = [_dot(qg_s[h, rows, :], sb[h]) for h in heads]
        vnb = [bf(u_s[h, rows, :] - ws[h]) for h in heads]
        kv = [_dot_tn(kd_s[h, rows, :], vnb[h]) for h in heads]
        av = [_dot(a_s[h, rows, :], vnb[h]) for h in heads]
        for h in heads:
            g_last = gcum_s[(c + 1) * ck - 1:(c + 1) * ck, dcol(h)]
            st_ref[h] = state[h] * jnp.exp(g_last) + kv[h]
            zg = a_ref[3 * nh + h, rows, :]
            out = _rms(qs[h] + av[h], gain) * (zg * jax.nn.sigmoid(zg))
            o_ref[rows, h * LANES:(h + 1) * LANES] = out.astype(BF16)


def _gdn(p4, conv_w, ab, norm_g, layer, ts=512, batch=4):
    _, b, s, _ = p4.shape
    nh = N_HEADS_A
    return pl.pallas_call(
        functools.partial(_gdn_kernel, batch=batch),
        out_shape=jax.ShapeDtypeStruct((b, s, nh * LANES), BF16),
        grid=(b, s // ts),
        in_specs=[pl.BlockSpec((4 * nh, None, ts, LANES), lambda i, j: (0, i, j, 0)),
                  pl.BlockSpec((None, None, ts, LANES), lambda i, j: (G_SMALL, i, j, 0)),
                  pl.BlockSpec((None, CONV_WIDTH * 3 * nh, LANES), lambda i, j: (layer, 0, 0)),
                  pl.BlockSpec((None, 2, LANES), lambda i, j: (layer, 0, 0)),
                  pl.BlockSpec((None, 1, LANES), lambda i, j: (layer, 0, 0))],
        out_specs=pl.BlockSpec((None, ts, nh * LANES), lambda i, j: (i, j, 0)),
        scratch_shapes=[pltpu.VMEM((3 * nh, ts + 8, LANES), F32),
                        pltpu.VMEM((nh, ts, LANES), F32),
                        pltpu.VMEM((nh, ts, LANES), F32),
                        pltpu.VMEM((nh, ts, LANES), F32),
                        pltpu.VMEM((ts, LANES), F32),
                        pltpu.VMEM((ts, LANES), F32),
                        pltpu.VMEM((nh, ts, HEAD_DIM), F32),
                        pltpu.VMEM((nh, ts, HEAD_DIM), BF16),
                        pltpu.VMEM((nh, ts, HEAD_DIM), BF16),
                        pltpu.VMEM((nh, ts, HEAD_DIM), BF16),
                        pltpu.VMEM((nh, ts, GDN_CHUNK), BF16),
                        pltpu.VMEM((nh, HEAD_DIM, HEAD_DIM), F32)],
        compiler_params=_cparams(("parallel", "arbitrary")),
        name="gdn",
    )(p4, p4, conv_w, ab, norm_g)


def _nsa_prep_kernel(kc2_ref, vc2_ref, ksl_ref, vsl_ref, kwn_ref, vwn_ref,
                     w1_ref, w2_ref, pos_ref, kg_ref, bm_ref, one_ref,
                     kc_o, vc_o, ks_o, vs_o, kw_o, vw_o):
    nc = kc_o.shape[0]
    half = CMP_STRIDE * LANES
    kg = kg_ref[...]
    for kind, (src, dst) in enumerate(((kc2_ref, kc_o), (vc2_ref, vc_o))):
        t2 = jnp.concatenate([src[pl.ds(l, nc, stride=CMP_STRIDE), :].astype(BF16)
                              for l in range(CMP_STRIDE)], axis=1)
        first = _dot(t2, w1_ref[kind, 0:half, :])
        second = _dot(t2, w1_ref[kind, half:2 * half, :])
        const = _dot(pos_ref[kind], w1_ref[kind])[0:1, :]
        pre = first + pltpu.roll(second, nc - 1, 0) + const
        y = _dot(jax.nn.gelu(pre).astype(BF16), w2_ref[kind])
        if kind == 0:
            y = _rms(y, kg)
        dst[...] = y.astype(BF16)
    front = ks_o.shape[0] - ksl_ref.shape[0]
    for dst, val in ((ks_o, _rms(ksl_ref[...], kg)), (kw_o, _rms(kwn_ref[...], kg)),
                     (vs_o, vsl_ref[...]), (vw_o, vwn_ref[...])):
        dst[0:front, 0:LANES] = jnp.zeros((front, LANES), BF16)
        dst[front:, 0:LANES] = val.astype(BF16)
    ks_o[:, LANES:] = bm_ref[...]
    vs_o[:, LANES:] = one_ref[...]
    vw_o[:, LANES:] = one_ref[...]


def _nsa_prep(p4, w1, w2, pos, k_norm, block_mask, ones_col, layer):
    ng, b, s, _ = p4.shape
    padded = s + WINDOW
    nc = s // CMP_STRIDE
    half = CMP_STRIDE * LANES
    grp = lambda base: (lambda i, g: (base + g, i, 0, 0))
    small = jax.ShapeDtypeStruct((b, KV_HEADS, nc, LANES), BF16)
    full = jax.ShapeDtypeStruct((b, KV_HEADS, padded, LANES), BF16)
    wide = jax.ShapeDtypeStruct((b, KV_HEADS, padded, 2 * LANES), BF16)
    out_small = pl.BlockSpec((None, None, nc, LANES), lambda i, g: (i, g, 0, 0))
    out_full = pl.BlockSpec((None, None, padded, LANES), lambda i, g: (i, g, 0, 0))
    out_wide = pl.BlockSpec((None, None, padded, 2 * LANES), lambda i, g: (i, g, 0, 0))
    const = pl.BlockSpec((padded, LANES), lambda i, g: (0, 0))
    return pl.pallas_call(
        _nsa_prep_kernel,
        out_shape=(small, small, wide, wide, full, wide),
        grid=(b, KV_HEADS),
        in_specs=[pl.BlockSpec((None, None, s, LANES), grp(G_KCMP)),
                  pl.BlockSpec((None, None, s, LANES), grp(G_VCMP)),
                  pl.BlockSpec((None, None, s, LANES), grp(G_KSLC)),
                  pl.BlockSpec((None, None, s, LANES), grp(G_VSLC)),
                  pl.BlockSpec((None, None, s, LANES), grp(G_KWIN)),
                  pl.BlockSpec((None, None, s, LANES), grp(G_VWIN)),
                  pl.BlockSpec((None, 2, 2 * half, LANES), lambda i, g: (layer, 0, 0, 0)),
                  pl.BlockSpec((None, 2, LANES, LANES), lambda i, g: (layer, 0, 0, 0)),
                  pl.BlockSpec((None, 2, 8, 2 * half), lambda i, g: (layer, 0, 0, 0)),
                  pl.BlockSpec((None, 1, LANES), lambda i, g: (layer, 0, 0)),
                  const, const],
        out_specs=(out_small, out_small, out_wide, out_wide, out_full, out_wide),
        compiler_params=_cparams(("parallel", "parallel")),
        name="nsa_prep",
    )(p4, p4, p4, p4, p4, p4, w1, w2, pos, k_norm, block_mask, ones_col)


def _nsa_kernel(q_ref, gate_ref, kc_ref, vc_ref, ks_ref, vs_ref, kw_ref, vw_ref,
                bc_ref, bs_ref, bw_ref, ov_ref, qg_ref, o_ref, sc_ref, *, n_cmp, n_sel):
    tq = q_ref.shape[1]
    r3 = Q_PER_KV
    i = pl.program_id(2)
    nc = kc_ref.shape[0]
    rows = r3 * tq

    qg = qg_ref[...]
    qf = jnp.concatenate([_rms(q_ref[r], qg) * (HEAD_DIM ** -0.5) for r in range(r3)], axis=0)
    qb = qf.astype(BF16)

    qq = lax.broadcasted_iota(jnp.int32, (tq, tq), 0)
    kk = lax.broadcasted_iota(jnp.int32, (tq, tq), 1)

    s_c = _dot_nt(qb, kc_ref[...]).reshape(r3, tq, nc) + bc_ref[...]
    qpos_c = i * tq + lax.broadcasted_iota(jnp.int32, (tq, nc), 0)
    n_idx = lax.broadcasted_iota(jnp.int32, (tq, nc), 1)
    mask_c = ((qpos_c >= n_idx * CMP_STRIDE + (CMP_LEN - 1)) & (n_idx < n_cmp))[None]
    s_c = jnp.where(mask_c, s_c, NEG_BIG)
    m_c = jnp.max(s_c, axis=-1, keepdims=True)
    p_c = jnp.where(mask_c, jnp.exp(s_c - m_c), 0.0)
    p_c = p_c / jnp.maximum(jnp.sum(p_c, axis=-1, keepdims=True), 1e-30)
    o_c = _dot(p_c.reshape(rows, nc).astype(BF16), vc_ref[...])
    p_sum = p_c[0] + p_c[1] + p_c[2]
    p_hi = p_sum.astype(BF16)
    p_lo = (p_sum - p_hi.astype(F32)).astype(BF16)
    importance = _dot(p_hi, ov_ref[...]) + _dot(p_lo, ov_ref[...])

    cur = (i * tq + qq) // SEL_LEN
    forced = (kk == 0) | (kk == cur) | (kk == cur - 1)
    score = jnp.where(forced, SEL_FORCE, jnp.where(kk <= cur, importance, NEG_BIG))
    st = score.T

    def selection_queries(n_blocks):
        sub = 8
        n_blocks = min(n_blocks, n_sel)
        slabs = [st[v * sub:(v + 1) * sub, :] for v in range((n_blocks + sub - 1) // sub)]
        keep = [slab > 0.5 * NEG_BIG for slab in slabs]
        if n_blocks > SEL_TOPK:
            blk = lax.broadcasted_iota(jnp.int32, (sub, tq), 0)
            ranks = [jnp.zeros((sub, tq), F32) for _ in slabs]
            for b in range(n_blocks):
                row = st[b:b + 1, :]
                for v, slab in enumerate(slabs):
                    if v * sub > b:
                        beats = row >= slab
                    elif (v + 1) * sub - 1 <= b:
                        beats = row > slab
                    else:
                        beats = (row > slab) | ((row == slab) & (blk + v * sub > b))
                    ranks[v] = ranks[v] + jnp.where(beats, 1.0, 0.0)
            keep = [k & (r < float(SEL_TOPK)) for k, r in zip(keep, ranks)]
        drop_t = jnp.concatenate([jnp.where(k, 0.0, 1.0) for k in keep]
                                 + [jnp.ones((tq - sub * len(slabs), tq), F32)], axis=0)
        drop = drop_t.T.astype(BF16)
        return jnp.concatenate([qb, jnp.concatenate([drop] * r3, axis=0)], axis=1)

    def normalise(acc):
        den = jnp.maximum(acc[:, HEAD_DIM:HEAD_DIM + 1], 1e-30)
        return (acc[:, 0:HEAD_DIM] / den).reshape(r3, tq, HEAD_DIM)

    band = (N_WIN_PREV + 1) * tq
    front = N_WIN_PREV * tq
    far_keys = sc_ref.shape[-1]
    boff = pl.multiple_of(i * tq, tq)
    q_b = lax.broadcasted_iota(jnp.int32, (tq, band), 0)
    k_b = lax.broadcasted_iota(jnp.int32, (tq, band), 1)

    def lane_fold(x, op):
        out = x[..., 0:tq]
        for c in range(1, x.shape[-1] // tq):
            out = op(out, x[..., c * tq:(c + 1) * tq])
        return out

    n_far = jnp.maximum(i - 1, 0) // (far_keys // tq)

    def selected(nf):
        last_tile = (nf + 1) * (far_keys // tq)
        q_sel = selection_queries((last_tile + 1) * tq // SEL_LEN)
        first_near = front - (i * tq - nf * far_keys)
        ok_s = (k_b <= q_b + front) & (k_b >= first_near)
        def group(pieces, with_band):
            keys_of = lambda c, lo, hi: slice(front + c * far_keys + lo, front + c * far_keys + hi)
            mx = None
            if with_band:
                s_b = (_dot_nt(q_sel, ks_ref[pl.ds(boff, band), :]).reshape(r3, tq, band) + bs_ref[...]
                       + jnp.where(ok_s, 0.0, NEG_BIG)[None])
                mx = lane_fold(s_b, jnp.maximum)
            for c, lo, hi in pieces:
                s = _dot_nt(q_sel, ks_ref[keys_of(c, lo, hi), :]).reshape(r3, tq, hi - lo)
                sc_ref[c, :, :, lo:hi] = s
                fold = lane_fold(s, jnp.maximum)
                mx = fold if mx is None else jnp.maximum(mx, fold)
            m = jnp.max(mx, axis=-1, keepdims=True)
            acc = None
            if with_band:
                acc = _dot(jnp.exp(s_b - m).reshape(rows, band).astype(BF16), vs_ref[pl.ds(boff, band), :])
            for c, lo, hi in pieces:
                p = jnp.exp(sc_ref[c, :, :, lo:hi] - m).reshape(rows, hi - lo).astype(BF16)
                part = _dot(p, vs_ref[keys_of(c, lo, hi), :])
                acc = part if acc is None else acc + part
            return m.reshape(rows, 1), acc

        if nf == 0:
            return normalise(group([], True)[1])
        whole = lambda cs: [(c, 0, far_keys) for c in cs]
        if nf % 2:
            first, second = whole(range((nf + 1) // 2)), whole(range((nf + 1) // 2, nf))
        else:
            mid = nf // 2
            first = whole(range(mid)) + [(mid, 0, far_keys // 2)]
            second = [(mid, far_keys // 2, far_keys)] + whole(range(mid + 1, nf))
        m_b, acc_b = group(second, True)
        m_a, acc_a = group(first, False)
        m_s = jnp.maximum(m_a, m_b)
        return normalise(acc_a * jnp.exp(m_a - m_s) + acc_b * jnp.exp(m_b - m_s))

    ok_w = (k_b > q_b) & (k_b <= q_b + front) & (k_b >= front - i * tq)
    s_w = (_dot_nt(qb, kw_ref[pl.ds(boff, band), :]).reshape(r3, tq, band) + bw_ref[...]
           + jnp.where(ok_w, 0.0, NEG_BIG)[None])
    p_w = jnp.exp(s_w - jnp.max(s_w, axis=-1, keepdims=True))
    o_w = normalise(_dot(p_w.reshape(rows, band).astype(BF16), vw_ref[pl.ds(boff, band), :]))

    gates = jax.nn.sigmoid(gate_ref[...])
    o_c = o_c.reshape(r3, tq, HEAD_DIM)
    col = lambda branch, r: gates[:, LANE_GATE + branch * r3 + r:LANE_GATE + branch * r3 + r + 1]
    partial = [col(0, r) * o_c[r] + col(2, r) * o_w[r] for r in range(r3)]
    gate_s = [col(1, r) for r in range(r3)]

    for nf in range(sc_ref.shape[0]):
        @pl.when(n_far == nf)
        def _(nf=nf):
            o_s = selected(nf)
            for r in range(r3):
                o_ref[:, r * HEAD_DIM:(r + 1) * HEAD_DIM] = (partial[r] + gate_s[r] * o_s[r]).astype(BF16)


def _nsa(p4, prep, bias_c, bias_sel, bias_win, overlap, q_norm, layer, tq=WIN_BLOCK, far_keys=512):
    _, b, s, _ = p4.shape
    kc, vc, ks, vs, kw, vw = prep
    nc = kc.shape[2]
    r3 = Q_PER_KV
    n_sel = s // SEL_LEN
    n_cmp = (s - CMP_LEN) // CMP_STRIDE + 1
    band = (N_WIN_PREV + 1) * tq
    padded = ks.shape[2]
    kv_small = pl.BlockSpec((None, None, nc, LANES), lambda i, g, j: (i, g, 0, 0))
    kv_full = pl.BlockSpec((None, None, padded, LANES), lambda i, g, j: (i, g, 0, 0))
    kv_wide = pl.BlockSpec((None, None, padded, 2 * LANES), lambda i, g, j: (i, g, 0, 0))
    return pl.pallas_call(
        functools.partial(_nsa_kernel, n_cmp=n_cmp, n_sel=n_sel),
        out_shape=jax.ShapeDtypeStruct((b, s, N_HEADS_C * HEAD_DIM), BF16),
        grid=(b, KV_HEADS, s // tq),
        in_specs=[pl.BlockSpec((r3, None, tq, LANES), lambda i, g, j: (G_QC // r3 + g, i, j, 0)),
                  pl.BlockSpec((None, None, tq, LANES), lambda i, g, j: (G_SMALL + g, i, j, 0)),
                  kv_small, kv_small, kv_wide, kv_wide, kv_full, kv_wide,
                  pl.BlockSpec((r3, tq, nc), lambda i, g, j: (g, j, 0)),
                  pl.BlockSpec((r3, tq, band), lambda i, g, j: (g, 0, 0)),
                  pl.BlockSpec((r3, tq, band), lambda i, g, j: (g, 0, 0)),
                  pl.BlockSpec((nc, LANES), lambda i, g, j: (0, 0)),
                  pl.BlockSpec((None, 1, LANES), lambda i, g, j: (layer, 0, 0))],
        out_specs=pl.BlockSpec((None, tq, r3 * HEAD_DIM), lambda i, g, j: (i, j, g)),
        scratch_shapes=[pltpu.VMEM((s // far_keys, r3, tq, far_keys), F32)],
        compiler_params=_cparams(("parallel", "parallel", "arbitrary")),
        name="nsa",
    )(p4, p4, kc, vc, ks, vs, kw, vw, bias_c, bias_sel, bias_win, overlap, q_norm)


def _interleave(fragments):
    live = list(fragments)
    while live:
        for fragment in list(live):
            try:
                next(fragment)
            except StopIteration:
                live.remove(fragment)


def _nsa_pair_kernel(q_ref, gate_ref, kc_ref, vc_ref, ks_ref, vs_ref, kw_ref, vw_ref,
                     bc_ref, bs_ref, bw_ref, ov_ref, qg_ref, o_ref, sc_ref, *, n_cmp, n_sel):
    tq = q_ref.shape[1]
    r3 = Q_PER_KV
    i = pl.program_id(1)
    nc = kc_ref.shape[1]
    rows = r3 * tq
    band = (N_WIN_PREV + 1) * tq
    front = N_WIN_PREV * tq
    far_keys = sc_ref.shape[-1]
    boff = pl.multiple_of(i * tq, tq)
    n_far = jnp.maximum(i - 1, 0) // (far_keys // tq)

    qq = lax.broadcasted_iota(jnp.int32, (tq, tq), 0)
    kk = lax.broadcasted_iota(jnp.int32, (tq, tq), 1)
    q_b = lax.broadcasted_iota(jnp.int32, (tq, band), 0)
    k_b = lax.broadcasted_iota(jnp.int32, (tq, band), 1)
    qpos_c = i * tq + lax.broadcasted_iota(jnp.int32, (tq, nc), 0)
    n_idx = lax.broadcasted_iota(jnp.int32, (tq, nc), 1)
    mask_c = ((qpos_c >= n_idx * CMP_STRIDE + (CMP_LEN - 1)) & (n_idx < n_cmp))[None]
    ok_w = (k_b > q_b) & (k_b <= q_b + front) & (k_b >= front - i * tq)
    cur = (i * tq + qq) // SEL_LEN
    forced = (kk == 0) | (kk == cur) | (kk == cur - 1)
    qg = qg_ref[...]

    def lane_fold(x, op):
        out = x[..., 0:tq]
        for c in range(1, x.shape[-1] // tq):
            out = op(out, x[..., c * tq:(c + 1) * tq])
        return out

    def normalise(acc):
        den = jnp.maximum(acc[:, HEAD_DIM:HEAD_DIM + 1], 1e-30)
        return (acc[:, 0:HEAD_DIM] / den).reshape(r3, tq, HEAD_DIM)

    kept = [{} for _ in range(KV_HEADS)]

    def before_selection(g):
        heads = slice(g * r3, (g + 1) * r3)
        qf = jnp.concatenate([_rms(q_ref[g * r3 + r], qg) * (HEAD_DIM ** -0.5) for r in range(r3)], axis=0)
        qb = qf.astype(BF16)
        yield
        s_c = _dot_nt(qb, kc_ref[g]).reshape(r3, tq, nc) + bc_ref[heads]
        s_w = (_dot_nt(qb, kw_ref[g, pl.ds(boff, band), :]).reshape(r3, tq, band) + bw_ref[heads]
               + jnp.where(ok_w, 0.0, NEG_BIG)[None])
        yield
        s_c = jnp.where(mask_c, s_c, NEG_BIG)
        m_c = jnp.max(s_c, axis=-1, keepdims=True)
        p_c = jnp.where(mask_c, jnp.exp(s_c - m_c), 0.0)
        p_c = p_c / jnp.maximum(jnp.sum(p_c, axis=-1, keepdims=True), 1e-30)
        yield
        o_c = _dot(p_c.reshape(rows, nc).astype(BF16), vc_ref[g]).reshape(r3, tq, HEAD_DIM)
        p_sum = p_c[0] + p_c[1] + p_c[2]
        p_hi = p_sum.astype(BF16)
        p_lo = (p_sum - p_hi.astype(F32)).astype(BF16)
        importance = _dot(p_hi, ov_ref[...]) + _dot(p_lo, ov_ref[...])
        yield
        p_w = jnp.exp(s_w - jnp.max(s_w, axis=-1, keepdims=True))
        o_w = normalise(_dot(p_w.reshape(rows, band).astype(BF16), vw_ref[g, pl.ds(boff, band), :]))
        yield
        score = jnp.where(forced, SEL_FORCE, jnp.where(kk <= cur, importance, NEG_BIG))
        n_rows = ((n_sel + 7) // 8) * 8
        st = score.T[0:n_rows, :]
        sub = 8
        slabs = [st[v * sub:(v + 1) * sub, :] for v in range(n_rows // sub)]
        blk = lax.broadcasted_iota(jnp.int32, (sub, tq), 0)
        ranks = [jnp.zeros((sub, tq), F32) for _ in slabs]
        for b in range(n_sel):
            row = st[b:b + 1, :]
            for v, slab in enumerate(slabs):
                if v * sub > b:
                    beats = row >= slab
                elif (v + 1) * sub - 1 <= b:
                    beats = row > slab
                else:
                    beats = (row > slab) | ((row == slab) & (blk + v * sub > b))
                ranks[v] = ranks[v] + jnp.where(beats, 1.0, 0.0)
            if b % 8 == 7:
                yield
        rank = jnp.concatenate(ranks, axis=0)
        drop_t = jnp.where((rank < float(min(SEL_TOPK, n_sel))) & (st > 0.5 * NEG_BIG), 0.0, 1.0)
        if n_rows < tq:
            drop_t = jnp.concatenate([drop_t, jnp.zeros((tq - n_rows, tq), F32)], axis=0)
        drop = drop_t.T.astype(BF16)
        q_sel = jnp.concatenate([qb, jnp.concatenate([drop] * r3, axis=0)], axis=1)
        gates = jax.nn.sigmoid(gate_ref[g])
        col = lambda branch, r: gates[:, LANE_GATE + branch * r3 + r:LANE_GATE + branch * r3 + r + 1]
        kept[g].update(q_sel=q_sel, gate_s=[col(1, r) for r in range(r3)],
                       partial=[col(0, r) * o_c[r] + col(2, r) * o_w[r] for r in range(r3)])

    _interleave(before_selection(g) for g in range(KV_HEADS))

    def half_softmax(g, pieces, with_band, nf, result):
        q_sel = kept[g]["q_sel"]
        keys_of = lambda c, lo, hi: slice(front + c * far_keys + lo, front + c * far_keys + hi)
        mx = None
        if with_band:
            first_near = front - (i * tq - nf * far_keys)
            ok_s = (k_b <= q_b + front) & (k_b >= first_near)
            s_b = (_dot_nt(q_sel, ks_ref[g, pl.ds(boff, band), :]).reshape(r3, tq, band)
                   + bs_ref[g * r3:(g + 1) * r3] + jnp.where(ok_s, 0.0, NEG_BIG)[None])
            mx = lane_fold(s_b, jnp.maximum)
            yield
        for c, lo, hi in pieces:
            s = _dot_nt(q_sel, ks_ref[g, keys_of(c, lo, hi), :]).reshape(r3, tq, hi - lo)
            sc_ref[g, c, :, :, lo:hi] = s
            fold = lane_fold(s, jnp.maximum)
            mx = fold if mx is None else jnp.maximum(mx, fold)
            yield
        m = jnp.max(mx, axis=-1, keepdims=True)
        acc = None
        if with_band:
            acc = _dot(jnp.exp(s_b - m).reshape(rows, band).astype(BF16), vs_ref[g, pl.ds(boff, band), :])
            yield
        for c, lo, hi in pieces:
            p = jnp.exp(sc_ref[g, c, :, :, lo:hi] - m).reshape(rows, hi - lo).astype(BF16)
            part = _dot(p, vs_ref[g, keys_of(c, lo, hi), :])
            acc = part if acc is None else acc + part
            yield
        result.append((m.reshape(rows, 1), acc))

    def selected(nf):
        whole = lambda cs: [(c, 0, far_keys) for c in cs]
        if nf == 0:
            first, second = None, []
        elif nf % 2:
            first, second = whole(range((nf + 1) // 2)), whole(range((nf + 1) // 2, nf))
        else:
            mid = nf // 2
            first = whole(range(mid)) + [(mid, 0, far_keys // 2)]
            second = [(mid, far_keys // 2, far_keys)] + whole(range(mid + 1, nf))
        halves = [([], []) for _ in range(KV_HEADS)]
        fragments = []
        for g in range(KV_HEADS):
            fragments.append(half_softmax(g, second, True, nf, halves[g][1]))
            if first is not None:
                fragments.append(half_softmax(g, first, False, nf, halves[g][0]))
        _interleave(fragments)
        for g in range(KV_HEADS):
            (m_b, acc_b), = halves[g][1]
            if first is None:
                o_s = normalise(acc_b)
            else:
                (m_a, acc_a), = halves[g][0]
                m_s = jnp.maximum(m_a, m_b)
                o_s = normalise(acc_a * jnp.exp(m_a - m_s) + acc_b * jnp.exp(m_b - m_s))
            for r in range(r3):
                out = kept[g]["partial"][r] + kept[g]["gate_s"][r] * o_s[r]
                o_ref[:, (g * r3 + r) * HEAD_DIM:(g * r3 + r + 1) * HEAD_DIM] = out.astype(BF16)

    for nf in range(sc_ref.shape[1]):
        pl.when(n_far == nf)(functools.partial(selected, nf))


def _nsa_pair(p4, prep, bias_c, bias_sel, bias_win, overlap, q_norm, layer, tq=WIN_BLOCK, far_keys=512):
    _, b, s, _ = p4.shape
    kc, vc, ks, vs, kw, vw = prep
    nc = kc.shape[2]
    nq = N_HEADS_C
    n_sel = s // SEL_LEN
    n_cmp = (s - CMP_LEN) // CMP_STRIDE + 1
    band = (N_WIN_PREV + 1) * tq
    padded = ks.shape[2]
    per_batch = lambda *tail: pl.BlockSpec((None, KV_HEADS) + tail, lambda i, j: (i, 0, 0, 0))
    resident = lambda *tail: pl.BlockSpec((None, KV_HEADS) + tail, lambda i, j: (i, 0, 0, 0),
                                          pipeline_mode=pl.Buffered(1))
    fixed = lambda *shape: pl.BlockSpec(shape, lambda i, j: (0,) * len(shape), pipeline_mode=pl.Buffered(1))
    return pl.pallas_call(
        functools.partial(_nsa_pair_kernel, n_cmp=n_cmp, n_sel=n_sel),
        out_shape=jax.ShapeDtypeStruct((b, s, nq * HEAD_DIM), BF16),
        grid=(b, s // tq),
        in_specs=[pl.BlockSpec((nq, None, tq, LANES), lambda i, j: (G_QC // nq, i, j, 0)),
                  pl.BlockSpec((KV_HEADS, None, tq, LANES), lambda i, j: (G_SMALL // KV_HEADS, i, j, 0)),
                  per_batch(nc, LANES), per_batch(nc, LANES),
                  resident(padded, 2 * LANES), resident(padded, 2 * LANES),
                  resident(padded, LANES), resident(padded, 2 * LANES),
                  pl.BlockSpec((nq, tq, nc), lambda i, j: (0, j, 0)),
                  fixed(nq, tq, band), fixed(nq, tq, band), fixed(nc, LANES),
                  pl.BlockSpec((None, 1, LANES), lambda i, j: (layer, 0, 0))],
        out_specs=pl.BlockSpec((None, tq, nq * HEAD_DIM), lambda i, j: (i, j, 0)),
        scratch_shapes=[pltpu.VMEM((KV_HEADS, s // far_keys, Q_PER_KV, tq, far_keys), F32)],
        compiler_params=_cparams(("parallel", "arbitrary")),
        name="nsa",
    )(p4, p4, kc, vc, ks, vs, kw, vw, bias_c, bias_sel, bias_win, overlap, q_norm)


def _t5_bucket(dist):
    n = jnp.maximum(dist, 0)
    max_exact = RPB_BUCKETS // 2
    log_ratio = jnp.log(jnp.maximum(n, 1).astype(F32) / max_exact) / math.log(RPB_MAX_DIST / max_exact)
    large = jnp.minimum(max_exact + (log_ratio * (RPB_BUCKETS - max_exact)).astype(jnp.int32), RPB_BUCKETS - 1)
    return jnp.where(n < max_exact, n, large)


def _bias_tables(rel_bias, s, tq):
    table = rel_bias.astype(F32)
    buckets = jnp.arange(RPB_BUCKETS, dtype=jnp.int32)[:, None, None]

    def look(dist):
        onehot = (_t5_bucket(dist)[None] == buckets).astype(F32)
        return jnp.einsum("nh,nqk->hqk", table, onehot, precision=lax.Precision.HIGHEST)

    nc = s // CMP_STRIDE
    pos = jnp.arange(s, dtype=jnp.int32)
    cmp_end = jnp.arange(nc, dtype=jnp.int32) * CMP_STRIDE + (CMP_LEN - 1)
    bias_c = look(pos[:, None] - cmp_end[None, :])
    q = jnp.arange(tq, dtype=jnp.int32)[:, None]
    kb = jnp.arange((N_WIN_PREV + 1) * tq, dtype=jnp.int32)[None, :]
    bias_win = look(N_WIN_PREV * tq + q - kb)
    bias_sel = bias_win - table[RPB_BUCKETS - 1][:, None, None]
    return bias_c, bias_sel, bias_win


def _selection_constants(s, tq):
    nc = s // CMP_STRIDE
    n_cmp = (s - CMP_LEN) // CMP_STRIDE + 1
    n_sel = s // SEL_LEN
    cmp_start = np.arange(nc) * CMP_STRIDE
    sel_start = np.arange(LANES) * SEL_LEN
    overlap = ((cmp_start[:, None] < sel_start[None, :] + SEL_LEN)
               & (cmp_start[:, None] + CMP_LEN > sel_start[None, :])
               & (np.arange(nc)[:, None] < n_cmp) & (np.arange(LANES)[None, :] < n_sel))
    key_pos = np.arange(-WINDOW, s)[:, None]
    in_block = (np.arange(LANES)[None, :] == (key_pos // SEL_LEN)) & (key_pos >= 0)
    block_mask = jnp.where(jnp.asarray(in_block), NEG_BIG, 0.0).astype(BF16)
    ones_col = jnp.asarray((np.arange(LANES)[None, :] == 0) & (key_pos >= 0), BF16)
    return jnp.asarray(overlap, BF16), block_mask, ones_col


def _arrange_w_in(w_in):
    depth, d, _ = w_in.shape
    w_in = w_in.astype(BF16)
    da, db, dc, dkv = N_HEADS_A * 128, N_HEADS_B * 128, N_HEADS_C * 128, KV_HEADS * 128
    o_ba = 4 * da
    o_aa = o_ba + N_HEADS_A
    o_ub = o_aa + N_HEADS_A
    o_qc = o_ub + 2 * db
    o_kc = o_qc + dc
    o_gc = o_kc + 6 * dkv
    sl = lambda lo, n: w_in[:, :, lo:lo + n]
    gate = w_in[:, :, o_gc:o_gc + 3 * N_HEADS_C].reshape(depth, d, 3, KV_HEADS, Q_PER_KV)
    zeros = lambda n: jnp.zeros((depth, d, n), w_in.dtype)
    small0 = jnp.concatenate([sl(o_ba, 2 * N_HEADS_A), gate[:, :, :, 0, :].reshape(depth, d, 3 * Q_PER_KV),
                              zeros(LANES - 2 * N_HEADS_A - 3 * Q_PER_KV)], axis=-1)
    small1 = jnp.concatenate([zeros(LANE_GATE), gate[:, :, :, 1, :].reshape(depth, d, 3 * Q_PER_KV),
                              zeros(LANES - LANE_GATE - 3 * Q_PER_KV)], axis=-1)
    parts = [sl(0, 4 * da),
             sl(o_qc, dc),
             sl(o_kc, dkv),
             sl(o_ub, 2 * db),
             sl(o_kc + dkv, 5 * dkv),
             small0, small1]
    return jnp.concatenate(parts, axis=-1).astype(BF16)


def kernel(x, attn_norm, w_in, conv_a, a_log, dt_bias, gdn_norm, sgu_ln_g, sgu_ln_b, sgu_w, sgu_b,
           nsa_q_norm, nsa_k_norm, cmp_pos, cmp_w1, cmp_w2, rel_bias, w_out, mlp_norm, w_up, w_down):
    b, s, d = x.shape
    depth = w_in.shape[0]
    t = b * s
    tq = 128

    w_in_r = _arrange_w_in(w_in)
    w_out_b = w_out.astype(BF16)
    w_up_b = w_up.astype(BF16)
    w_down_b = w_down.astype(BF16)
    attn_g = attn_norm.reshape(depth, 1, d)
    mlp_g = mlp_norm.reshape(depth, 1, d)
    conv_r = conv_a.reshape(depth, CONV_WIDTH * 3 * N_HEADS_A, LANES)
    pad_to = lambda v, lo: jnp.pad(v, ((0, 0), (lo, LANES - lo - v.shape[1])))
    ab = jnp.stack([pad_to(a_log, LANE_DECAY), pad_to(dt_bias, LANE_DECAY)], axis=1)
    gdn_g = gdn_norm.reshape(depth, 1, LANES)
    ln_g = sgu_ln_g.reshape(depth, N_HEADS_B, LANES)
    ln_b = sgu_ln_b.reshape(depth, N_HEADS_B, LANES)
    sgu_bias = jnp.broadcast_to(sgu_b[..., None], sgu_b.shape + (LANES,))
    q_g = nsa_q_norm.reshape(depth, 1, LANES)
    k_g = nsa_k_norm.reshape(depth, 1, LANES)
    w1_b = cmp_w1.astype(BF16)
    w2_b = cmp_w2.astype(BF16)
    pos_b = jnp.broadcast_to(cmp_pos.reshape(depth, 2, 1, CMP_LEN * LANES),
                             (depth, 2, 8, CMP_LEN * LANES)).astype(BF16)
    bias_c, bias_sel, bias_win = _bias_tables(rel_bias, s, tq)
    overlap, block_mask, ones_col = _selection_constants(s, tq)

    x2 = x.reshape(t, d)
    for layer in range(depth):
        p = _inproj(x2, attn_g, w_in_r, layer)
        p4 = p.reshape(N_GROUPS, b, s, LANES)
        mix_a = _gdn(p4, conv_r, ab, gdn_g, layer)
        mix_b = _sgu(p4, ln_g, ln_b, sgu_w, sgu_bias, layer)
        prep = _nsa_prep(p4, w1_b, w2_b, pos_b, k_g, block_mask, ones_col, layer)
        mix_c = _nsa_pair(p4, prep, bias_c, bias_sel, bias_win, overlap, q_g, layer, tq=tq)
        x2 = _outproj(x2, mix_a.reshape(t, -1), mix_b.reshape(t, -1), mix_c.reshape(t, -1), w_out_b, layer)
        x2 = _mlp(x2, mlp_g, w_up_b, w_down_b, layer)
    return x2.reshape(b, s, d)
```

```python
import functools
import math

import numpy as np
import jax
import jax.numpy as jnp
from jax import lax
from jax.experimental import pallas as pl
from jax.experimental.pallas import tpu as pltpu

F32 = jnp.float32
BF16 = jnp.bfloat16

LANES = 128
HEAD_DIM = 128
N_HEADS_A = 6
N_HEADS_B = 4
N_HEADS_C = 6
KV_HEADS = 2
Q_PER_KV = N_HEADS_C // KV_HEADS
CONV_WIDTH = 4
GDN_CHUNK = 64
SGU_CHUNK = 128
CMP_LEN = 32
CMP_STRIDE = 16
SEL_LEN = 64
SEL_TOPK = 16
WINDOW = 512
WIN_BLOCK = 128
N_WIN_PREV = WINDOW // WIN_BLOCK
RPB_BUCKETS = 32
RPB_MAX_DIST = 128
NORM_EPS = 1e-6
NEG_BIG = -1e30
SEL_FORCE = 1e9
VMEM_LIMIT = 56 * 1024 * 1024

G_QA, G_KA, G_VA, G_ZA = 0, 6, 12, 18
G_QC = 24
G_KCMP = 30
G_UB, G_VB = 32, 36
G_VCMP = 40
G_KSLC, G_VSLC, G_KWIN, G_VWIN = 42, 44, 46, 48
G_SMALL = 50
N_GROUPS = 52
LANE_BETA, LANE_DECAY, LANE_GATE = 0, 6, 12


def _dot(a, b, precision=None):
    return jnp.dot(a, b, preferred_element_type=F32, precision=precision)


def _dot_nt(a, b):
    return lax.dot_general(a, b, (((1,), (1,)), ((), ())), preferred_element_type=F32)


def _dot_tn(a, b):
    return lax.dot_general(a, b, (((0,), (0,)), ((), ())), preferred_element_type=F32)


def _rms(x, gain):
    return x * lax.rsqrt(jnp.mean(x * x, axis=-1, keepdims=True) + NORM_EPS) * gain


def _cparams(sem):
    return pltpu.CompilerParams(dimension_semantics=sem, vmem_limit_bytes=VMEM_LIMIT)


def _inproj_kernel(x0_ref, xn_ref, g_ref, w_ref, o_ref, h_ref):
    i, j = pl.program_id(0), pl.program_id(1)
    part = xn_ref.shape[0] // pl.num_programs(1)

    @pl.when((i == 0) & (j == 0))
    def _():
        h_ref[0] = _rms(x0_ref[...], g_ref[...]).astype(BF16)

    cur = i % 2
    r = _dot(h_ref[cur], w_ref[...])
    for c in range(o_ref.shape[0]):
        o_ref[c] = r[:, c * LANES:(c + 1) * LANES]

    rows = pl.ds(pl.multiple_of(j * part, part), part)
    h_ref[1 - cur, rows, :] = _rms(xn_ref[rows, :], g_ref[...]).astype(BF16)


def _inproj(x2, gain, w, layer, tm=512, tn=13 * LANES):
    t, d = x2.shape
    n = w.shape[-1]
    last = t // tm - 1
    return pl.pallas_call(
        _inproj_kernel,
        out_shape=jax.ShapeDtypeStruct((n // LANES, t, LANES), F32),
        grid=(t // tm, n // tn),
        in_specs=[pl.BlockSpec((tm, d), lambda i, j: (0, 0), pipeline_mode=pl.Buffered(1)),
                  pl.BlockSpec((tm, d), lambda i, j: (jnp.minimum(i + 1, last), 0)),
                  pl.BlockSpec((None, 1, d), lambda i, j: (layer, 0, 0)),
                  pl.BlockSpec((None, d, tn), lambda i, j: (layer, 0, j))],
        out_specs=pl.BlockSpec((tn // LANES, tm, LANES), lambda i, j: (j, i, 0)),
        scratch_shapes=[pltpu.VMEM((2, tm, d), BF16)],
        compiler_params=_cparams(("arbitrary", "arbitrary")),
        name="inproj",
    )(x2, x2, gain, w)


def _outproj_kernel(x_ref, a_ref, b_ref, c_ref, w_ref, o_ref):
    da, db = a_ref.shape[1], b_ref.shape[1]
    acc = _dot(a_ref[...], w_ref[0:da, :])
    acc += _dot(b_ref[...], w_ref[da:da + db, :])
    acc += _dot(c_ref[...], w_ref[da + db:, :])
    o_ref[...] = x_ref[...] + acc


def _outproj(x2, ma, mb, mc, w_out, layer, tm=512):
    t, d = x2.shape
    da, db, dc = ma.shape[1], mb.shape[1], mc.shape[1]
    return pl.pallas_call(
        _outproj_kernel,
        out_shape=jax.ShapeDtypeStruct((t, d), F32),
        grid=(t // tm,),
        in_specs=[pl.BlockSpec((tm, d), lambda i: (i, 0)),
                  pl.BlockSpec((tm, da), lambda i: (i, 0)),
                  pl.BlockSpec((tm, db), lambda i: (i, 0)),
                  pl.BlockSpec((tm, dc), lambda i: (i, 0)),
                  pl.BlockSpec((None, da + db + dc, d), lambda i: (layer, 0, 0))],
        out_specs=pl.BlockSpec((tm, d), lambda i: (i, 0)),
        compiler_params=_cparams(("parallel",)),
        name="outproj",
    )(x2, ma, mb, mc, w_out)


def _mlp_kernel(x_ref, xn_ref, g_ref, wu_ref, wd_ref, o_ref, h_ref):
    i, k = pl.program_id(0), pl.program_id(1)
    part = xn_ref.shape[0] // pl.num_programs(1)

    @pl.when((i == 0) & (k == 0))
    def _():
        h_ref[0] = _rms(x_ref[...], g_ref[...]).astype(BF16)

    @pl.when(k == 0)
    def _():
        o_ref[...] = x_ref[...]

    cur = i % 2
    a = jnp.maximum(_dot(h_ref[cur], wu_ref[...]), 0.0)
    o_ref[...] += _dot((a * a).astype(BF16), wd_ref[...])

    rows = pl.ds(pl.multiple_of(k * part, part), part)
    h_ref[1 - cur, rows, :] = _rms(xn_ref[rows, :], g_ref[...]).astype(BF16)


def _mlp(x2, gain, w_up, w_down, layer, tm=512, tf=1024):
    t, d = x2.shape
    f = w_up.shape[-1]
    last = t // tm - 1
    return pl.pallas_call(
        _mlp_kernel,
        out_shape=jax.ShapeDtypeStruct((t, d), F32),
        grid=(t // tm, f // tf),
        in_specs=[pl.BlockSpec((tm, d), lambda i, k: (i, 0)),
                  pl.BlockSpec((tm, d), lambda i, k: (jnp.minimum(i + 1, last), 0)),
                  pl.BlockSpec((None, 1, d), lambda i, k: (layer, 0, 0)),
                  pl.BlockSpec((None, d, tf), lambda i, k: (layer, 0, k)),
                  pl.BlockSpec((None, tf, d), lambda i, k: (layer, k, 0))],
        out_specs=pl.BlockSpec((tm, d), lambda i, k: (i, 0)),
        scratch_shapes=[pltpu.VMEM((2, tm, d), BF16)],
        compiler_params=_cparams(("arbitrary", "arbitrary")),
        name="mlp",
    )(x2, x2, gain, w_up, w_down)


def _sgu_kernel(p_ref, lg_ref, lb_ref, w_ref, b_ref, o_ref):
    ts = p_ref.shape[1]
    nb = N_HEADS_B
    v = [jax.nn.gelu(p_ref[nb + g]) for g in range(nb)]
    width = float(nb * LANES)
    mu = sum(jnp.sum(vg, axis=-1, keepdims=True) for vg in v) / width
    var = sum(jnp.sum(jnp.square(vg - mu), axis=-1, keepdims=True) for vg in v) / width
    inv = lax.rsqrt(var + NORM_EPS)
    row = lax.broadcasted_iota(jnp.int32, (SGU_CHUNK, SGU_CHUNK), 0)
    col = lax.broadcasted_iota(jnp.int32, (SGU_CHUNK, SGU_CHUNK), 1)
    causal = col <= row
    for g in range(nb):
        vn = ((v[g] - mu) * inv * lg_ref[g:g + 1, :] + lb_ref[g:g + 1, :]).astype(BF16)
        wg = jnp.where(causal, w_ref[g], 0.0).astype(BF16)
        for c in range(ts // SGU_CHUNK):
            rows = slice(c * SGU_CHUNK, (c + 1) * SGU_CHUNK)
            mixed = _dot(wg, vn[rows]) + b_ref[g]
            u = jax.nn.gelu(p_ref[g, rows, :])
            o_ref[rows, g * LANES:(g + 1) * LANES] = (u * mixed).astype(BF16)


def _sgu(p4, ln_g, ln_b, w_s, b_s, layer, ts=512):
    _, b, s, _ = p4.shape
    nb = N_HEADS_B
    return pl.pallas_call(
        _sgu_kernel,
        out_shape=jax.ShapeDtypeStruct((b, s, nb * LANES), BF16),
        grid=(b, s // ts),
        in_specs=[pl.BlockSpec((2 * nb, None, ts, LANES), lambda i, j: (G_UB // (2 * nb), i, j, 0)),
                  pl.BlockSpec((None, nb, LANES), lambda i, j: (layer, 0, 0)),
                  pl.BlockSpec((None, nb, LANES), lambda i, j: (layer, 0, 0)),
                  pl.BlockSpec((None, nb, SGU_CHUNK, SGU_CHUNK), lambda i, j: (layer, 0, 0, 0)),
                  pl.BlockSpec((None, nb, SGU_CHUNK, LANES), lambda i, j: (layer, 0, 0, 0))],
        out_specs=pl.BlockSpec((None, ts, nb * LANES), lambda i, j: (i, j, 0)),
        compiler_params=_cparams(("parallel", "parallel")),
        name="sgu",
    )(p4, ln_g, ln_b, w_s, b_s)


def _gdn_kernel(a_ref, sm_ref, cw_ref, ab_ref, ng_ref, o_ref,
                xbuf, q_s, k_s, v_s, beta_s, gcum_s, u_s, w_s, qg_s, kd_s, a_s, st_ref, *, batch):
    ts = a_ref.shape[1]
    ck = GDN_CHUNK
    nh = N_HEADS_A
    pad = 8

    @pl.when(pl.program_id(1) == 0)
    def _():
        xbuf[:, 0:pad, :] = jnp.zeros((3 * nh, pad, LANES), F32)
        st_ref[...] = jnp.zeros_like(st_ref)

    xbuf[:, pad:pad + ts, :] = a_ref[0:3 * nh]
    dests = (q_s, k_s, v_s)
    for grp in range(3 * nh):
        acc = None
        for j in range(CONV_WIDTH):
            lo = pad - (CONV_WIDTH - 1) + j
            term = xbuf[grp, lo:lo + ts, :] * cw_ref[j * 3 * nh + grp:j * 3 * nh + grp + 1, :]
            acc = term if acc is None else acc + term
        y = acc * jax.nn.sigmoid(acc)
        kind, h = divmod(grp, nh)
        if kind < 2:
            y = y * lax.rsqrt(jnp.sum(y * y, axis=-1, keepdims=True) + NORM_EPS)
        dests[kind][h] = y
    xbuf[:, 0:pad, :] = xbuf[:, ts:ts + pad, :]

    sm = sm_ref[...]
    beta_s[...] = jax.nn.sigmoid(sm)
    z = sm + ab_ref[1:2, :]
    softplus = jnp.maximum(z, 0.0) + jnp.log1p(jnp.exp(-jnp.abs(z)))
    g_all = -jnp.exp(ab_ref[0:1, :]) * softplus

    ii = lax.broadcasted_iota(jnp.int32, (ck, ck), 0)
    jj = lax.broadcasted_iota(jnp.int32, (ck, ck), 1)
    incl = jj <= ii
    strict = jj < ii
    tri = jnp.where(incl, 1.0, 0.0).astype(F32)
    for c in range(ts // ck):
        rows = slice(c * ck, (c + 1) * ck)
        gcum_s[rows, :] = _dot(tri, g_all[rows], precision=lax.Precision.HIGHEST)

    scale = HEAD_DIM ** -0.5
    gain = ng_ref[...]
    bf = lambda m: m.astype(BF16)
    same = lambda size: (ii // size) == (jj // size)
    n_chunks = ts // ck
    rows_of = lambda c: slice(c * ck, (c + 1) * ck)
    dcol = lambda h: slice(LANE_DECAY + h, LANE_DECAY + h + 1)

    for c0 in range(0, n_chunks, batch):
        gc = {c: gcum_s[rows_of(c), :] for c in range(c0, c0 + batch)}
        gct = {c: gc[c].T for c in gc}
        items = [(c, h) for c in range(c0, c0 + batch) for h in range(nh)]
        kk, qk = {}, {}
        for c, h in items:
            g_col = gc[c][:, dcol(h)]
            q = q_s[h, rows_of(c), :] * scale
            k = k_s[h, rows_of(c), :]
            kbf = bf(k)
            kk[c, h] = _dot_nt(bf(k * beta_s[rows_of(c), LANE_BETA + h:LANE_BETA + h + 1]), kbf)
            qk[c, h] = _dot_nt(bf(q), kbf)
            qg_s[h, rows_of(c), :] = bf(q * jnp.exp(g_col))
            kd_s[h, rows_of(c), :] = bf(k * jnp.exp(gc[c][ck - 1:ck, dcol(h)] - g_col))
        lower, diag, p1 = {}, {}, {}
        for it in items:
            c, h = it
            decay = jnp.exp(jnp.where(incl, gc[c][:, dcol(h)] - gct[c][dcol(h), :], NEG_BIG))
            lower[it] = jnp.where(strict, kk[it] * decay, 0.0)
            a_s[h, rows_of(c), :] = bf(jnp.where(incl, qk[it] * decay, 0.0))
            diag[it] = jnp.where(same(8), lower[it], 0.0)
            p1[it] = _dot(bf(diag[it]), bf(diag[it]))
        p2 = {it: _dot(bf(p1[it]), bf(p1[it])) for it in items}
        dp1 = {it: _dot(bf(diag[it]), bf(p1[it])) for it in items}
        na = {it: p1[it] - diag[it] - dp1[it] for it in items}
        nap2 = {it: _dot(bf(na[it]), bf(p2[it])) for it in items}
        n = {it: na[it] + p2[it] + nap2[it] for it in items}
        size = 8
        while size < ck:
            off = same(2 * size) & ~same(size)
            cm = {it: jnp.where(off, lower[it], 0.0) for it in items}
            y = {it: cm[it] + _dot(bf(n[it]), bf(cm[it])) for it in items}
            yn = {it: _dot(bf(y[it]), bf(n[it])) for it in items}
            n = {it: n[it] - (y[it] + yn[it]) for it in items}
            size *= 2
        for it in items:
            c, h = it
            beta = beta_s[rows_of(c), LANE_BETA + h:LANE_BETA + h + 1]
            nb = bf(n[it])
            vb = v_s[h, rows_of(c), :] * beta
            kw = k_s[h, rows_of(c), :] * (beta * jnp.exp(gc[c][:, dcol(h)]))
            u_s[h, rows_of(c), :] = vb + _dot(nb, bf(vb))
            w_s[h, rows_of(c), :] = bf(kw + _dot(nb, bf(kw)))

    heads = range(nh)
    for c in range(n_chunks):
        rows = rows_of(c)
        state = [st_ref[h] for h in heads]
        sb = [bf(s) for s in state]
        ws = [_dot(w_s[h, rows, :], sb[h]) for h in heads]
        qs = [_dot(qg_s[h, rows, :], sb[h]) for h in heads]
        vnb = [bf(u_s[h, rows, :] - ws[h]) for h in heads]
        kv = [_dot_tn(kd_s[h, rows, :], vnb[h]) for h in heads]
        av = [_dot(a_s[h, rows, :], vnb[h]) for h in heads]
        for h in heads:
            g_last = gcum_s[(c + 1) * ck - 1:(c + 1) * ck, dcol(h)]
            st_ref[h] = state[h] * jnp.exp(g_last) + kv[h]
            zg = a_ref[3 * nh + h, rows, :]
            out = _rms(qs[h] + av[h], gain) * (zg * jax.nn.sigmoid(zg))
            o_ref[rows, h * LANES:(h + 1) * LANES] = out.astype(BF16)


def _gdn(p4, conv_w, ab, norm_g, layer, ts=512, batch=4):
    _, b, s, _ = p4.shape
    nh = N_HEADS_A
    return pl.pallas_call(
        functools.partial(_gdn_kernel, batch=batch),
        out_shape=jax.ShapeDtypeStruct((b, s, nh * LANES), BF16),
        grid=(b, s // ts),
        in_specs=[pl.BlockSpec((4 * nh, None, ts, LANES), lambda i, j: (0, i, j, 0)),
                  pl.BlockSpec((None, None, ts, LANES), lambda i, j: (G_SMALL, i, j, 0)),
                  pl.BlockSpec((None, CONV_WIDTH * 3 * nh, LANES), lambda i, j: (layer, 0, 0)),
                  pl.BlockSpec((None, 2, LANES), lambda i, j: (layer, 0, 0)),
                  pl.BlockSpec((None, 1, LANES), lambda i, j: (layer, 0, 0))],
        out_specs=pl.BlockSpec((None, ts, nh * LANES), lambda i, j: (i, j, 0)),
        scratch_shapes=[pltpu.VMEM((3 * nh, ts + 8, LANES), F32),
                        pltpu.VMEM((nh, ts, LANES), F32),
                        pltpu.VMEM((nh, ts, LANES), F32),
                        pltpu.VMEM((nh, ts, LANES), F32),
                        pltpu.VMEM((ts, LANES), F32),
                        pltpu.VMEM((ts, LANES), F32),
                        pltpu.VMEM((nh, ts, HEAD_DIM), F32),
                        pltpu.VMEM((nh, ts, HEAD_DIM), BF16),
                        pltpu.VMEM((nh, ts, HEAD_DIM), BF16),
                        pltpu.VMEM((nh, ts, HEAD_DIM), BF16),
                        pltpu.VMEM((nh, ts, GDN_CHUNK), BF16),
                        pltpu.VMEM((nh, HEAD_DIM, HEAD_DIM), F32)],
        compiler_params=_cparams(("parallel", "arbitrary")),
        name="gdn",
    )(p4, p4, conv_w, ab, norm_g)


def _nsa_prep_kernel(kc2_ref, vc2_ref, ksl_ref, vsl_ref, kwn_ref, vwn_ref,
                     w1_ref, w2_ref, pos_ref, kg_ref, bm_ref, one_ref,
                     kc_o, vc_o, ks_o, vs_o, kw_o, vw_o):
    nc = kc_o.shape[0]
    half = CMP_STRIDE * LANES
    kg = kg_ref[...]
    for kind, (src, dst) in enumerate(((kc2_ref, kc_o), (vc2_ref, vc_o))):
        t2 = jnp.concatenate([src[pl.ds(l, nc, stride=CMP_STRIDE), :].astype(BF16)
                              for l in range(CMP_STRIDE)], axis=1)
        first = _dot(t2, w1_ref[kind, 0:half, :])
        second = _dot(t2, w1_ref[kind, half:2 * half, :])
        const = _dot(pos_ref[kind], w1_ref[kind])[0:1, :]
        pre = first + pltpu.roll(second, nc - 1, 0) + const
        y = _dot(jax.nn.gelu(pre).astype(BF16), w2_ref[kind])
        if kind == 0:
            y = _rms(y, kg)
        dst[...] = y.astype(BF16)
    front = ks_o.shape[0] - ksl_ref.shape[0]
    for dst, val in ((ks_o, _rms(ksl_ref[...], kg)), (kw_o, _rms(kwn_ref[...], kg)),
                     (vs_o, vsl_ref[...]), (vw_o, vwn_ref[...])):
        dst[0:front, 0:LANES] = jnp.zeros((front, LANES), BF16)
        dst[front:, 0:LANES] = val.astype(BF16)
    ks_o[:, LANES:] = bm_ref[...]
    vs_o[:, LANES:] = one_ref[...]
    vw_o[:, LANES:] = one_ref[...]


def _nsa_prep(p4, w1, w2, pos, k_norm, block_mask, ones_col, layer):
    ng, b, s, _ = p4.shape
    padded = s + WINDOW
    nc = s // CMP_STRIDE
    half = CMP_STRIDE * LANES
    grp = lambda base: (lambda i, g: (base + g, i, 0, 0))
    small = jax.ShapeDtypeStruct((b, KV_HEADS, nc, LANES), BF16)
    full = jax.ShapeDtypeStruct((b, KV_HEADS, padded, LANES), BF16)
    wide = jax.ShapeDtypeStruct((b, KV_HEADS, padded, 2 * LANES), BF16)
    out_small = pl.BlockSpec((None, None, nc, LANES), lambda i, g: (i, g, 0, 0))
    out_full = pl.BlockSpec((None, None, padded, LANES), lambda i, g: (i, g, 0, 0))
    out_wide = pl.BlockSpec((None, None, padded, 2 * LANES), lambda i, g: (i, g, 0, 0))
    const = pl.BlockSpec((padded, LANES), lambda i, g: (0, 0))
    return pl.pallas_call(
        _nsa_prep_kernel,
        out_shape=(small, small, wide, wide, full, wide),
        grid=(b, KV_HEADS),
        in_specs=[pl.BlockSpec((None, None, s, LANES), grp(G_KCMP)),
                  pl.BlockSpec((None, None, s, LANES), grp(G_VCMP)),
                  pl.BlockSpec((None, None, s, LANES), grp(G_KSLC)),
                  pl.BlockSpec((None, None, s, LANES), grp(G_VSLC)),
                  pl.BlockSpec((None, None, s, LANES), grp(G_KWIN)),
                  pl.BlockSpec((None, None, s, LANES), grp(G_VWIN)),
                  pl.BlockSpec((None, 2, 2 * half, LANES), lambda i, g: (layer, 0, 0, 0)),
                  pl.BlockSpec((None, 2, LANES, LANES), lambda i, g: (layer, 0, 0, 0)),
                  pl.BlockSpec((None, 2, 8, 2 * half), lambda i, g: (layer, 0, 0, 0)),
                  pl.BlockSpec((None, 1, LANES), lambda i, g: (layer, 0, 0)),
                  const, const],
        out_specs=(out_small, out_small, out_wide, out_wide, out_full, out_wide),
        compiler_params=_cparams(("parallel", "parallel")),
        name="nsa_prep",
    )(p4, p4, p4, p4, p4, p4, w1, w2, pos, k_norm, block_mask, ones_col)


def _nsa_kernel(q_ref, gate_ref, kc_ref, vc_ref, ks_ref, vs_ref, kw_ref, vw_ref,
                bc_ref, bs_ref, bw_ref, ov_ref, qg_ref, o_ref, sc_ref, *, n_cmp, n_sel):
    tq = q_ref.shape[1]
    r3 = Q_PER_KV
    i = pl.program_id(2)
    nc = kc_ref.shape[0]
    rows = r3 * tq

    qg = qg_ref[...]
    qf = jnp.concatenate([_rms(q_ref[r], qg) * (HEAD_DIM ** -0.5) for r in range(r3)], axis=0)
    qb = qf.astype(BF16)

    qq = lax.broadcasted_iota(jnp.int32, (tq, tq), 0)
    kk = lax.broadcasted_iota(jnp.int32, (tq, tq), 1)

    s_c = _dot_nt(qb, kc_ref[...]).reshape(r3, tq, nc) + bc_ref[...]
    qpos_c = i * tq + lax.broadcasted_iota(jnp.int32, (tq, nc), 0)
    n_idx = lax.broadcasted_iota(jnp.int32, (tq, nc), 1)
    mask_c = ((qpos_c >= n_idx * CMP_STRIDE + (CMP_LEN - 1)) & (n_idx < n_cmp))[None]
    s_c = jnp.where(mask_c, s_c, NEG_BIG)
    m_c = jnp.max(s_c, axis=-1, keepdims=True)
    p_c = jnp.where(mask_c, jnp.exp(s_c - m_c), 0.0)
    p_c = p_c / jnp.maximum(jnp.sum(p_c, axis=-1, keepdims=True), 1e-30)
    o_c = _dot(p_c.reshape(rows, nc).astype(BF16), vc_ref[...])
    p_sum = p_c[0] + p_c[1] + p_c[2]
    p_hi = p_sum.astype(BF16)
    p_lo = (p_sum - p_hi.astype(F32)).astype(BF16)
    importance = _dot(p_hi, ov_ref[...]) + _dot(p_lo, ov_ref[...])

    cur = (i * tq + qq) // SEL_LEN
    forced = (kk == 0) | (kk == cur) | (kk == cur - 1)
    score = jnp.where(forced, SEL_FORCE, jnp.where(kk <= cur, importance, NEG_BIG))
    st = score.T

    def selection_queries(n_blocks):
        sub = 8
        n_blocks = min(n_blocks, n_sel)
        slabs = [st[v * sub:(v + 1) * sub, :] for v in range((n_blocks + sub - 1) // sub)]
        keep = [slab > 0.5 * NEG_BIG for slab in slabs]
        if n_blocks > SEL_TOPK:
            blk = lax.broadcasted_iota(jnp.int32, (sub, tq), 0)
            ranks = [jnp.zeros((sub, tq), F32) for _ in slabs]
            for b in range(n_blocks):
                row = st[b:b + 1, :]
                for v, slab in enumerate(slabs):
                    if v * sub > b:
                        beats = row >= slab
                    elif (v + 1) * sub - 1 <= b:
                        beats = row > slab
                    else:
                        beats = (row > slab) | ((row == slab) & (blk + v * sub > b))
                    ranks[v] = ranks[v] + jnp.where(beats, 1.0, 0.0)
            keep = [k & (r < float(SEL_TOPK)) for k, r in zip(keep, ranks)]
        drop_t = jnp.concatenate([jnp.where(k, 0.0, 1.0) for k in keep]
                                 + [jnp.ones((tq - sub * len(slabs), tq), F32)], axis=0)
        drop = drop_t.T.astype(BF16)
        return jnp.concatenate([qb, jnp.concatenate([drop] * r3, axis=0)], axis=1)

    def normalise(acc):
        den = jnp.maximum(acc[:, HEAD_DIM:HEAD_DIM + 1], 1e-30)
        return (acc[:, 0:HEAD_DIM] / den).reshape(r3, tq, HEAD_DIM)

    band = (N_WIN_PREV + 1) * tq
    front = N_WIN_PREV * tq
    far_keys = sc_ref.shape[-1]
    boff = pl.multiple_of(i * tq, tq)
    q_b = lax.broadcasted_iota(jnp.int32, (tq, band), 0)
    k_b = lax.broadcasted_iota(jnp.int32, (tq, band), 1)

    def lane_fold(x, op):
        out = x[..., 0:tq]
        for c in range(1, x.shape[-1] // tq):
            out = op(out, x[..., c * tq:(c + 1) * tq])
        return out

    n_far = jnp.maximum(i - 1, 0) // (far_keys // tq)

    def selected(nf):
        last_tile = (nf + 1) * (far_keys // tq)
        q_sel = selection_queries((last_tile + 1) * tq // SEL_LEN)
        first_near = front - (i * tq - nf * far_keys)
        ok_s = (k_b <= q_b + front) & (k_b >= first_near)
        def group(pieces, with_band):
            keys_of = lambda c, lo, hi: slice(front + c * far_keys + lo, front + c * far_keys + hi)
            mx = None
            if with_band:
                s_b = (_dot_nt(q_sel, ks_ref[pl.ds(boff, band), :]).reshape(r3, tq, band) + bs_ref[...]
                       + jnp.where(ok_s, 0.0, NEG_BIG)[None])
                mx = lane_fold(s_b, jnp.maximum)
            for c, lo, hi in pieces:
                s = _dot_nt(q_sel, ks_ref[keys_of(c, lo, hi), :]).reshape(r3, tq, hi - lo)
                sc_ref[c, :, :, lo:hi] = s
                fold = lane_fold(s, jnp.maximum)
                mx = fold if mx is None else jnp.maximum(mx, fold)
            m = jnp.max(mx, axis=-1, keepdims=True)
            acc = None
            if with_band:
                acc = _dot(jnp.exp(s_b - m).reshape(rows, band).astype(BF16), vs_ref[pl.ds(boff, band), :])
            for c, lo, hi in pieces:
                p = jnp.exp(sc_ref[c, :, :, lo:hi] - m).reshape(rows, hi - lo).astype(BF16)
                part = _dot(p, vs_ref[keys_of(c, lo, hi), :])
                acc = part if acc is None else acc + part
            return m.reshape(rows, 1), acc

        if nf == 0:
            return normalise(group([], True)[1])
        whole = lambda cs: [(c, 0, far_keys) for c in cs]
        if nf % 2:
            first, second = whole(range((nf + 1) // 2)), whole(range((nf + 1) // 2, nf))
        else:
            mid = nf // 2
            first = whole(range(mid)) + [(mid, 0, far_keys // 2)]
            second = [(mid, far_keys // 2, far_keys)] + whole(range(mid + 1, nf))
        m_b, acc_b = group(second, True)
        m_a, acc_a = group(first, False)
        m_s = jnp.maximum(m_a, m_b)
        return normalise(acc_a * jnp.exp(m_a - m_s) + acc_b * jnp.exp(m_b - m_s))

    ok_w = (k_b > q_b) & (k_b <= q_b + front) & (k_b >= front - i * tq)
    s_w = (_dot_nt(qb, kw_ref[pl.ds(boff, band), :]).reshape(r3, tq, band) + bw_ref[...]
           + jnp.where(ok_w, 0.0, NEG_BIG)[None])
    p_w = jnp.exp(s_w - jnp.max(s_w, axis=-1, keepdims=True))
    o_w = normalise(_dot(p_w.reshape(rows, band).astype(BF16), vw_ref[pl.ds(boff, band), :]))

    gates = jax.nn.sigmoid(gate_ref[...])
    o_c = o_c.reshape(r3, tq, HEAD_DIM)
    col = lambda branch, r: gates[:, LANE_GATE + branch * r3 + r:LANE_GATE + branch * r3 + r + 1]
    partial = [col(0, r) * o_c[r] + col(2, r) * o_w[r] for r in range(r3)]
    gate_s = [col(1, r) for r in range(r3)]

    for nf in range(sc_ref.shape[0]):
        @pl.when(n_far == nf)
        def _(nf=nf):
            o_s = selected(nf)
            for r in range(r3):
                o_ref[:, r * HEAD_DIM:(r + 1) * HEAD_DIM] = (partial[r] + gate_s[r] * o_s[r]).astype(BF16)


def _nsa(p4, prep, bias_c, bias_sel, bias_win, overlap, q_norm, layer, tq=WIN_BLOCK, far_keys=512):
    _, b, s, _ = p4.shape
    kc, vc, ks, vs, kw, vw = prep
    nc = kc.shape[2]
    r3 = Q_PER_KV
    n_sel = s // SEL_LEN
    n_cmp = (s - CMP_LEN) // CMP_STRIDE + 1
    band = (N_WIN_PREV + 1) * tq
    padded = ks.shape[2]
    kv_small = pl.BlockSpec((None, None, nc, LANES), lambda i, g, j: (i, g, 0, 0))
    kv_full = pl.BlockSpec((None, None, padded, LANES), lambda i, g, j: (i, g, 0, 0))
    kv_wide = pl.BlockSpec((None, None, padded, 2 * LANES), lambda i, g, j: (i, g, 0, 0))
    return pl.pallas_call(
        functools.partial(_nsa_kernel, n_cmp=n_cmp, n_sel=n_sel),
        out_shape=jax.ShapeDtypeStruct((b, s, N_HEADS_C * HEAD_DIM), BF16),
        grid=(b, KV_HEADS, s // tq),
        in_specs=[pl.BlockSpec((r3, None, tq, LANES), lambda i, g, j: (G_QC // r3 + g, i, j, 0)),
                  pl.BlockSpec((None, None, tq, LANES), lambda i, g, j: (G_SMALL + g, i, j, 0)),
                  kv_small, kv_small, kv_wide, kv_wide, kv_full, kv_wide,
                  pl.BlockSpec((r3, tq, nc), lambda i, g, j: (g, j, 0)),
                  pl.BlockSpec((r3, tq, band), lambda i, g, j: (g, 0, 0)),
                  pl.BlockSpec((r3, tq, band), lambda i, g, j: (g, 0, 0)),
                  pl.BlockSpec((nc, LANES), lambda i, g, j: (0, 0)),
                  pl.BlockSpec((None, 1, LANES), lambda i, g, j: (layer, 0, 0))],
        out_specs=pl.BlockSpec((None, tq, r3 * HEAD_DIM), lambda i, g, j: (i, j, g)),
        scratch_shapes=[pltpu.VMEM((s // far_keys, r3, tq, far_keys), F32)],
        compiler_params=_cparams(("parallel", "parallel", "arbitrary")),
        name="nsa",
    )(p4, p4, kc, vc, ks, vs, kw, vw, bias_c, bias_sel, bias_win, overlap, q_norm)


def _interleave(fragments):
    live = list(fragments)
    while live:
        for fragment in list(live):
            try:
                next(fragment)
            except StopIteration:
                live.remove(fragment)


def _nsa_pair_kernel(q_ref, gate_ref, kc_ref, vc_ref, ks_ref, vs_ref, kw_ref, vw_ref,
                     bc_ref, bs_ref, bw_ref, ov_ref, qg_ref, o_ref, sc_ref, *, n_cmp, n_sel):
    tq = q_ref.shape[1]
    r3 = Q_PER_KV
    i = pl.program_id(1)
    nc = kc_ref.shape[1]
    rows = r3 * tq
    band = (N_WIN_PREV + 1) * tq
    front = N_WIN_PREV * tq
    far_keys = sc_ref.shape[-1]
    boff = pl.multiple_of(i * tq, tq)
    n_far = jnp.maximum(i - 1, 0) // (far_keys // tq)

    qq = lax.broadcasted_iota(jnp.int32, (tq, tq), 0)
    kk = lax.broadcasted_iota(jnp.int32, (tq, tq), 1)
    q_b = lax.broadcasted_iota(jnp.int32, (tq, band), 0)
    k_b = lax.broadcasted_iota(jnp.int32, (tq, band), 1)
    qpos_c = i * tq + lax.broadcasted_iota(jnp.int32, (tq, nc), 0)
    n_idx = lax.broadcasted_iota(jnp.int32, (tq, nc), 1)
    mask_c = ((qpos_c >= n_idx * CMP_STRIDE + (CMP_LEN - 1)) & (n_idx < n_cmp))[None]
    ok_w = (k_b > q_b) & (k_b <= q_b + front) & (k_b >= front - i * tq)
    cur = (i * tq + qq) // SEL_LEN
    forced = (kk == 0) | (kk == cur) | (kk == cur - 1)
    qg = qg_ref[...]

    def lane_fold(x, op):
        out = x[..., 0:tq]
        for c in range(1, x.shape[-1] // tq):
            out = op(out, x[..., c * tq:(c + 1) * tq])
        return out

    def normalise(acc):
        den = jnp.maximum(acc[:, HEAD_DIM:HEAD_DIM + 1], 1e-30)
        return (acc[:, 0:HEAD_DIM] / den).reshape(r3, tq, HEAD_DIM)

    kept = [{} for _ in range(KV_HEADS)]

    def before_selection(g):
        heads = slice(g * r3, (g + 1) * r3)
        qf = jnp.concatenate([_rms(q_ref[g * r3 + r], qg) * (HEAD_DIM ** -0.5) for r in range(r3)], axis=0)
        qb = qf.astype(BF16)
        yield
        s_c = _dot_nt(qb, kc_ref[g]).reshape(r3, tq, nc) + bc_ref[heads]
        s_w = (_dot_nt(qb, kw_ref[g, pl.ds(boff, band), :]).reshape(r3, tq, band) + bw_ref[heads]
               + jnp.where(ok_w, 0.0, NEG_BIG)[None])
        yield
        s_c = jnp.where(mask_c, s_c, NEG_BIG)
        m_c = jnp.max(s_c, axis=-1, keepdims=True)
        p_c = jnp.where(mask_c, jnp.exp(s_c - m_c), 0.0)
        p_c = p_c / jnp.maximum(jnp.sum(p_c, axis=-1, keepdims=True), 1e-30)
        yield
        o_c = _dot(p_c.reshape(rows, nc).astype(BF16), vc_ref[g]).reshape(r3, tq, HEAD_DIM)
        p_sum = p_c[0] + p_c[1] + p_c[2]
        p_hi = p_sum.astype(BF16)
        p_lo = (p_sum - p_hi.astype(F32)).astype(BF16)
        importance = _dot(p_hi, ov_ref[...]) + _dot(p_lo, ov_ref[...])
        yield
        p_w = jnp.exp(s_w - jnp.max(s_w, axis=-1, keepdims=True))
        o_w = normalise(_dot(p_w.reshape(rows, band).astype(BF16), vw_ref[g, pl.ds(boff, band), :]))
        yield
        score = jnp.where(forced, SEL_FORCE, jnp.where(kk <= cur, importance, NEG_BIG))
        n_rows = ((n_sel + 7) // 8) * 8
        st = score.T[0:n_rows, :]
        sub = 8
        slabs = [st[v * sub:(v + 1) * sub, :] for v in range(n_rows // sub)]
        blk = lax.broadcasted_iota(jnp.int32, (sub, tq), 0)
        ranks = [jnp.zeros((sub, tq), F32) for _ in slabs]
        for b in range(n_sel):
            row = st[b:b + 1, :]
            for v, slab in enumerate(slabs):
                if v * sub > b:
                    beats = row >= slab
                elif (v + 1) * sub - 1 <= b:
                    beats = row > slab
                else:
                    beats = (row > slab) | ((row == slab) & (blk + v * sub > b))
                ranks[v] = ranks[v] + jnp.where(beats, 1.0, 0.0)
            if b % 8 == 7:
                yield
        rank = jnp.concatenate(ranks, axis=0)
        drop_t = jnp.where((rank < float(min(SEL_TOPK, n_sel))) & (st > 0.5 * NEG_BIG), 0.0, 1.0)
        if n_rows < tq:
            drop_t = jnp.concatenate([drop_t, jnp.zeros((tq - n_rows, tq), F32)], axis=0)
        drop = drop_t.T.astype(BF16)
        q_sel = jnp.concatenate([qb, jnp.concatenate([drop] * r3, axis=0)], axis=1)
        gates = jax.nn.sigmoid(gate_ref[g])
        col = lambda branch, r: gates[:, LANE_GATE + branch * r3 + r:LANE_GATE + branch * r3 + r + 1]
        kept[g].update(q_sel=q_sel, gate_s=[col(1, r) for r in range(r3)],
                       partial=[col(0, r) * o_c[r] + col(2, r) * o_w[r] for r in range(r3)])

    _interleave(before_selection(g) for g in range(KV_HEADS))

    def half_softmax(g, pieces, with_band, nf, result):
        q_sel = kept[g]["q_sel"]
        keys_of = lambda c, lo, hi: slice(front + c * far_keys + lo, front + c * far_keys + hi)
        mx = None
        if with_band:
            first_near = front - (i * tq - nf * far_keys)
            ok_s = (k_b <= q_b + front) & (k_b >= first_near)
            s_b = (_dot_nt(q_sel, ks_ref[g, pl.ds(boff, band), :]).reshape(r3, tq, band)
                   + bs_ref[g * r3:(g + 1) * r3] + jnp.where(ok_s, 0.0, NEG_BIG)[None])
            mx = lane_fold(s_b, jnp.maximum)
            yield
        for c, lo, hi in pieces:
            s = _dot_nt(q_sel, ks_ref[g, keys_of(c, lo, hi), :]).reshape(r3, tq, hi - lo)
            sc_ref[g, c, :, :, lo:hi] = s
            fold = lane_fold(s, jnp.maximum)
            mx = fold if mx is None else jnp.maximum(mx, fold)
            yield
        m = jnp.max(mx, axis=-1, keepdims=True)
        acc = None
        if with_band:
            acc = _dot(jnp.exp(s_b - m).reshape(rows, band).astype(BF16), vs_ref[g, pl.ds(boff, band), :])
            yield
        for c, lo, hi in pieces:
            p = jnp.exp(sc_ref[g, c, :, :, lo:hi] - m).reshape(rows, hi - lo).astype(BF16)
            part = _dot(p, vs_ref[g, keys_of(c, lo, hi), :])
            acc = part if acc is None else acc + part
            yield
        result.append((m.reshape(rows, 1), acc))

    def selected(nf):
        whole = lambda cs: [(c, 0, far_keys) for c in cs]
        if nf == 0:
            first, second = None, []
        elif nf % 2:
            first, second = whole(range((nf + 1) // 2)), whole(range((nf + 1) // 2, nf))
        else:
            mid = nf // 2
            first = whole(range(mid)) + [(mid, 0, far_keys // 2)]
            second = [(mid, far_keys // 2, far_keys)] + whole(range(mid + 1, nf))
        halves = [([], []) for _ in range(KV_HEADS)]
        fragments = []
        for g in range(KV_HEADS):
            fragments.append(half_softmax(g, second, True, nf, halves[g][1]))
            if first is not None:
                fragments.append(half_softmax(g, first, False, nf, halves[g][0]))
        _interleave(fragments)
        for g in range(KV_HEADS):
            (m_b, acc_b), = halves[g][1]
            if first is None:
                o_s = normalise(acc_b)
            else:
                (m_a, acc_a), = halves[g][0]
                m_s = jnp.maximum(m_a, m_b)
                o_s = normalise(acc_a * jnp.exp(m_a - m_s) + acc_b * jnp.exp(m_b - m_s))
            for r in range(r3):
                out = kept[g]["partial"][r] + kept[g]["gate_s"][r] * o_s[r]
                o_ref[:, (g * r3 + r) * HEAD_DIM:(g * r3 + r + 1) * HEAD_DIM] = out.astype(BF16)

    for nf in range(sc_ref.shape[1]):
        pl.when(n_far == nf)(functools.partial(selected, nf))


def _nsa_pair(p4, prep, bias_c, bias_sel, bias_win, overlap, q_norm, layer, tq=WIN_BLOCK, far_keys=512):
    _, b, s, _ = p4.shape
    kc, vc, ks, vs, kw, vw = prep
    nc = kc.shape[2]
    nq = N_HEADS_C
    n_sel = s // SEL_LEN
    n_cmp = (s - CMP_LEN) // CMP_STRIDE + 1
    band = (N_WIN_PREV + 1) * tq
    padded = ks.shape[2]
    per_batch = lambda *tail: pl.BlockSpec((None, KV_HEADS) + tail, lambda i, j: (i, 0, 0, 0))
    resident = lambda *tail: pl.BlockSpec((None, KV_HEADS) + tail, lambda i, j: (i, 0, 0, 0),
                                          pipeline_mode=pl.Buffered(1))
    fixed = lambda *shape: pl.BlockSpec(shape, lambda i, j: (0,) * len(shape), pipeline_mode=pl.Buffered(1))
    return pl.pallas_call(
        functools.partial(_nsa_pair_kernel, n_cmp=n_cmp, n_sel=n_sel),
        out_shape=jax.ShapeDtypeStruct((b, s, nq * HEAD_DIM), BF16),
        grid=(b, s // tq),
        in_specs=[pl.BlockSpec((nq, None, tq, LANES), lambda i, j: (G_QC // nq, i, j, 0)),
                  pl.BlockSpec((KV_HEADS, None, tq, LANES), lambda i, j: (G_SMALL // KV_HEADS, i, j, 0)),
                  per_batch(nc, LANES), per_batch(nc, LANES),
                  resident(padded, 2 * LANES), resident(padded, 2 * LANES),
                  resident(padded, LANES), resident(padded, 2 * LANES),
                  pl.BlockSpec((nq, tq, nc), lambda i, j: (0, j, 0)),
                  fixed(nq, tq, band), fixed(nq, tq, band), fixed(nc, LANES),
                  pl.BlockSpec((None, 1, LANES), lambda i, j: (layer, 0, 0))],
        out_specs=pl.BlockSpec((None, tq, nq * HEAD_DIM), lambda i, j: (i, j, 0)),
        scratch_shapes=[pltpu.VMEM((KV_HEADS, s // far_keys, Q_PER_KV, tq, far_keys), F32)],
        compiler_params=_cparams(("parallel", "arbitrary")),
        name="nsa",
    )(p4, p4, kc, vc, ks, vs, kw, vw, bias_c, bias_sel, bias_win, overlap, q_norm)


def _t5_bucket(dist):
    n = jnp.maximum(dist, 0)
    max_exact = RPB_BUCKETS // 2
    log_ratio = jnp.log(jnp.maximum(n, 1).astype(F32) / max_exact) / math.log(RPB_MAX_DIST / max_exact)
    large = jnp.minimum(max_exact + (log_ratio * (RPB_BUCKETS - max_exact)).astype(jnp.int32), RPB_BUCKETS - 1)
    return jnp.where(n < max_exact, n, large)


def _bias_tables(rel_bias, s, tq):
    table = rel_bias.astype(F32)
    buckets = jnp.arange(RPB_BUCKETS, dtype=jnp.int32)[:, None, None]

    def look(dist):
        onehot = (_t5_bucket(dist)[None] == buckets).astype(F32)
        return jnp.einsum("nh,nqk->hqk", table, onehot, precision=lax.Precision.HIGHEST)

    nc = s // CMP_STRIDE
    pos = jnp.arange(s, dtype=jnp.int32)
    cmp_end = jnp.arange(nc, dtype=jnp.int32) * CMP_STRIDE + (CMP_LEN - 1)
    bias_c = look(pos[:, None] - cmp_end[None, :])
    q = jnp.arange(tq, dtype=jnp.int32)[:, None]
    kb = jnp.arange((N_WIN_PREV + 1) * tq, dtype=jnp.int32)[None, :]
    bias_win = look(N_WIN_PREV * tq + q - kb)
    bias_sel = bias_win - table[RPB_BUCKETS - 1][:, None, None]
    return bias_c, bias_sel, bias_win


def _selection_constants(s, tq):
    nc = s // CMP_STRIDE
    n_cmp = (s - CMP_LEN) // CMP_STRIDE + 1
    n_sel = s // SEL_LEN
    cmp_start = np.arange(nc) * CMP_STRIDE
    sel_start = np.arange(LANES) * SEL_LEN
    overlap = ((cmp_start[:, None] < sel_start[None, :] + SEL_LEN)
               & (cmp_start[:, None] + CMP_LEN > sel_start[None, :])
               & (np.arange(nc)[:, None] < n_cmp) & (np.arange(LANES)[None, :] < n_sel))
    key_pos = np.arange(-WINDOW, s)[:, None]
    in_block = (np.arange(LANES)[None, :] == (key_pos // SEL_LEN)) & (key_pos >= 0)
    block_mask = jnp.where(jnp.asarray(in_block), NEG_BIG, 0.0).astype(BF16)
    ones_col = jnp.asarray((np.arange(LANES)[None, :] == 0) & (key_pos >= 0), BF16)
    return jnp.asarray(overlap, BF16), block_mask, ones_col


def _arrange_w_in(w_in):
    depth, d, _ = w_in.shape
    w_in = w_in.astype(BF16)
    da, db, dc, dkv = N_HEADS_A * 128, N_HEADS_B * 128, N_HEADS_C * 128, KV_HEADS * 128
    o_ba = 4 * da
    o_aa = o_ba + N_HEADS_A
    o_ub = o_aa + N_HEADS_A
    o_qc = o_ub + 2 * db
    o_kc = o_qc + dc
    o_gc = o_kc + 6 * dkv
    sl = lambda lo, n: w_in[:, :, lo:lo + n]
    gate = w_in[:, :, o_gc:o_gc + 3 * N_HEADS_C].reshape(depth, d, 3, KV_HEADS, Q_PER_KV)
    zeros = lambda n: jnp.zeros((depth, d, n), w_in.dtype)
    small0 = jnp.concatenate([sl(o_ba, 2 * N_HEADS_A), gate[:, :, :, 0, :].reshape(depth, d, 3 * Q_PER_KV),
                              zeros(LANES - 2 * N_HEADS_A - 3 * Q_PER_KV)], axis=-1)
    small1 = jnp.concatenate([zeros(LANE_GATE), gate[:, :, :, 1, :].reshape(depth, d, 3 * Q_PER_KV),
                              zeros(LANES - LANE_GATE - 3 * Q_PER_KV)], axis=-1)
    parts = [sl(0, 4 * da),
             sl(o_qc, dc),
             sl(o_kc, dkv),
             sl(o_ub, 2 * db),
             sl(o_kc + dkv, 5 * dkv),
             small0, small1]
    return jnp.concatenate(parts, axis=-1).astype(BF16)


def kernel(x, attn_norm, w_in, conv_a, a_log, dt_bias, gdn_norm, sgu_ln_g, sgu_ln_b, sgu_w, sgu_b,
           nsa_q_norm, nsa_k_norm, cmp_pos, cmp_w1, cmp_w2, rel_bias, w_out, mlp_norm, w_up, w_down):
    b, s, d = x.shape
    depth = w_in.shape[0]
    t = b * s
    tq = 128

    w_in_r = _arrange_w_in(w_in)
    w_out_b = w_out.astype(BF16)
    w_up_b = w_up.astype(BF16)
    w_down_b = w_down.astype(BF16)
    attn_g = attn_norm.reshape(depth, 1, d)
    mlp_g = mlp_norm.reshape(depth, 1, d)
    conv_r = conv_a.reshape(depth, CONV_WIDTH * 3 * N_HEADS_A, LANES)
    pad_to = lambda v, lo: jnp.pad(v, ((0, 0), (lo, LANES - lo - v.shape[1])))
    ab = jnp.stack([pad_to(a_log, LANE_DECAY), pad_to(dt_bias, LANE_DECAY)], axis=1)
    gdn_g = gdn_norm.reshape(depth, 1, LANES)
    ln_g = sgu_ln_g.reshape(depth, N_HEADS_B, LANES)
    ln_b = sgu_ln_b.reshape(depth, N_HEADS_B, LANES)
    sgu_bias = jnp.broadcast_to(sgu_b[..., None], sgu_b.shape + (LANES,))
    q_g = nsa_q_norm.reshape(depth, 1, LANES)
    k_g = nsa_k_norm.reshape(depth, 1, LANES)
    w1_b = cmp_w1.astype(BF16)
    w2_b = cmp_w2.astype(BF16)
    pos_b = jnp.broadcast_to(cmp_pos.reshape(depth, 2, 1, CMP_LEN * LANES),
                             (depth, 2, 8, CMP_LEN * LANES)).astype(BF16)
    bias_c, bias_sel, bias_win = _bias_tables(rel_bias, s, tq)
    overlap, block_mask, ones_col = _selection_constants(s, tq)

    x2 = x.reshape(t, d)
    for layer in range(depth):
        p = _inproj(x2, attn_g, w_in_r, layer)
        p4 = p.reshape(N_GROUPS, b, s, LANES)
        mix_a = _gdn(p4, conv_r, ab, gdn_g, layer)
        mix_b = _sgu(p4, ln_g, ln_b, sgu_w, sgu_bias, layer)
        prep = _nsa_prep(p4, w1_b, w2_b, pos_b, k_g, block_mask, ones_col, layer)
        mix_c = _nsa_pair(p4, prep, bias_c, bias_sel, bias_win, overlap, q_g, layer, tq=tq)
        x2 = _outproj(x2, mix_a.reshape(t, -1), mix_b.reshape(t, -1), mix_c.reshape(t, -1), w_out_b, layer)
        x2 = _mlp(x2, mlp_g, w_up_b, w_down_b, layer)
    return x2.reshape(b, s, d)
```

```python
import functools
import math

import numpy as np
import jax
import jax.numpy as jnp
from jax import lax
from jax.experimental import pallas as pl
from jax.experimental.pallas import tpu as pltpu

F32 = jnp.float32
BF16 = jnp.bfloat16

LANES = 128
HEAD_DIM = 128
N_HEADS_A = 6
N_HEADS_B = 4
N_HEADS_C = 6
KV_HEADS = 2
Q_PER_KV = N_HEADS_C // KV_HEADS
CONV_WIDTH = 4
GDN_CHUNK = 64
SGU_CHUNK = 128
CMP_LEN = 32
CMP_STRIDE = 16
SEL_LEN = 64
SEL_TOPK = 16
WINDOW = 512
WIN_BLOCK = 128
N_WIN_PREV = WINDOW // WIN_BLOCK
RPB_BUCKETS = 32
RPB_MAX_DIST = 128
NORM_EPS = 1e-6
NEG_BIG = -1e30
SEL_FORCE = 1e9
VMEM_LIMIT = 56 * 1024 * 1024

G_QA, G_KA, G_VA, G_ZA = 0, 6, 12, 18
G_QC = 24
G_KCMP = 30
G_UB, G_VB = 32, 36
G_VCMP = 40
G_KSLC, G_VSLC, G_KWIN, G_VWIN = 42, 44, 46, 48
G_SMALL = 50
N_GROUPS = 52
LANE_BETA, LANE_DECAY, LANE_GATE = 0, 6, 12


def _dot(a, b, precision=None):
    return jnp.dot(a, b, preferred_element_type=F32, precision=precision)


def _dot_nt(a, b):
    return lax.dot_general(a, b, (((1,), (1,)), ((), ())), preferred_element_type=F32)


def _dot_tn(a, b):
    return lax.dot_general(a, b, (((0,), (0,)), ((), ())), preferred_element_type=F32)


def _rms(x, gain):
    return x * lax.rsqrt(jnp.mean(x * x, axis=-1, keepdims=True) + NORM_EPS) * gain


def _cparams(sem):
    return pltpu.CompilerParams(dimension_semantics=sem, vmem_limit_bytes=VMEM_LIMIT)


def _inproj_kernel(x0_ref, xn_ref, g_ref, w_ref, o_ref, h_ref):
    i, j = pl.program_id(0), pl.program_id(1)
    part = xn_ref.shape[0] // pl.num_programs(1)

    @pl.when((i == 0) & (j == 0))
    def _():
        h_ref[0] = _rms(x0_ref[...], g_ref[...]).astype(BF16)

    cur = i % 2
    r = _dot(h_ref[cur], w_ref[...])
    for c in range(o_ref.shape[0]):
        o_ref[c] = r[:, c * LANES:(c + 1) * LANES]

    rows = pl.ds(pl.multiple_of(j * part, part), part)
    h_ref[1 - cur, rows, :] = _rms(xn_ref[rows, :], g_ref[...]).astype(BF16)


def _inproj(x2, gain, w, layer, tm=512, tn=13 * LANES):
    t, d = x2.shape
    n = w.shape[-1]
    last = t // tm - 1
    return pl.pallas_call(
        _inproj_kernel,
        out_shape=jax.ShapeDtypeStruct((n // LANES, t, LANES), F32),
        grid=(t // tm, n // tn),
        in_specs=[pl.BlockSpec((tm, d), lambda i, j: (0, 0), pipeline_mode=pl.Buffered(1)),
                  pl.BlockSpec((tm, d), lambda i, j: (jnp.minimum(i + 1, last), 0)),
                  pl.BlockSpec((None, 1, d), lambda i, j: (layer, 0, 0)),
                  pl.BlockSpec((None, d, tn), lambda i, j: (layer, 0, j))],
        out_specs=pl.BlockSpec((tn // LANES, tm, LANES), lambda i, j: (j, i, 0)),
        scratch_shapes=[pltpu.VMEM((2, tm, d), BF16)],
        compiler_params=_cparams(("arbitrary", "arbitrary")),
        name="inproj",
    )(x2, x2, gain, w)


def _outproj_kernel(x_ref, a_ref, b_ref, c_ref, w_ref, o_ref):
    da, db = a_ref.shape[1], b_ref.shape[1]
    acc = _dot(a_ref[...], w_ref[0:da, :])
    acc += _dot(b_ref[...], w_ref[da:da + db, :])
    acc += _dot(c_ref[...], w_ref[da + db:, :])
    o_ref[...] = x_ref[...] + acc


def _outproj(x2, ma, mb, mc, w_out, layer, tm=512):
    t, d = x2.shape
    da, db, dc = ma.shape[1], mb.shape[1], mc.shape[1]
    return pl.pallas_call(
        _outproj_kernel,
        out_shape=jax.ShapeDtypeStruct((t, d), F32),
        grid=(t // tm,),
        in_specs=[pl.BlockSpec((tm, d), lambda i: (i, 0)),
                  pl.BlockSpec((tm, da), lambda i: (i, 0)),
                  pl.BlockSpec((tm, db), lambda i: (i, 0)),
                  pl.BlockSpec((tm, dc), lambda i: (i, 0)),
                  pl.BlockSpec((None, da + db + dc, d), lambda i: (layer, 0, 0))],
        out_specs=pl.BlockSpec((tm, d), lambda i: (i, 0)),
        compiler_params=_cparams(("parallel",)),
        name="outproj",
    )(x2, ma, mb, mc, w_out)


def _mlp_kernel(x0_ref, xn_ref, g_ref, wu_ref, wd_ref, o_ref, h_ref, keep_ref):
    i, k = pl.program_id(0), pl.program_id(1)
    part = xn_ref.shape[0] // pl.num_programs(1)
    cur = i % 2

    @pl.when((i == 0) & (k == 0))
    def _():
        keep_ref[0] = x0_ref[...]
        h_ref[0] = _rms(x0_ref[...], g_ref[...]).astype(BF16)

    @pl.when(k == 0)
    def _():
        o_ref[...] = keep_ref[cur]

    a = jnp.maximum(_dot(h_ref[cur], wu_ref[...]), 0.0)
    o_ref[...] += _dot((a * a).astype(BF16), wd_ref[...])

    rows = pl.ds(pl.multiple_of(k * part, part), part)
    nxt = xn_ref[rows, :]
    keep_ref[1 - cur, rows, :] = nxt
    h_ref[1 - cur, rows, :] = _rms(nxt, g_ref[...]).astype(BF16)


def _mlp(x2, gain, w_up, w_down, layer, tm=512, tf=1024):
    t, d = x2.shape
    f = w_up.shape[-1]
    last = t // tm - 1
    return pl.pallas_call(
        _mlp_kernel,
        out_shape=jax.ShapeDtypeStruct((t, d), F32),
        grid=(t // tm, f // tf),
        in_specs=[pl.BlockSpec((tm, d), lambda i, k: (0, 0), pipeline_mode=pl.Buffered(1)),
                  pl.BlockSpec((tm, d), lambda i, k: (jnp.minimum(i + 1, last), 0)),
                  pl.BlockSpec((None, 1, d), lambda i, k: (layer, 0, 0)),
                  pl.BlockSpec((None, d, tf), lambda i, k: (layer, 0, k)),
                  pl.BlockSpec((None, tf, d), lambda i, k: (layer, k, 0))],
        out_specs=pl.BlockSpec((tm, d), lambda i, k: (i, 0)),
        scratch_shapes=[pltpu.VMEM((2, tm, d), BF16), pltpu.VMEM((2, tm, d), F32)],
        compiler_params=_cparams(("arbitrary", "arbitrary")),
        name="mlp",
    )(x2, x2, gain, w_up, w_down)


def _sgu_kernel(p_ref, lg_ref, lb_ref, w_ref, b_ref, o_ref):
    ts = p_ref.shape[1]
    nb = N_HEADS_B
    v = [jax.nn.gelu(p_ref[nb + g]) for g in range(nb)]
    width = float(nb * LANES)
    mu = sum(jnp.sum(vg, axis=-1, keepdims=True) for vg in v) / width
    var = sum(jnp.sum(jnp.square(vg - mu), axis=-1, keepdims=True) for vg in v) / width
    inv = lax.rsqrt(var + NORM_EPS)
    row = lax.broadcasted_iota(jnp.int32, (SGU_CHUNK, SGU_CHUNK), 0)
    col = lax.broadcasted_iota(jnp.int32, (SGU_CHUNK, SGU_CHUNK), 1)
    causal = col <= row
    for g in range(nb):
        vn = ((v[g] - mu) * inv * lg_ref[g:g + 1, :] + lb_ref[g:g + 1, :]).astype(BF16)
        wg = jnp.where(causal, w_ref[g], 0.0).astype(BF16)
        for c in range(ts // SGU_CHUNK):
            rows = slice(c * SGU_CHUNK, (c + 1) * SGU_CHUNK)
            mixed = _dot(wg, vn[rows]) + b_ref[g]
            u = jax.nn.gelu(p_ref[g, rows, :])
            o_ref[rows, g * LANES:(g + 1) * LANES] = (u * mixed).astype(BF16)


def _sgu(p4, ln_g, ln_b, w_s, b_s, layer, ts=512):
    _, b, s, _ = p4.shape
    nb = N_HEADS_B
    return pl.pallas_call(
        _sgu_kernel,
        out_shape=jax.ShapeDtypeStruct((b, s, nb * LANES), BF16),
        grid=(b, s // ts),
        in_specs=[pl.BlockSpec((2 * nb, None, ts, LANES), lambda i, j: (G_UB // (2 * nb), i, j, 0)),
                  pl.BlockSpec((None, nb, LANES), lambda i, j: (layer, 0, 0)),
                  pl.BlockSpec((None, nb, LANES), lambda i, j: (layer, 0, 0)),
                  pl.BlockSpec((None, nb, SGU_CHUNK, SGU_CHUNK), lambda i, j: (layer, 0, 0, 0)),
                  pl.BlockSpec((None, nb, SGU_CHUNK, LANES), lambda i, j: (layer, 0, 0, 0))],
        out_specs=pl.BlockSpec((None, ts, nb * LANES), lambda i, j: (i, j, 0)),
        compiler_params=_cparams(("parallel", "parallel")),
        name="sgu",
    )(p4, ln_g, ln_b, w_s, b_s)


def _gdn_kernel(a_ref, sm_ref, cw_ref, ab_ref, ng_ref, o_ref,
                xbuf, q_s, k_s, v_s, beta_s, gcum_s, u_s, w_s, qg_s, kd_s, a_s, st_ref, *, batch):
    ts = a_ref.shape[1]
    ck = GDN_CHUNK
    nh = N_HEADS_A
    pad = 8

    @pl.when(pl.program_id(1) == 0)
    def _():
        xbuf[:, 0:pad, :] = jnp.zeros((3 * nh, pad, LANES), F32)
        st_ref[...] = jnp.zeros_like(st_ref)

    xbuf[:, pad:pad + ts, :] = a_ref[0:3 * nh]
    dests = (q_s, k_s, v_s)
    for grp in range(3 * nh):
        acc = None
        for j in range(CONV_WIDTH):
            lo = pad - (CONV_WIDTH - 1) + j
            term = xbuf[grp, lo:lo + ts, :] * cw_ref[j * 3 * nh + grp:j * 3 * nh + grp + 1, :]
            acc = term if acc is None else acc + term
        y = acc * jax.nn.sigmoid(acc)
        kind, h = divmod(grp, nh)
        if kind < 2:
            y = y * lax.rsqrt(jnp.sum(y * y, axis=-1, keepdims=True) + NORM_EPS)
        dests[kind][h] = y
    xbuf[:, 0:pad, :] = xbuf[:, ts:ts + pad, :]

    sm = sm_ref[...]
    beta_s[...] = jax.nn.sigmoid(sm)
    z = sm + ab_ref[1:2, :]
    softplus = jnp.maximum(z, 0.0) + jnp.log1p(jnp.exp(-jnp.abs(z)))
    g_all = -jnp.exp(ab_ref[0:1, :]) * softplus

    ii = lax.broadcasted_iota(jnp.int32, (ck, ck), 0)
    jj = lax.broadcasted_iota(jnp.int32, (ck, ck), 1)
    incl = jj <= ii
    strict = jj < ii
    tri = jnp.where(incl, 1.0, 0.0).astype(F32)
    for c in range(ts // ck):
        rows = slice(c * ck, (c + 1) * ck)
        gcum_s[rows, :] = _dot(tri, g_all[rows], precision=lax.Precision.HIGHEST)

    scale = HEAD_DIM ** -0.5
    gain = ng_ref[...]
    bf = lambda m: m.astype(BF16)
    same = lambda size: (ii // size) == (jj // size)
    n_chunks = ts // ck
    rows_of = lambda c: slice(c * ck, (c + 1) * ck)
    dcol = lambda h: slice(LANE_DECAY + h, LANE_DECAY + h + 1)

    for c0 in range(0, n_chunks, batch):
        gc = {c: gcum_s[rows_of(c), :] for c in range(c0, c0 + batch)}
        gct = {c: gc[c].T for c in gc}
        items = [(c, h) for c in range(c0, c0 + batch) for h in range(nh)]
        kk, qk = {}, {}
        for c, h in items:
            g_col = gc[c][:, dcol(h)]
            q = q_s[h, rows_of(c), :] * scale
            k = k_s[h, rows_of(c), :]
            kbf = bf(k)
            kk[c, h] = _dot_nt(bf(k * beta_s[rows_of(c), LANE_BETA + h:LANE_BETA + h + 1]), kbf)
            qk[c, h] = _dot_nt(bf(q), kbf)
            qg_s[h, rows_of(c), :] = bf(q * jnp.exp(g_col))
            kd_s[h, rows_of(c), :] = bf(k * jnp.exp(gc[c][ck - 1:ck, dcol(h)] - g_col))
        lower, diag, p1 = {}, {}, {}
        for it in items:
            c, h = it
            decay = jnp.exp(jnp.where(incl, gc[c][:, dcol(h)] - gct[c][dcol(h), :], NEG_BIG))
            lower[it] = jnp.where(strict, kk[it] * decay, 0.0)
            a_s[h, rows_of(c), :] = bf(jnp.where(incl, qk[it] * decay, 0.0))
            diag[it] = jnp.where(same(8), lower[it], 0.0)
            p1[it] = _dot(bf(diag[it]), bf(diag[it]))
        p2 = {it: _dot(bf(p1[it]), bf(p1[it])) for it in items}
        dp1 = {it: _dot(bf(diag[it]), bf(p1[it])) for it in items}
        na = {it: p1[it] - diag[it] - dp1[it] for it in items}
        nap2 = {it: _dot(bf(na[it]), bf(p2[it])) for it in items}
        n = {it: na[it] + p2[it] + nap2[it] for it in items}
        size = 8
        while size < ck:
            off = same(2 * size) & ~same(size)
            cm = {it: jnp.where(off, lower[it], 0.0) for it in items}
            y = {it: cm[it] + _dot(bf(n[it]), bf(cm[it])) for it in items}
            yn = {it: _dot(bf(y[it]), bf(n[it])) for it in items}
            n = {it: n[it] - (y[it] + yn[it]) for it in items}
            size *= 2
        for it in items:
            c, h = it
            beta = beta_s[rows_of(c), LANE_BETA + h:LANE_BETA + h + 1]
            nb = bf(n[it])
            vb = v_s[h, rows_of(c), :] * beta
            kw = k_s[h, rows_of(c), :] * (beta * jnp.exp(gc[c][:, dcol(h)]))
            u_s[h, rows_of(c), :] = vb + _dot(nb, bf(vb))
            w_s[h, rows_of(c), :] = bf(kw + _dot(nb, bf(kw)))

    heads = range(nh)
    for c in range(n_chunks):
        rows = rows_of(c)
        state = [st_ref[h] for h in heads]
        sb = [bf(s) for s in state]
        ws = [_dot(w_s[h, rows, :], sb[h]) for h in heads]
        qs = [_dot(qg_s[h, rows, :], sb[h]) for h in heads]
        vnb = [bf(u_s[h, rows, :] - ws[h]) for h in heads]
        kv = [_dot_tn(kd_s[h, rows, :], vnb[h]) for h in heads]
        av = [_dot(a_s[h, rows, :], vnb[h]) for h in heads]
        for h in heads:
            g_last = gcum_s[(c + 1) * ck - 1:(c + 1) * ck, dcol(h)]
            st_ref[h] = state[h] * jnp.exp(g_last) + kv[h]
            zg = a_ref[3 * nh + h, rows, :]
            out = _rms(qs[h] + av[h], gain) * (zg * jax.nn.sigmoid(zg))
            o_ref[rows, h * LANES:(h + 1) * LANES] = out.astype(BF16)


def _gdn(p4, conv_w, ab, norm_g, layer, ts=512, batch=4):
    _, b, s, _ = p4.shape
    nh = N_HEADS_A
    return pl.pallas_call(
        functools.partial(_gdn_kernel, batch=batch),
        out_shape=jax.ShapeDtypeStruct((b, s, nh * LANES), BF16),
        grid=(b, s // ts),
        in_specs=[pl.BlockSpec((4 * nh, None, ts, LANES), lambda i, j: (0, i, j, 0)),
                  pl.BlockSpec((None, None, ts, LANES), lambda i, j: (G_SMALL, i, j, 0)),
                  pl.BlockSpec((None, CONV_WIDTH * 3 * nh, LANES), lambda i, j: (layer, 0, 0)),
                  pl.BlockSpec((None, 2, LANES), lambda i, j: (layer, 0, 0)),
                  pl.BlockSpec((None, 1, LANES), lambda i, j: (layer, 0, 0))],
        out_specs=pl.BlockSpec((None, ts, nh * LANES), lambda i, j: (i, j, 0)),
        scratch_shapes=[pltpu.VMEM((3 * nh, ts + 8, LANES), F32),
                        pltpu.VMEM((nh, ts, LANES), F32),
                        pltpu.VMEM((nh, ts, LANES), F32),
                        pltpu.VMEM((nh, ts, LANES), F32),
                        pltpu.VMEM((ts, LANES), F32),
                        pltpu.VMEM((ts, LANES), F32),
                        pltpu.VMEM((nh, ts, HEAD_DIM), F32),
                        pltpu.VMEM((nh, ts, HEAD_DIM), BF16),
                        pltpu.VMEM((nh, ts, HEAD_DIM), BF16),
                        pltpu.VMEM((nh, ts, HEAD_DIM), BF16),
                        pltpu.VMEM((nh, ts, GDN_CHUNK), BF16),
                        pltpu.VMEM((nh, HEAD_DIM, HEAD_DIM), F32)],
        compiler_params=_cparams(("parallel", "arbitrary")),
        name="gdn",
    )(p4, p4, conv_w, ab, norm_g)


def _nsa_prep_kernel(kc2_ref, vc2_ref, ksl_ref, vsl_ref, kwn_ref, vwn_ref,
                     w1_ref, w2_ref, pos_ref, kg_ref, bm_ref, one_ref,
                     kc_o, vc_o, ks_o, vs_o, kw_o, vw_o):
    nc = kc_o.shape[0]
    half = CMP_STRIDE * LANES
    kg = kg_ref[...]
    for kind, (src, dst) in enumerate(((kc2_ref, kc_o), (vc2_ref, vc_o))):
        t2 = jnp.concatenate([src[pl.ds(l, nc, stride=CMP_STRIDE), :].astype(BF16)
                              for l in range(CMP_STRIDE)], axis=1)
        first = _dot(t2, w1_ref[kind, 0:half, :])
        second = _dot(t2, w1_ref[kind, half:2 * half, :])
        const = _dot(pos_ref[kind], w1_ref[kind])[0:1, :]
        pre = first + pltpu.roll(second, nc - 1, 0) + const
        y = _dot(jax.nn.gelu(pre).astype(BF16), w2_ref[kind])
        if kind == 0:
            y = _rms(y, kg)
        dst[...] = y.astype(BF16)
    front = ks_o.shape[0] - ksl_ref.shape[0]
    for dst, val in ((ks_o, _rms(ksl_ref[...], kg)), (kw_o, _rms(kwn_ref[...], kg)),
                     (vs_o, vsl_ref[...]), (vw_o, vwn_ref[...])):
        dst[0:front, 0:LANES] = jnp.zeros((front, LANES), BF16)
        dst[front:, 0:LANES] = val.astype(BF16)
    ks_o[:, LANES:] = bm_ref[...]
    vs_o[:, LANES:] = one_ref[...]
    vw_o[:, LANES:] = one_ref[...]


def _nsa_prep(p4, w1, w2, pos, k_norm, block_mask, ones_col, layer):
    ng, b, s, _ = p4.shape
    padded = s + WINDOW
    nc = s // CMP_STRIDE
    half = CMP_STRIDE * LANES
    grp = lambda base: (lambda i, g: (base + g, i, 0, 0))
    small = jax.ShapeDtypeStruct((b, KV_HEADS, nc, LANES), BF16)
    full = jax.ShapeDtypeStruct((b, KV_HEADS, padded, LANES), BF16)
    wide = jax.ShapeDtypeStruct((b, KV_HEADS, padded, 2 * LANES), BF16)
    out_small = pl.BlockSpec((None, None, nc, LANES), lambda i, g: (i, g, 0, 0))
    out_full = pl.BlockSpec((None, None, padded, LANES), lambda i, g: (i, g, 0, 0))
    out_wide = pl.BlockSpec((None, None, padded, 2 * LANES), lambda i, g: (i, g, 0, 0))
    const = pl.BlockSpec((padded, LANES), lambda i, g: (0, 0))
    return pl.pallas_call(
        _nsa_prep_kernel,
        out_shape=(small, small, wide, wide, full, wide),
        grid=(b, KV_HEADS),
        in_specs=[pl.BlockSpec((None, None, s, LANES), grp(G_KCMP)),
                  pl.BlockSpec((None, None, s, LANES), grp(G_VCMP)),
                  pl.BlockSpec((None, None, s, LANES), grp(G_KSLC)),
                  pl.BlockSpec((None, None, s, LANES), grp(G_VSLC)),
                  pl.BlockSpec((None, None, s, LANES), grp(G_KWIN)),
                  pl.BlockSpec((None, None, s, LANES), grp(G_VWIN)),
                  pl.BlockSpec((None, 2, 2 * half, LANES), lambda i, g: (layer, 0, 0, 0)),
                  pl.BlockSpec((None, 2, LANES, LANES), lambda i, g: (layer, 0, 0, 0)),
                  pl.BlockSpec((None, 2, 8, 2 * half), lambda i, g: (layer, 0, 0, 0)),
                  pl.BlockSpec((None, 1, LANES), lambda i, g: (layer, 0, 0)),
                  const, const],
        out_specs=(out_small, out_small, out_wide, out_wide, out_full, out_wide),
        compiler_params=_cparams(("parallel", "parallel")),
        name="nsa_prep",
    )(p4, p4, p4, p4, p4, p4, w1, w2, pos, k_norm, block_mask, ones_col)


def _nsa_kernel(q_ref, gate_ref, kc_ref, vc_ref, ks_ref, vs_ref, kw_ref, vw_ref,
                bc_ref, bs_ref, bw_ref, ov_ref, qg_ref, o_ref, sc_ref, *, n_cmp, n_sel):
    tq = q_ref.shape[1]
    r3 = Q_PER_KV
    i = pl.program_id(2)
    nc = kc_ref.shape[0]
    rows = r3 * tq

    qg = qg_ref[...]
    qf = jnp.concatenate([_rms(q_ref[r], qg) * (HEAD_DIM ** -0.5) for r in range(r3)], axis=0)
    qb = qf.astype(BF16)

    qq = lax.broadcasted_iota(jnp.int32, (tq, tq), 0)
    kk = lax.broadcasted_iota(jnp.int32, (tq, tq), 1)

    s_c = _dot_nt(qb, kc_ref[...]).reshape(r3, tq, nc) + bc_ref[...]
    qpos_c = i * tq + lax.broadcasted_iota(jnp.int32, (tq, nc), 0)
    n_idx = lax.broadcasted_iota(jnp.int32, (tq, nc), 1)
    mask_c = ((qpos_c >= n_idx * CMP_STRIDE + (CMP_LEN - 1)) & (n_idx < n_cmp))[None]
    s_c = jnp.where(mask_c, s_c, NEG_BIG)
    m_c = jnp.max(s_c, axis=-1, keepdims=True)
    p_c = jnp.where(mask_c, jnp.exp(s_c - m_c), 0.0)
    p_c = p_c / jnp.maximum(jnp.sum(p_c, axis=-1, keepdims=True), 1e-30)
    o_c = _dot(p_c.reshape(rows, nc).astype(BF16), vc_ref[...])
    p_sum = p_c[0] + p_c[1] + p_c[2]
    p_hi = p_sum.astype(BF16)
    p_lo = (p_sum - p_hi.astype(F32)).astype(BF16)
    importance = _dot(p_hi, ov_ref[...]) + _dot(p_lo, ov_ref[...])

    cur = (i * tq + qq) // SEL_LEN
    forced = (kk == 0) | (kk == cur) | (kk == cur - 1)
    score = jnp.where(forced, SEL_FORCE, jnp.where(kk <= cur, importance, NEG_BIG))
    st = score.T

    def selection_queries(n_blocks):
        sub = 8
        n_blocks = min(n_blocks, n_sel)
        slabs = [st[v * sub:(v + 1) * sub, :] for v in range((n_blocks + sub - 1) // sub)]
        keep = [slab > 0.5 * NEG_BIG for slab in slabs]
        if n_blocks > SEL_TOPK:
            blk = lax.broadcasted_iota(jnp.int32, (sub, tq), 0)
            ranks = [jnp.zeros((sub, tq), F32) for _ in slabs]
            for b in range(n_blocks):
                row = st[b:b + 1, :]
                for v, slab in enumerate(slabs):
                    if v * sub > b:
                        beats = row >= slab
                    elif (v + 1) * sub - 1 <= b:
                        beats = row > slab
                    else:
                        beats = (row > slab) | ((row == slab) & (blk + v * sub > b))
                    ranks[v] = ranks[v] + jnp.where(beats, 1.0, 0.0)
            keep = [k & (r < float(SEL_TOPK)) for k, r in zip(keep, ranks)]
        drop_t = jnp.concatenate([jnp.where(k, 0.0, 1.0) for k in keep]
                                 + [jnp.ones((tq - sub * len(slabs), tq), F32)], axis=0)
        drop = drop_t.T.astype(BF16)
        return jnp.concatenate([qb, jnp.concatenate([drop] * r3, axis=0)], axis=1)

    def normalise(acc):
        den = jnp.maximum(acc[:, HEAD_DIM:HEAD_DIM + 1], 1e-30)
        return (acc[:, 0:HEAD_DIM] / den).reshape(r3, tq, HEAD_DIM)

    band = (N_WIN_PREV + 1) * tq
    front = N_WIN_PREV * tq
    far_keys = sc_ref.shape[-1]
    boff = pl.multiple_of(i * tq, tq)
    q_b = lax.broadcasted_iota(jnp.int32, (tq, band), 0)
    k_b = lax.broadcasted_iota(jnp.int32, (tq, band), 1)

    def lane_fold(x, op):
        out = x[..., 0:tq]
        for c in range(1, x.shape[-1] // tq):
            out = op(out, x[..., c * tq:(c + 1) * tq])
        return out

    n_far = jnp.maximum(i - 1, 0) // (far_keys // tq)

    def selected(nf):
        last_tile = (nf + 1) * (far_keys // tq)
        q_sel = selection_queries((last_tile + 1) * tq // SEL_LEN)
        first_near = front - (i * tq - nf * far_keys)
        ok_s = (k_b <= q_b + front) & (k_b >= first_near)
        def group(pieces, with_band):
            keys_of = lambda c, lo, hi: slice(front + c * far_keys + lo, front + c * far_keys + hi)
            mx = None
            if with_band:
                s_b = (_dot_nt(q_sel, ks_ref[pl.ds(boff, band), :]).reshape(r3, tq, band) + bs_ref[...]
                       + jnp.where(ok_s, 0.0, NEG_BIG)[None])
                mx = lane_fold(s_b, jnp.maximum)
            for c, lo, hi in pieces:
                s = _dot_nt(q_sel, ks_ref[keys_of(c, lo, hi), :]).reshape(r3, tq, hi - lo)
                sc_ref[c, :, :, lo:hi] = s
                fold = lane_fold(s, jnp.maximum)
                mx = fold if mx is None else jnp.maximum(mx, fold)
            m = jnp.max(mx, axis=-1, keepdims=True)
            acc = None
            if with_band:
                acc = _dot(jnp.exp(s_b - m).reshape(rows, band).astype(BF16), vs_ref[pl.ds(boff, band), :])
            for c, lo, hi in pieces:
                p = jnp.exp(sc_ref[c, :, :, lo:hi] - m).reshape(rows, hi - lo).astype(BF16)
                part = _dot(p, vs_ref[keys_of(c, lo, hi), :])
                acc = part if acc is None else acc + part
            return m.reshape(rows, 1), acc

        if nf == 0:
            return normalise(group([], True)[1])
        whole = lambda cs: [(c, 0, far_keys) for c in cs]
        if nf % 2:
            first, second = whole(range((nf + 1) // 2)), whole(range((nf + 1) // 2, nf))
        else:
            mid = nf // 2
            first = whole(range(mid)) + [(mid, 0, far_keys // 2)]
            second = [(mid, far_keys // 2, far_keys)] + whole(range(mid + 1, nf))
        m_b, acc_b = group(second, True)
        m_a, acc_a = group(first, False)
        m_s = jnp.maximum(m_a, m_b)
        return normalise(acc_a * jnp.exp(m_a - m_s) + acc_b * jnp.exp(m_b - m_s))

    ok_w = (k_b > q_b) & (k_b <= q_b + front) & (k_b >= front - i * tq)
    s_w = (_dot_nt(qb, kw_ref[pl.ds(boff, band), :]).reshape(r3, tq, band) + bw_ref[...]
           + jnp.where(ok_w, 0.0, NEG_BIG)[None])
    p_w = jnp.exp(s_w - jnp.max(s_w, axis=-1, keepdims=True))
    o_w = normalise(_dot(p_w.reshape(rows, band).astype(BF16), vw_ref[pl.ds(boff, band), :]))

    gates = jax.nn.sigmoid(gate_ref[...])
    o_c = o_c.reshape(r3, tq, HEAD_DIM)
    col = lambda branch, r: gates[:, LANE_GATE + branch * r3 + r:LANE_GATE + branch * r3 + r + 1]
    partial = [col(0, r) * o_c[r] + col(2, r) * o_w[r] for r in range(r3)]
    gate_s = [col(1, r) for r in range(r3)]

    for nf in range(sc_ref.shape[0]):
        @pl.when(n_far == nf)
        def _(nf=nf):
            o_s = selected(nf)
            for r in range(r3):
                o_ref[:, r * HEAD_DIM:(r + 1) * HEAD_DIM] = (partial[r] + gate_s[r] * o_s[r]).astype(BF16)


def _nsa(p4, prep, bias_c, bias_sel, bias_win, overlap, q_norm, layer, tq=WIN_BLOCK, far_keys=512):
    _, b, s, _ = p4.shape
    kc, vc, ks, vs, kw, vw = prep
    nc = kc.shape[2]
    r3 = Q_PER_KV
    n_sel = s // SEL_LEN
    n_cmp = (s - CMP_LEN) // CMP_STRIDE + 1
    band = (N_WIN_PREV + 1) * tq
    padded = ks.shape[2]
    kv_small = pl.BlockSpec((None, None, nc, LANES), lambda i, g, j: (i, g, 0, 0))
    kv_full = pl.BlockSpec((None, None, padded, LANES), lambda i, g, j: (i, g, 0, 0))
    kv_wide = pl.BlockSpec((None, None, padded, 2 * LANES), lambda i, g, j: (i, g, 0, 0))
    return pl.pallas_call(
        functools.partial(_nsa_kernel, n_cmp=n_cmp, n_sel=n_sel),
        out_shape=jax.ShapeDtypeStruct((b, s, N_HEADS_C * HEAD_DIM), BF16),
        grid=(b, KV_HEADS, s // tq),
        in_specs=[pl.BlockSpec((r3, None, tq, LANES), lambda i, g, j: (G_QC // r3 + g, i, j, 0)),
                  pl.BlockSpec((None, None, tq, LANES), lambda i, g, j: (G_SMALL + g, i, j, 0)),
                  kv_small, kv_small, kv_wide, kv_wide, kv_full, kv_wide,
                  pl.BlockSpec((r3, tq, nc), lambda i, g, j: (g, j, 0)),
                  pl.BlockSpec((r3, tq, band), lambda i, g, j: (g, 0, 0)),
                  pl.BlockSpec((r3, tq, band), lambda i, g, j: (g, 0, 0)),
                  pl.BlockSpec((nc, LANES), lambda i, g, j: (0, 0)),
                  pl.BlockSpec((None, 1, LANES), lambda i, g, j: (layer, 0, 0))],
        out_specs=pl.BlockSpec((None, tq, r3 * HEAD_DIM), lambda i, g, j: (i, j, g)),
        scratch_shapes=[pltpu.VMEM((s // far_keys, r3, tq, far_keys), F32)],
        compiler_params=_cparams(("parallel", "parallel", "arbitrary")),
        name="nsa",
    )(p4, p4, kc, vc, ks, vs, kw, vw, bias_c, bias_sel, bias_win, overlap, q_norm)


def _interleave(fragments):
    live = list(fragments)
    while live:
        for fragment in list(live):
            try:
                next(fragment)
            except StopIteration:
                live.remove(fragment)


def _nsa_pair_kernel(q_ref, gate_ref, kc_ref, vc_ref, ks_ref, vs_ref, kw_ref, vw_ref,
                     bc_ref, bs_ref, bw_ref, ov_ref, qg_ref, o_ref, sc_ref, *, n_cmp, n_sel):
    tq = q_ref.shape[1]
    r3 = Q_PER_KV
    i = pl.program_id(1)
    nc = kc_ref.shape[1]
    rows = r3 * tq
    band = (N_WIN_PREV + 1) * tq
    front = N_WIN_PREV * tq
    far_keys = sc_ref.shape[-1]
    boff = pl.multiple_of(i * tq, tq)
    n_far = jnp.maximum(i - 1, 0) // (far_keys // tq)

    qq = lax.broadcasted_iota(jnp.int32, (tq, tq), 0)
    kk = lax.broadcasted_iota(jnp.int32, (tq, tq), 1)
    q_b = lax.broadcasted_iota(jnp.int32, (tq, band), 0)
    k_b = lax.broadcasted_iota(jnp.int32, (tq, band), 1)
    qpos_c = i * tq + lax.broadcasted_iota(jnp.int32, (tq, nc), 0)
    n_idx = lax.broadcasted_iota(jnp.int32, (tq, nc), 1)
    mask_c = ((qpos_c >= n_idx * CMP_STRIDE + (CMP_LEN - 1)) & (n_idx < n_cmp))[None]
    ok_w = (k_b > q_b) & (k_b <= q_b + front) & (k_b >= front - i * tq)
    cur = (i * tq + qq) // SEL_LEN
    forced = (kk == 0) | (kk == cur) | (kk == cur - 1)
    qg = qg_ref[...]

    def lane_fold(x, op):
        out = x[..., 0:tq]
        for c in range(1, x.shape[-1] // tq):
            out = op(out, x[..., c * tq:(c + 1) * tq])
        return out

    def normalise(acc):
        den = jnp.maximum(acc[:, HEAD_DIM:HEAD_DIM + 1], 1e-30)
        return (acc[:, 0:HEAD_DIM] / den).reshape(r3, tq, HEAD_DIM)

    kept = [{} for _ in range(KV_HEADS)]

    def before_selection(g):
        heads = slice(g * r3, (g + 1) * r3)
        qf = jnp.concatenate([_rms(q_ref[g * r3 + r], qg) * (HEAD_DIM ** -0.5) for r in range(r3)], axis=0)
        qb = qf.astype(BF16)
        yield
        s_c = _dot_nt(qb, kc_ref[g]).reshape(r3, tq, nc) + bc_ref[heads]
        s_w = (_dot_nt(qb, kw_ref[g, pl.ds(boff, band), :]).reshape(r3, tq, band) + bw_ref[heads]
               + jnp.where(ok_w, 0.0, NEG_BIG)[None])
        yield
        s_c = jnp.where(mask_c, s_c, NEG_BIG)
        m_c = jnp.max(s_c, axis=-1, keepdims=True)
        p_c = jnp.where(mask_c, jnp.exp(s_c - m_c), 0.0)
        p_c = p_c / jnp.maximum(jnp.sum(p_c, axis=-1, keepdims=True), 1e-30)
        yield
        o_c = _dot(p_c.reshape(rows, nc).astype(BF16), vc_ref[g]).reshape(r3, tq, HEAD_DIM)
        p_sum = p_c[0] + p_c[1] + p_c[2]
        p_hi = p_sum.astype(BF16)
        p_lo = (p_sum - p_hi.astype(F32)).astype(BF16)
        importance = _dot(p_hi, ov_ref[...]) + _dot(p_lo, ov_ref[...])
        yield
        p_w = jnp.exp(s_w - jnp.max(s_w, axis=-1, keepdims=True))
        o_w = normalise(_dot(p_w.reshape(rows, band).astype(BF16), vw_ref[g, pl.ds(boff, band), :]))
        yield
        score = jnp.where(forced, SEL_FORCE, jnp.where(kk <= cur, importance, NEG_BIG))
        n_rows = ((n_sel + 7) // 8) * 8
        st = score.T[0:n_rows, :]
        sub = 8
        slabs = [st[v * sub:(v + 1) * sub, :] for v in range(n_rows // sub)]
        blk = lax.broadcasted_iota(jnp.int32, (sub, tq), 0)
        ranks = [jnp.zeros((sub, tq), F32) for _ in slabs]
        for b in range(n_sel):
            row = st[b:b + 1, :]
            for v, slab in enumerate(slabs):
                if v * sub > b:
                    beats = row >= slab
                elif (v + 1) * sub - 1 <= b:
                    beats = row > slab
                else:
                    beats = (row > slab) | ((row == slab) & (blk + v * sub > b))
                ranks[v] = ranks[v] + jnp.where(beats, 1.0, 0.0)
            if b % 8 == 7:
                yield
        rank = jnp.concatenate(ranks, axis=0)
        drop_t = jnp.where((rank < float(min(SEL_TOPK, n_sel))) & (st > 0.5 * NEG_BIG), 0.0, 1.0)
        if n_rows < tq:
            drop_t = jnp.concatenate([drop_t, jnp.zeros((tq - n_rows, tq), F32)], axis=0)
        drop = drop_t.T.astype(BF16)
        q_sel = jnp.concatenate([qb, jnp.concatenate([drop] * r3, axis=0)], axis=1)
        gates = jax.nn.sigmoid(gate_ref[g])
        col = lambda branch, r: gates[:, LANE_GATE + branch * r3 + r:LANE_GATE + branch * r3 + r + 1]
        kept[g].update(q_sel=q_sel, gate_s=[col(1, r) for r in range(r3)],
                       partial=[col(0, r) * o_c[r] + col(2, r) * o_w[r] for r in range(r3)])

    _interleave(before_selection(g) for g in range(KV_HEADS))

    def half_softmax(g, pieces, with_band, nf, result):
        q_sel = kept[g]["q_sel"]
        keys_of = lambda c, lo, hi: slice(front + c * far_keys + lo, front + c * far_keys + hi)
        mx = None
        if with_band:
            first_near = front - (i * tq - nf * far_keys)
            ok_s = (k_b <= q_b + front) & (k_b >= first_near)
            s_b = (_dot_nt(q_sel, ks_ref[g, pl.ds(boff, band), :]).reshape(r3, tq, band)
                   + bs_ref[g * r3:(g + 1) * r3] + jnp.where(ok_s, 0.0, NEG_BIG)[None])
            mx = lane_fold(s_b, jnp.maximum)
            yield
        for c, lo, hi in pieces:
            s = _dot_nt(q_sel, ks_ref[g, keys_of(c, lo, hi), :]).reshape(r3, tq, hi - lo)
            sc_ref[g, c, :, :, lo:hi] = s
            fold = lane_fold(s, jnp.maximum)
            mx = fold if mx is None else jnp.maximum(mx, fold)
            yield
        m = jnp.max(mx, axis=-1, keepdims=True)
        acc = None
        if with_band:
            acc = _dot(jnp.exp(s_b - m).reshape(rows, band).astype(BF16), vs_ref[g, pl.ds(boff, band), :])
            yield
        for c, lo, hi in pieces:
            p = jnp.exp(sc_ref[g, c, :, :, lo:hi] - m).reshape(rows, hi - lo).astype(BF16)
            part = _dot(p, vs_ref[g, keys_of(c, lo, hi), :])
            acc = part if acc is None else acc + part
            yield
        result.append((m.reshape(rows, 1), acc))

    def selected(nf):
        whole = lambda cs: [(c, 0, far_keys) for c in cs]
        if nf == 0:
            first, second = None, []
        elif nf % 2:
            first, second = whole(range((nf + 1) // 2)), whole(range((nf + 1) // 2, nf))
        else:
            mid = nf // 2
            first = whole(range(mid)) + [(mid, 0, far_keys // 2)]
            second = [(mid, far_keys // 2, far_keys)] + whole(range(mid + 1, nf))
        halves = [([], []) for _ in range(KV_HEADS)]
        fragments = []
        for g in range(KV_HEADS):
            fragments.append(half_softmax(g, second, True, nf, halves[g][1]))
            if first is not None:
                fragments.append(half_softmax(g, first, False, nf, halves[g][0]))
        _interleave(fragments)
        for g in range(KV_HEADS):
            (m_b, acc_b), = halves[g][1]
            if first is None:
                o_s = normalise(acc_b)
            else:
                (m_a, acc_a), = halves[g][0]
                m_s = jnp.maximum(m_a, m_b)
                o_s = normalise(acc_a * jnp.exp(m_a - m_s) + acc_b * jnp.exp(m_b - m_s))
            for r in range(r3):
                out = kept[g]["partial"][r] + kept[g]["gate_s"][r] * o_s[r]
                o_ref[:, (g * r3 + r) * HEAD_DIM:(g * r3 + r + 1) * HEAD_DIM] = out.astype(BF16)

    for nf in range(sc_ref.shape[1]):
        pl.when(n_far == nf)(functools.partial(selected, nf))


def _nsa_pair(p4, prep, bias_c, bias_sel, bias_win, overlap, q_norm, layer, tq=WIN_BLOCK, far_keys=512):
    _, b, s, _ = p4.shape
    kc, vc, ks, vs, kw, vw = prep
    nc = kc.shape[2]
    nq = N_HEADS_C
    n_sel = s // SEL_LEN
    n_cmp = (s - CMP_LEN) // CMP_STRIDE + 1
    band = (N_WIN_PREV + 1) * tq
    padded = ks.shape[2]
    per_batch = lambda *tail: pl.BlockSpec((None, KV_HEADS) + tail, lambda i, j: (i, 0, 0, 0))
    resident = lambda *tail: pl.BlockSpec((None, KV_HEADS) + tail, lambda i, j: (i, 0, 0, 0),
                                          pipeline_mode=pl.Buffered(1))
    fixed = lambda *shape: pl.BlockSpec(shape, lambda i, j: (0,) * len(shape), pipeline_mode=pl.Buffered(1))
    return pl.pallas_call(
        functools.partial(_nsa_pair_kernel, n_cmp=n_cmp, n_sel=n_sel),
        out_shape=jax.ShapeDtypeStruct((b, s, nq * HEAD_DIM), BF16),
        grid=(b, s // tq),
        in_specs=[pl.BlockSpec((nq, None, tq, LANES), lambda i, j: (G_QC // nq, i, j, 0)),
                  pl.BlockSpec((KV_HEADS, None, tq, LANES), lambda i, j: (G_SMALL // KV_HEADS, i, j, 0)),
                  per_batch(nc, LANES), per_batch(nc, LANES),
                  resident(padded, 2 * LANES), resident(padded, 2 * LANES),
                  resident(padded, LANES), resident(padded, 2 * LANES),
                  pl.BlockSpec((nq, tq, nc), lambda i, j: (0, j, 0)),
                  fixed(nq, tq, band), fixed(nq, tq, band), fixed(nc, LANES),
                  pl.BlockSpec((None, 1, LANES), lambda i, j: (layer, 0, 0))],
        out_specs=pl.BlockSpec((None, tq, nq * HEAD_DIM), lambda i, j: (i, j, 0)),
        scratch_shapes=[pltpu.VMEM((KV_HEADS, s // far_keys, Q_PER_KV, tq, far_keys), F32)],
        compiler_params=_cparams(("parallel", "arbitrary")),
        name="nsa",
    )(p4, p4, kc, vc, ks, vs, kw, vw, bias_c, bias_sel, bias_win, overlap, q_norm)


def _t5_bucket(dist):
    n = jnp.maximum(dist, 0)
    max_exact = RPB_BUCKETS // 2
    log_ratio = jnp.log(jnp.maximum(n, 1).astype(F32) / max_exact) / math.log(RPB_MAX_DIST / max_exact)
    large = jnp.minimum(max_exact + (log_ratio * (RPB_BUCKETS - max_exact)).astype(jnp.int32), RPB_BUCKETS - 1)
    return jnp.where(n < max_exact, n, large)


def _bias_tables(rel_bias, s, tq):
    table = rel_bias.astype(F32)
    buckets = jnp.arange(RPB_BUCKETS, dtype=jnp.int32)[:, None, None]

    def look(dist):
        onehot = (_t5_bucket(dist)[None] == buckets).astype(F32)
        return jnp.einsum("nh,nqk->hqk", table, onehot, precision=lax.Precision.HIGHEST)

    nc = s // CMP_STRIDE
    pos = jnp.arange(s, dtype=jnp.int32)
    cmp_end = jnp.arange(nc, dtype=jnp.int32) * CMP_STRIDE + (CMP_LEN - 1)
    bias_c = look(pos[:, None] - cmp_end[None, :])
    q = jnp.arange(tq, dtype=jnp.int32)[:, None]
    kb = jnp.arange((N_WIN_PREV + 1) * tq, dtype=jnp.int32)[None, :]
    bias_win = look(N_WIN_PREV * tq + q - kb)
    bias_sel = bias_win - table[RPB_BUCKETS - 1][:, None, None]
    return bias_c, bias_sel, bias_win


def _selection_constants(s, tq):
    nc = s // CMP_STRIDE
    n_cmp = (s - CMP_LEN) // CMP_STRIDE + 1
    n_sel = s // SEL_LEN
    cmp_start = np.arange(nc) * CMP_STRIDE
    sel_start = np.arange(LANES) * SEL_LEN
    overlap = ((cmp_start[:, None] < sel_start[None, :] + SEL_LEN)
               & (cmp_start[:, None] + CMP_LEN > sel_start[None, :])
               & (np.arange(nc)[:, None] < n_cmp) & (np.arange(LANES)[None, :] < n_sel))
    key_pos = np.arange(-WINDOW, s)[:, None]
    in_block = (np.arange(LANES)[None, :] == (key_pos // SEL_LEN)) & (key_pos >= 0)
    block_mask = jnp.where(jnp.asarray(in_block), NEG_BIG, 0.0).astype(BF16)
    ones_col = jnp.asarray((np.arange(LANES)[None, :] == 0) & (key_pos >= 0), BF16)
    return jnp.asarray(overlap, BF16), block_mask, ones_col


def _arrange_w_in(w_in):
    depth, d, _ = w_in.shape
    w_in = w_in.astype(BF16)
    da, db, dc, dkv = N_HEADS_A * 128, N_HEADS_B * 128, N_HEADS_C * 128, KV_HEADS * 128
    o_ba = 4 * da
    o_aa = o_ba + N_HEADS_A
    o_ub = o_aa + N_HEADS_A
    o_qc = o_ub + 2 * db
    o_kc = o_qc + dc
    o_gc = o_kc + 6 * dkv
    sl = lambda lo, n: w_in[:, :, lo:lo + n]
    gate = w_in[:, :, o_gc:o_gc + 3 * N_HEADS_C].reshape(depth, d, 3, KV_HEADS, Q_PER_KV)
    zeros = lambda n: jnp.zeros((depth, d, n), w_in.dtype)
    small0 = jnp.concatenate([sl(o_ba, 2 * N_HEADS_A), gate[:, :, :, 0, :].reshape(depth, d, 3 * Q_PER_KV),
                              zeros(LANES - 2 * N_HEADS_A - 3 * Q_PER_KV)], axis=-1)
    small1 = jnp.concatenate([zeros(LANE_GATE), gate[:, :, :, 1, :].reshape(depth, d, 3 * Q_PER_KV),
                              zeros(LANES - LANE_GATE - 3 * Q_PER_KV)], axis=-1)
    parts = [sl(0, 4 * da),
             sl(o_qc, dc),
             sl(o_kc, dkv),
             sl(o_ub, 2 * db),
             sl(o_kc + dkv, 5 * dkv),
             small0, small1]
    return jnp.concatenate(parts, axis=-1).astype(BF16)


def kernel(x, attn_norm, w_in, conv_a, a_log, dt_bias, gdn_norm, sgu_ln_g, sgu_ln_b, sgu_w, sgu_b,
           nsa_q_norm, nsa_k_norm, cmp_pos, cmp_w1, cmp_w2, rel_bias, w_out, mlp_norm, w_up, w_down):
    b, s, d = x.shape
    depth = w_in.shape[0]
    t = b * s
    tq = 128

    w_in_r = _arrange_w_in(w_in)
    w_out_b = w_out.astype(BF16)
    w_up_b = w_up.astype(BF16)
    w_down_b = w_down.astype(BF16)
    attn_g = attn_norm.reshape(depth, 1, d)
    mlp_g = mlp_norm.reshape(depth, 1, d)
    conv_r = conv_a.reshape(depth, CONV_WIDTH * 3 * N_HEADS_A, LANES)
    pad_to = lambda v, lo: jnp.pad(v, ((0, 0), (lo, LANES - lo - v.shape[1])))
    ab = jnp.stack([pad_to(a_log, LANE_DECAY), pad_to(dt_bias, LANE_DECAY)], axis=1)
    gdn_g = gdn_norm.reshape(depth, 1, LANES)
    ln_g = sgu_ln_g.reshape(depth, N_HEADS_B, LANES)
    ln_b = sgu_ln_b.reshape(depth, N_HEADS_B, LANES)
    sgu_bias = jnp.broadcast_to(sgu_b[..., None], sgu_b.shape + (LANES,))
    q_g = nsa_q_norm.reshape(depth, 1, LANES)
    k_g = nsa_k_norm.reshape(depth, 1, LANES)
    w1_b = cmp_w1.astype(BF16)
    w2_b = cmp_w2.astype(BF16)
    pos_b = jnp.broadcast_to(cmp_pos.reshape(depth, 2, 1, CMP_LEN * LANES),
                             (depth, 2, 8, CMP_LEN * LANES)).astype(BF16)
    bias_c, bias_sel, bias_win = _bias_tables(rel_bias, s, tq)
    overlap, block_mask, ones_col = _selection_constants(s, tq)

    x2 = x.reshape(t, d)
    for layer in range(depth):
        p = _inproj(x2, attn_g, w_in_r, layer)
        p4 = p.reshape(N_GROUPS, b, s, LANES)
        mix_a = _gdn(p4, conv_r, ab, gdn_g, layer)
        mix_b = _sgu(p4, ln_g, ln_b, sgu_w, sgu_bias, layer)
        prep = _nsa_prep(p4, w1_b, w2_b, pos_b, k_g, block_mask, ones_col, layer)
        mix_c = _nsa_pair(p4, prep, bias_c, bias_sel, bias_win, overlap, q_g, layer, tq=tq)
        x2 = _outproj(x2, mix_a.reshape(t, -1), mix_b.reshape(t, -1), mix_c.reshape(t, -1), w_out_b, layer)
        x2 = _mlp(x2, mlp_g, w_up_b, w_down_b, layer)
    return x2.reshape(b, s, d)
```

```python
import functools
import math

import numpy as np
import jax
import jax.numpy as jnp
from jax import lax
from jax.experimental import pallas as pl
from jax.experimental.pallas import tpu as pltpu

F32 = jnp.float32
BF16 = jnp.bfloat16

LANES = 128
HEAD_DIM = 128
N_HEADS_A = 6
N_HEADS_B = 4
N_HEADS_C = 6
KV_HEADS = 2
Q_PER_KV = N_HEADS_C // KV_HEADS
CONV_WIDTH = 4
GDN_CHUNK = 64
SGU_CHUNK = 128
CMP_LEN = 32
CMP_STRIDE = 16
SEL_LEN = 64
SEL_TOPK = 16
WINDOW = 512
WIN_BLOCK = 128
N_WIN_PREV = WINDOW // WIN_BLOCK
RPB_BUCKETS = 32
RPB_MAX_DIST = 128
NORM_EPS = 1e-6
NEG_BIG = -1e30
SEL_FORCE = 1e9
VMEM_LIMIT = 56 * 1024 * 1024

G_QA, G_KA, G_VA, G_ZA = 0, 6, 12, 18
G_QC = 24
G_KCMP = 30
G_UB, G_VB = 32, 36
G_VCMP = 40
G_KSLC, G_VSLC, G_KWIN, G_VWIN = 42, 44, 46, 48
G_SMALL = 50
N_GROUPS = 52
LANE_BETA, LANE_DECAY, LANE_GATE = 0, 6, 12


def _dot(a, b, precision=None):
    return jnp.dot(a, b, preferred_element_type=F32, precision=precision)


def _dot_nt(a, b):
    return lax.dot_general(a, b, (((1,), (1,)), ((), ())), preferred_element_type=F32)


def _dot_tn(a, b):
    return lax.dot_general(a, b, (((0,), (0,)), ((), ())), preferred_element_type=F32)


def _rms(x, gain):
    return x * lax.rsqrt(jnp.mean(x * x, axis=-1, keepdims=True) + NORM_EPS) * gain


def _cparams(sem):
    return pltpu.CompilerParams(dimension_semantics=sem, vmem_limit_bytes=VMEM_LIMIT)


def _inproj_kernel(x0_ref, xn_ref, g_ref, w_ref, o_ref, h_ref, *, steps):
    i, j = pl.program_id(0), pl.program_id(1)
    tm, sublanes = xn_ref.shape[0], 8
    part = -(-tm // (steps * sublanes)) * sublanes

    @pl.when((i == 0) & (j == 0))
    def _():
        h_ref[0] = _rms(x0_ref[...], g_ref[...]).astype(BF16)

    cur = i % 2
    r = _dot(h_ref[cur], w_ref[...])
    for c in range(o_ref.shape[0]):
        o_ref[c] = r[:, c * LANES:(c + 1) * LANES]

    rows = pl.ds(pl.multiple_of(jnp.minimum(j * part, tm - part), sublanes), part)
    h_ref[1 - cur, rows, :] = _rms(xn_ref[rows, :], g_ref[...]).astype(BF16)


def _inproj(x2, gain, w, layer, tm=1024, tn=4 * LANES):
    t, d = x2.shape
    n = w.shape[-1]
    last = t // tm - 1
    return pl.pallas_call(
        functools.partial(_inproj_kernel, steps=n // tn),
        out_shape=jax.ShapeDtypeStruct((n // LANES, t, LANES), F32),
        grid=(t // tm, n // tn),
        in_specs=[pl.BlockSpec((tm, d), lambda i, j: (0, 0), pipeline_mode=pl.Buffered(1)),
                  pl.BlockSpec((tm, d), lambda i, j: (jnp.minimum(i + 1, last), 0)),
                  pl.BlockSpec((None, 1, d), lambda i, j: (layer, 0, 0)),
                  pl.BlockSpec((None, d, tn), lambda i, j: (layer, 0, j))],
        out_specs=pl.BlockSpec((tn // LANES, tm, LANES), lambda i, j: (j, i, 0)),
        scratch_shapes=[pltpu.VMEM((2, tm, d), BF16)],
        compiler_params=_cparams(("arbitrary", "arbitrary")),
        name="inproj",
    )(x2, x2, gain, w)


def _outproj_kernel(x_ref, a_ref, b_ref, c_ref, w_ref, o_ref):
    da, db = a_ref.shape[1], b_ref.shape[1]
    acc = _dot(a_ref[...], w_ref[0:da, :])
    acc += _dot(b_ref[...], w_ref[da:da + db, :])
    acc += _dot(c_ref[...], w_ref[da + db:, :])
    o_ref[...] = x_ref[...] + acc


def _outproj(x2, ma, mb, mc, w_out, layer, tm=512):
    t, d = x2.shape
    da, db, dc = ma.shape[1], mb.shape[1], mc.shape[1]
    return pl.pallas_call(
        _outproj_kernel,
        out_shape=jax.ShapeDtypeStruct((t, d), F32),
        grid=(t // tm,),
        in_specs=[pl.BlockSpec((tm, d), lambda i: (i, 0)),
                  pl.BlockSpec((tm, da), lambda i: (i, 0)),
                  pl.BlockSpec((tm, db), lambda i: (i, 0)),
                  pl.BlockSpec((tm, dc), lambda i: (i, 0)),
                  pl.BlockSpec((None, da + db + dc, d), lambda i: (layer, 0, 0))],
        out_specs=pl.BlockSpec((tm, d), lambda i: (i, 0)),
        compiler_params=_cparams(("parallel",)),
        name="outproj",
    )(x2, ma, mb, mc, w_out)


def _mlp_kernel(x0_ref, xn_ref, g_ref, wu_ref, wd_ref, o_ref, h_ref, keep_ref):
    i, k = pl.program_id(0), pl.program_id(1)
    part = xn_ref.shape[0] // pl.num_programs(1)
    cur = i % 2

    @pl.when((i == 0) & (k == 0))
    def _():
        keep_ref[0] = x0_ref[...]
        h_ref[0] = _rms(x0_ref[...], g_ref[...]).astype(BF16)

    @pl.when(k == 0)
    def _():
        o_ref[...] = keep_ref[cur]

    a = jnp.maximum(_dot(h_ref[cur], wu_ref[...]), 0.0)
    o_ref[...] += _dot((a * a).astype(BF16), wd_ref[...])

    rows = pl.ds(pl.multiple_of(k * part, part), part)
    nxt = xn_ref[rows, :]
    keep_ref[1 - cur, rows, :] = nxt
    h_ref[1 - cur, rows, :] = _rms(nxt, g_ref[...]).astype(BF16)


def _mlp(x2, gain, w_up, w_down, layer, tm=512, tf=1024):
    t, d = x2.shape
    f = w_up.shape[-1]
    last = t // tm - 1
    return pl.pallas_call(
        _mlp_kernel,
        out_shape=jax.ShapeDtypeStruct((t, d), F32),
        grid=(t // tm, f // tf),
        in_specs=[pl.BlockSpec((tm, d), lambda i, k: (0, 0), pipeline_mode=pl.Buffered(1)),
                  pl.BlockSpec((tm, d), lambda i, k: (jnp.minimum(i + 1, last), 0)),
                  pl.BlockSpec((None, 1, d), lambda i, k: (layer, 0, 0)),
                  pl.BlockSpec((None, d, tf), lambda i, k: (layer, 0, k)),
                  pl.BlockSpec((None, tf, d), lambda i, k: (layer, k, 0))],
        out_specs=pl.BlockSpec((tm, d), lambda i, k: (i, 0)),
        scratch_shapes=[pltpu.VMEM((2, tm, d), BF16), pltpu.VMEM((2, tm, d), F32)],
        compiler_params=_cparams(("arbitrary", "arbitrary")),
        name="mlp",
    )(x2, x2, gain, w_up, w_down)


def _sgu_kernel(p_ref, lg_ref, lb_ref, w_ref, b_ref, o_ref):
    ts = p_ref.shape[1]
    nb = N_HEADS_B
    v = [jax.nn.gelu(p_ref[nb + g]) for g in range(nb)]
    width = float(nb * LANES)
    mu = sum(jnp.sum(vg, axis=-1, keepdims=True) for vg in v) / width
    var = sum(jnp.sum(jnp.square(vg - mu), axis=-1, keepdims=True) for vg in v) / width
    inv = lax.rsqrt(var + NORM_EPS)
    row = lax.broadcasted_iota(jnp.int32, (SGU_CHUNK, SGU_CHUNK), 0)
    col = lax.broadcasted_iota(jnp.int32, (SGU_CHUNK, SGU_CHUNK), 1)
    causal = col <= row
    for g in range(nb):
        vn = ((v[g] - mu) * inv * lg_ref[g:g + 1, :] + lb_ref[g:g + 1, :]).astype(BF16)
        wg = jnp.where(causal, w_ref[g], 0.0).astype(BF16)
        for c in range(ts // SGU_CHUNK):
            rows = slice(c * SGU_CHUNK, (c + 1) * SGU_CHUNK)
            mixed = _dot(wg, vn[rows]) + b_ref[g]
            u = jax.nn.gelu(p_ref[g, rows, :])
            o_ref[rows, g * LANES:(g + 1) * LANES] = (u * mixed).astype(BF16)


def _sgu(p4, ln_g, ln_b, w_s, b_s, layer, ts=512):
    _, b, s, _ = p4.shape
    nb = N_HEADS_B
    return pl.pallas_call(
        _sgu_kernel,
        out_shape=jax.ShapeDtypeStruct((b, s, nb * LANES), BF16),
        grid=(b, s // ts),
        in_specs=[pl.BlockSpec((2 * nb, None, ts, LANES), lambda i, j: (G_UB // (2 * nb), i, j, 0)),
                  pl.BlockSpec((None, nb, LANES), lambda i, j: (layer, 0, 0)),
                  pl.BlockSpec((None, nb, LANES), lambda i, j: (layer, 0, 0)),
                  pl.BlockSpec((None, nb, SGU_CHUNK, SGU_CHUNK), lambda i, j: (layer, 0, 0, 0)),
                  pl.BlockSpec((None, nb, SGU_CHUNK, LANES), lambda i, j: (layer, 0, 0, 0))],
        out_specs=pl.BlockSpec((None, ts, nb * LANES), lambda i, j: (i, j, 0)),
        compiler_params=_cparams(("parallel", "parallel")),
        name="sgu",
    )(p4, ln_g, ln_b, w_s, b_s)


def _gdn_kernel(a_ref, sm_ref, cw_ref, ab_ref, ng_ref, o_ref,
                xbuf, q_s, k_s, v_s, beta_s, gcum_s, u_s, w_s, qg_s, kd_s, a_s, st_ref, *, batch):
    ts = a_ref.shape[1]
    ck = GDN_CHUNK
    nh = N_HEADS_A
    pad = 8

    @pl.when(pl.program_id(1) == 0)
    def _():
        xbuf[:, 0:pad, :] = jnp.zeros((3 * nh, pad, LANES), F32)
        st_ref[...] = jnp.zeros_like(st_ref)

    xbuf[:, pad:pad + ts, :] = a_ref[0:3 * nh]
    dests = (q_s, k_s, v_s)
    for grp in range(3 * nh):
        acc = None
        for j in range(CONV_WIDTH):
            lo = pad - (CONV_WIDTH - 1) + j
            term = xbuf[grp, lo:lo + ts, :] * cw_ref[j * 3 * nh + grp:j * 3 * nh + grp + 1, :]
            acc = term if acc is None else acc + term
        y = acc * jax.nn.sigmoid(acc)
        kind, h = divmod(grp, nh)
        if kind < 2:
            y = y * lax.rsqrt(jnp.sum(y * y, axis=-1, keepdims=True) + NORM_EPS)
        dests[kind][h] = y
    xbuf[:, 0:pad, :] = xbuf[:, ts:ts + pad, :]

    sm = sm_ref[...]
    beta_s[...] = jax.nn.sigmoid(sm)
    z = sm + ab_ref[1:2, :]
    softplus = jnp.maximum(z, 0.0) + jnp.log1p(jnp.exp(-jnp.abs(z)))
    g_all = -jnp.exp(ab_ref[0:1, :]) * softplus

    ii = lax.broadcasted_iota(jnp.int32, (ck, ck), 0)
    jj = lax.broadcasted_iota(jnp.int32, (ck, ck), 1)
    incl = jj <= ii
    strict = jj < ii
    tri = jnp.where(incl, 1.0, 0.0).astype(F32)
    for c in range(ts // ck):
        rows = slice(c * ck, (c + 1) * ck)
        gcum_s[rows, :] = _dot(tri, g_all[rows], precision=lax.Precision.HIGHEST)

    scale = HEAD_DIM ** -0.5
    gain = ng_ref[...]
    bf = lambda m: m.astype(BF16)
    same = lambda size: (ii // size) == (jj // size)
    n_chunks = ts // ck
    rows_of = lambda c: slice(c * ck, (c + 1) * ck)
    dcol = lambda h: slice(LANE_DECAY + h, LANE_DECAY + h + 1)

    for c0 in range(0, n_chunks, batch):
        gc = {c: gcum_s[rows_of(c), :] for c in range(c0, c0 + batch)}
        gct = {c: gc[c].T for c in gc}
        items = [(c, h) for c in range(c0, c0 + batch) for h in range(nh)]
        kk, qk = {}, {}
        for c, h in items:
            g_col = gc[c][:, dcol(h)]
            q = q_s[h, rows_of(c), :] * scale
            k = k_s[h, rows_of(c), :]
            kbf = bf(k)
            kk[c, h] = _dot_nt(bf(k * beta_s[rows_of(c), LANE_BETA + h:LANE_BETA + h + 1]), kbf)
            qk[c, h] = _dot_nt(bf(q), kbf)
            qg_s[h, rows_of(c), :] = bf(q * jnp.exp(g_col))
            kd_s[h, rows_of(c), :] = bf(k * jnp.exp(gc[c][ck - 1:ck, dcol(h)] - g_col))
        lower, diag, p1 = {}, {}, {}
        for it in items:
            c, h = it
            decay = jnp.exp(jnp.where(incl, gc[c][:, dcol(h)] - gct[c][dcol(h), :], NEG_BIG))
            lower[it] = jnp.where(strict, kk[it] * decay, 0.0)
            a_s[h, rows_of(c), :] = bf(jnp.where(incl, qk[it] * decay, 0.0))
            diag[it] = jnp.where(same(8), lower[it], 0.0)
            p1[it] = _dot(bf(diag[it]), bf(diag[it]))
        p2 = {it: _dot(bf(p1[it]), bf(p1[it])) for it in items}
        dp1 = {it: _dot(bf(diag[it]), bf(p1[it])) for it in items}
        na = {it: p1[it] - diag[it] - dp1[it] for it in items}
        nap2 = {it: _dot(bf(na[it]), bf(p2[it])) for it in items}
        n = {it: na[it] + p2[it] + nap2[it] for it in items}
        size = 8
        while size < ck:
            off = same(2 * size) & ~same(size)
            cm = {it: jnp.where(off, lower[it], 0.0) for it in items}
            y = {it: cm[it] + _dot(bf(n[it]), bf(cm[it])) for it in items}
            yn = {it: _dot(bf(y[it]), bf(n[it])) for it in items}
            n = {it: n[it] - (y[it] + yn[it]) for it in items}
            size *= 2
        for it in items:
            c, h = it
            beta = beta_s[rows_of(c), LANE_BETA + h:LANE_BETA + h + 1]
            nb = bf(n[it])
            vb = v_s[h, rows_of(c), :] * beta
            kw = k_s[h, rows_of(c), :] * (beta * jnp.exp(gc[c][:, dcol(h)]))
            u_s[h, rows_of(c), :] = vb + _dot(nb, bf(vb))
            w_s[h, rows_of(c), :] = bf(kw + _dot(nb, bf(kw)))

    heads = range(nh)
    for c in range(n_chunks):
        rows = rows_of(c)
        state = [st_ref[h] for h in heads]
        sb = [bf(s) for s in state]
        ws = [_dot(w_s[h, rows, :], sb[h]) for h in heads]
        qs = [_dot(qg_s[h, rows, :], sb[h]) for h in heads]
        vnb = [bf(u_s[h, rows, :] - ws[h]) for h in heads]
        kv = [_dot_tn(kd_s[h, rows, :], vnb[h]) for h in heads]
        av = [_dot(a_s[h, rows, :], vnb[h]) for h in heads]
        for h in heads:
            g_last = gcum_s[(c + 1) * ck - 1:(c + 1) * ck, dcol(h)]
            st_ref[h] = state[h] * jnp.exp(g_last) + kv[h]
            zg = a_ref[3 * nh + h, rows, :]
            out = _rms(qs[h] + av[h], gain) * (zg * jax.nn.sigmoid(zg))
            o_ref[rows, h * LANES:(h + 1) * LANES] = out.astype(BF16)


def _gdn(p4, conv_w, ab, norm_g, layer, ts=512, batch=4):
    _, b, s, _ = p4.shape
    nh = N_HEADS_A
    return pl.pallas_call(
        functools.partial(_gdn_kernel, batch=batch),
        out_shape=jax.ShapeDtypeStruct((b, s, nh * LANES), BF16),
        grid=(b, s // ts),
        in_specs=[pl.BlockSpec((4 * nh, None, ts, LANES), lambda i, j: (0, i, j, 0)),
                  pl.BlockSpec((None, None, ts, LANES), lambda i, j: (G_SMALL, i, j, 0)),
                  pl.BlockSpec((None, CONV_WIDTH * 3 * nh, LANES), lambda i, j: (layer, 0, 0)),
                  pl.BlockSpec((None, 2, LANES), lambda i, j: (layer, 0, 0)),
                  pl.BlockSpec((None, 1, LANES), lambda i, j: (layer, 0, 0))],
        out_specs=pl.BlockSpec((None, ts, nh * LANES), lambda i, j: (i, j, 0)),
        scratch_shapes=[pltpu.VMEM((3 * nh, ts + 8, LANES), F32),
                        pltpu.VMEM((nh, ts, LANES), F32),
                        pltpu.VMEM((nh, ts, LANES), F32),
                        pltpu.VMEM((nh, ts, LANES), F32),
                        pltpu.VMEM((ts, LANES), F32),
                        pltpu.VMEM((ts, LANES), F32),
                        pltpu.VMEM((nh, ts, HEAD_DIM), F32),
                        pltpu.VMEM((nh, ts, HEAD_DIM), BF16),
                        pltpu.VMEM((nh, ts, HEAD_DIM), BF16),
                        pltpu.VMEM((nh, ts, HEAD_DIM), BF16),
                        pltpu.VMEM((nh, ts, GDN_CHUNK), BF16),
                        pltpu.VMEM((nh, HEAD_DIM, HEAD_DIM), F32)],
        compiler_params=_cparams(("parallel", "arbitrary")),
        name="gdn",
    )(p4, p4, conv_w, ab, norm_g)


def _nsa_prep_kernel(kc2_ref, vc2_ref, ksl_ref, vsl_ref, kwn_ref, vwn_ref,
                     w1_ref, w2_ref, pos_ref, kg_ref, bm_ref, one_ref,
                     kc_o, vc_o, ks_o, vs_o, kw_o, vw_o):
    nc = kc_o.shape[0]
    half = CMP_STRIDE * LANES
    kg = kg_ref[...]
    for kind, (src, dst) in enumerate(((kc2_ref, kc_o), (vc2_ref, vc_o))):
        t2 = jnp.concatenate([src[pl.ds(l, nc, stride=CMP_STRIDE), :].astype(BF16)
                              for l in range(CMP_STRIDE)], axis=1)
        first = _dot(t2, w1_ref[kind, 0:half, :])
        second = _dot(t2, w1_ref[kind, half:2 * half, :])
        const = _dot(pos_ref[kind], w1_ref[kind])[0:1, :]
        pre = first + pltpu.roll(second, nc - 1, 0) + const
        y = _dot(jax.nn.gelu(pre).astype(BF16), w2_ref[kind])
        if kind == 0:
            y = _rms(y, kg)
        dst[...] = y.astype(BF16)
    front = ks_o.shape[0] - ksl_ref.shape[0]
    for dst, val in ((ks_o, _rms(ksl_ref[...], kg)), (kw_o, _rms(kwn_ref[...], kg)),
                     (vs_o, vsl_ref[...]), (vw_o, vwn_ref[...])):
        dst[0:front, 0:LANES] = jnp.zeros((front, LANES), BF16)
        dst[front:, 0:LANES] = val.astype(BF16)
    ks_o[:, LANES:] = bm_ref[...]
    vs_o[:, LANES:] = one_ref[...]
    vw_o[:, LANES:] = one_ref[...]


def _nsa_prep(p4, w1, w2, pos, k_norm, block_mask, ones_col, layer):
    ng, b, s, _ = p4.shape
    padded = s + WINDOW
    nc = s // CMP_STRIDE
    half = CMP_STRIDE * LANES
    grp = lambda base: (lambda i, g: (base + g, i, 0, 0))
    small = jax.ShapeDtypeStruct((b, KV_HEADS, nc, LANES), BF16)
    full = jax.ShapeDtypeStruct((b, KV_HEADS, padded, LANES), BF16)
    wide = jax.ShapeDtypeStruct((b, KV_HEADS, padded, 2 * LANES), BF16)
    out_small = pl.BlockSpec((None, None, nc, LANES), lambda i, g: (i, g, 0, 0))
    out_full = pl.BlockSpec((None, None, padded, LANES), lambda i, g: (i, g, 0, 0))
    out_wide = pl.BlockSpec((None, None, padded, 2 * LANES), lambda i, g: (i, g, 0, 0))
    const = pl.BlockSpec((padded, LANES), lambda i, g: (0, 0))
    return pl.pallas_call(
        _nsa_prep_kernel,
        out_shape=(small, small, wide, wide, full, wide),
        grid=(b, KV_HEADS),
        in_specs=[pl.BlockSpec((None, None, s, LANES), grp(G_KCMP)),
                  pl.BlockSpec((None, None, s, LANES), grp(G_VCMP)),
                  pl.BlockSpec((None, None, s, LANES), grp(G_KSLC)),
                  pl.BlockSpec((None, None, s, LANES), grp(G_VSLC)),
                  pl.BlockSpec((None, None, s, LANES), grp(G_KWIN)),
                  pl.BlockSpec((None, None, s, LANES), grp(G_VWIN)),
                  pl.BlockSpec((None, 2, 2 * half, LANES), lambda i, g: (layer, 0, 0, 0)),
                  pl.BlockSpec((None, 2, LANES, LANES), lambda i, g: (layer, 0, 0, 0)),
                  pl.BlockSpec((None, 2, 8, 2 * half), lambda i, g: (layer, 0, 0, 0)),
                  pl.BlockSpec((None, 1, LANES), lambda i, g: (layer, 0, 0)),
                  const, const],
        out_specs=(out_small, out_small, out_wide, out_wide, out_full, out_wide),
        compiler_params=_cparams(("parallel", "parallel")),
        name="nsa_prep",
    )(p4, p4, p4, p4, p4, p4, w1, w2, pos, k_norm, block_mask, ones_col)


def _nsa_kernel(q_ref, gate_ref, kc_ref, vc_ref, ks_ref, vs_ref, kw_ref, vw_ref,
                bc_ref, bs_ref, bw_ref, ov_ref, qg_ref, o_ref, sc_ref, *, n_cmp, n_sel):
    tq = q_ref.shape[1]
    r3 = Q_PER_KV
    i = pl.program_id(2)
    nc = kc_ref.shape[0]
    rows = r3 * tq

    qg = qg_ref[...]
    qf = jnp.concatenate([_rms(q_ref[r], qg) * (HEAD_DIM ** -0.5) for r in range(r3)], axis=0)
    qb = qf.astype(BF16)

    qq = lax.broadcasted_iota(jnp.int32, (tq, tq), 0)
    kk = lax.broadcasted_iota(jnp.int32, (tq, tq), 1)

    s_c = _dot_nt(qb, kc_ref[...]).reshape(r3, tq, nc) + bc_ref[...]
    qpos_c = i * tq + lax.broadcasted_iota(jnp.int32, (tq, nc), 0)
    n_idx = lax.broadcasted_iota(jnp.int32, (tq, nc), 1)
    mask_c = ((qpos_c >= n_idx * CMP_STRIDE + (CMP_LEN - 1)) & (n_idx < n_cmp))[None]
    s_c = jnp.where(mask_c, s_c, NEG_BIG)
    m_c = jnp.max(s_c, axis=-1, keepdims=True)
    p_c = jnp.where(mask_c, jnp.exp(s_c - m_c), 0.0)
    p_c = p_c / jnp.maximum(jnp.sum(p_c, axis=-1, keepdims=True), 1e-30)
    o_c = _dot(p_c.reshape(rows, nc).astype(BF16), vc_ref[...])
    p_sum = p_c[0] + p_c[1] + p_c[2]
    p_hi = p_sum.astype(BF16)
    p_lo = (p_sum - p_hi.astype(F32)).astype(BF16)
    importance = _dot(p_hi, ov_ref[...]) + _dot(p_lo, ov_ref[...])

    cur = (i * tq + qq) // SEL_LEN
    forced = (kk == 0) | (kk == cur) | (kk == cur - 1)
    score = jnp.where(forced, SEL_FORCE, jnp.where(kk <= cur, importance, NEG_BIG))
    st = score.T

    def selection_queries(n_blocks):
        sub = 8
        n_blocks = min(n_blocks, n_sel)
        slabs = [st[v * sub:(v + 1) * sub, :] for v in range((n_blocks + sub - 1) // sub)]
        keep = [slab > 0.5 * NEG_BIG for slab in slabs]
        if n_blocks > SEL_TOPK:
            blk = lax.broadcasted_iota(jnp.int32, (sub, tq), 0)
            ranks = [jnp.zeros((sub, tq), F32) for _ in slabs]
            for b in range(n_blocks):
                row = st[b:b + 1, :]
                for v, slab in enumerate(slabs):
                    if v * sub > b:
                        beats = row >= slab
                    elif (v + 1) * sub - 1 <= b:
                        beats = row > slab
                    else:
                        beats = (row > slab) | ((row == slab) & (blk + v * sub > b))
                    ranks[v] = ranks[v] + jnp.where(beats, 1.0, 0.0)
            keep = [k & (r < float(SEL_TOPK)) for k, r in zip(keep, ranks)]
        drop_t = jnp.concatenate([jnp.where(k, 0.0, 1.0) for k in keep]
                                 + [jnp.ones((tq - sub * len(slabs), tq), F32)], axis=0)
        drop = drop_t.T.astype(BF16)
        return jnp.concatenate([qb, jnp.concatenate([drop] * r3, axis=0)], axis=1)

    def normalise(acc):
        den = jnp.maximum(acc[:, HEAD_DIM:HEAD_DIM + 1], 1e-30)
        return (acc[:, 0:HEAD_DIM] / den).reshape(r3, tq, HEAD_DIM)

    band = (N_WIN_PREV + 1) * tq
    front = N_WIN_PREV * tq
    far_keys = sc_ref.shape[-1]
    boff = pl.multiple_of(i * tq, tq)
    q_b = lax.broadcasted_iota(jnp.int32, (tq, band), 0)
    k_b = lax.broadcasted_iota(jnp.int32, (tq, band), 1)

    def lane_fold(x, op):
        out = x[..., 0:tq]
        for c in range(1, x.shape[-1] // tq):
            out = op(out, x[..., c * tq:(c + 1) * tq])
        return out

    n_far = jnp.maximum(i - 1, 0) // (far_keys // tq)

    def selected(nf):
        last_tile = (nf + 1) * (far_keys // tq)
        q_sel = selection_queries((last_tile + 1) * tq // SEL_LEN)
        first_near = front - (i * tq - nf * far_keys)
        ok_s = (k_b <= q_b + front) & (k_b >= first_near)
        def group(pieces, with_band):
            keys_of = lambda c, lo, hi: slice(front + c * far_keys + lo, front + c * far_keys + hi)
            mx = None
            if with_band:
                s_b = (_dot_nt(q_sel, ks_ref[pl.ds(boff, band), :]).reshape(r3, tq, band) + bs_ref[...]
                       + jnp.where(ok_s, 0.0, NEG_BIG)[None])
                mx = lane_fold(s_b, jnp.maximum)
            for c, lo, hi in pieces:
                s = _dot_nt(q_sel, ks_ref[keys_of(c, lo, hi), :]).reshape(r3, tq, hi - lo)
                sc_ref[c, :, :, lo:hi] = s
                fold = lane_fold(s, jnp.maximum)
                mx = fold if mx is None else jnp.maximum(mx, fold)
            m = jnp.max(mx, axis=-1, keepdims=True)
            acc = None
            if with_band:
                acc = _dot(jnp.exp(s_b - m).reshape(rows, band).astype(BF16), vs_ref[pl.ds(boff, band), :])
            for c, lo, hi in pieces:
                p = jnp.exp(sc_ref[c, :, :, lo:hi] - m).reshape(rows, hi - lo).astype(BF16)
                part = _dot(p, vs_ref[keys_of(c, lo, hi), :])
                acc = part if acc is None else acc + part
            return m.reshape(rows, 1), acc

        if nf == 0:
            return normalise(group([], True)[1])
        whole = lambda cs: [(c, 0, far_keys) for c in cs]
        if nf % 2:
            first, second = whole(range((nf + 1) // 2)), whole(range((nf + 1) // 2, nf))
        else:
            mid = nf // 2
            first = whole(range(mid)) + [(mid, 0, far_keys // 2)]
            second = [(mid, far_keys // 2, far_keys)] + whole(range(mid + 1, nf))
        m_b, acc_b = group(second, True)
        m_a, acc_a = group(first, False)
        m_s = jnp.maximum(m_a, m_b)
        return normalise(acc_a * jnp.exp(m_a - m_s) + acc_b * jnp.exp(m_b - m_s))

    ok_w = (k_b > q_b) & (k_b <= q_b + front) & (k_b >= front - i * tq)
    s_w = (_dot_nt(qb, kw_ref[pl.ds(boff, band), :]).reshape(r3, tq, band) + bw_ref[...]
           + jnp.where(ok_w, 0.0, NEG_BIG)[None])
    p_w = jnp.exp(s_w - jnp.max(s_w, axis=-1, keepdims=True))
    o_w = normalise(_dot(p_w.reshape(rows, band).astype(BF16), vw_ref[pl.ds(boff, band), :]))

    gates = jax.nn.sigmoid(gate_ref[...])
    o_c = o_c.reshape(r3, tq, HEAD_DIM)
    col = lambda branch, r: gates[:, LANE_GATE + branch * r3 + r:LANE_GATE + branch * r3 + r + 1]
    partial = [col(0, r) * o_c[r] + col(2, r) * o_w[r] for r in range(r3)]
    gate_s = [col(1, r) for r in range(r3)]

    for nf in range(sc_ref.shape[0]):
        @pl.when(n_far == nf)
        def _(nf=nf):
            o_s = selected(nf)
            for r in range(r3):
                o_ref[:, r * HEAD_DIM:(r + 1) * HEAD_DIM] = (partial[r] + gate_s[r] * o_s[r]).astype(BF16)


def _nsa(p4, prep, bias_c, bias_sel, bias_win, overlap, q_norm, layer, tq=WIN_BLOCK, far_keys=512):
    _, b, s, _ = p4.shape
    kc, vc, ks, vs, kw, vw = prep
    nc = kc.shape[2]
    r3 = Q_PER_KV
    n_sel = s // SEL_LEN
    n_cmp = (s - CMP_LEN) // CMP_STRIDE + 1
    band = (N_WIN_PREV + 1) * tq
    padded = ks.shape[2]
    kv_small = pl.BlockSpec((None, None, nc, LANES), lambda i, g, j: (i, g, 0, 0))
    kv_full = pl.BlockSpec((None, None, padded, LANES), lambda i, g, j: (i, g, 0, 0))
    kv_wide = pl.BlockSpec((None, None, padded, 2 * LANES), lambda i, g, j: (i, g, 0, 0))
    return pl.pallas_call(
        functools.partial(_nsa_kernel, n_cmp=n_cmp, n_sel=n_sel),
        out_shape=jax.ShapeDtypeStruct((b, s, N_HEADS_C * HEAD_DIM), BF16),
        grid=(b, KV_HEADS, s // tq),
        in_specs=[pl.BlockSpec((r3, None, tq, LANES), lambda i, g, j: (G_QC // r3 + g, i, j, 0)),
                  pl.BlockSpec((None, None, tq, LANES), lambda i, g, j: (G_SMALL + g, i, j, 0)),
                  kv_small, kv_small, kv_wide, kv_wide, kv_full, kv_wide,
                  pl.BlockSpec((r3, tq, nc), lambda i, g, j: (g, j, 0)),
                  pl.BlockSpec((r3, tq, band), lambda i, g, j: (g, 0, 0)),
                  pl.BlockSpec((r3, tq, band), lambda i, g, j: (g, 0, 0)),
                  pl.BlockSpec((nc, LANES), lambda i, g, j: (0, 0)),
                  pl.BlockSpec((None, 1, LANES), lambda i, g, j: (layer, 0, 0))],
        out_specs=pl.BlockSpec((None, tq, r3 * HEAD_DIM), lambda i, g, j: (i, j, g)),
        scratch_shapes=[pltpu.VMEM((s // far_keys, r3, tq, far_keys), F32)],
        compiler_params=_cparams(("parallel", "parallel", "arbitrary")),
        name="nsa",
    )(p4, p4, kc, vc, ks, vs, kw, vw, bias_c, bias_sel, bias_win, overlap, q_norm)


def _interleave(fragments):
    live = list(fragments)
    while live:
        for fragment in list(live):
            try:
                next(fragment)
            except StopIteration:
                live.remove(fragment)


def _nsa_pair_kernel(q_ref, gate_ref, kc_ref, vc_ref, ks_ref, vs_ref, kw_ref, vw_ref,
                     bc_ref, bs_ref, bw_ref, ov_ref, qg_ref, o_ref, sc_ref, *, n_cmp, n_sel):
    tq = q_ref.shape[1]
    r3 = Q_PER_KV
    i = pl.program_id(1)
    nc = kc_ref.shape[1]
    rows = r3 * tq
    band = (N_WIN_PREV + 1) * tq
    front = N_WIN_PREV * tq
    far_keys = sc_ref.shape[-1]
    boff = pl.multiple_of(i * tq, tq)
    n_far = jnp.maximum(i - 1, 0) // (far_keys // tq)

    qq = lax.broadcasted_iota(jnp.int32, (tq, tq), 0)
    kk = lax.broadcasted_iota(jnp.int32, (tq, tq), 1)
    q_b = lax.broadcasted_iota(jnp.int32, (tq, band), 0)
    k_b = lax.broadcasted_iota(jnp.int32, (tq, band), 1)
    qpos_c = i * tq + lax.broadcasted_iota(jnp.int32, (tq, nc), 0)
    n_idx = lax.broadcasted_iota(jnp.int32, (tq, nc), 1)
    mask_c = ((qpos_c >= n_idx * CMP_STRIDE + (CMP_LEN - 1)) & (n_idx < n_cmp))[None]
    ok_w = (k_b > q_b) & (k_b <= q_b + front) & (k_b >= front - i * tq)
    cur = (i * tq + qq) // SEL_LEN
    forced = (kk == 0) | (kk == cur) | (kk == cur - 1)
    qg = qg_ref[...]

    def lane_fold(x, op):
        out = x[..., 0:tq]
        for c in range(1, x.shape[-1] // tq):
            out = op(out, x[..., c * tq:(c + 1) * tq])
        return out

    def normalise(acc):
        den = jnp.maximum(acc[:, HEAD_DIM:HEAD_DIM + 1], 1e-30)
        return (acc[:, 0:HEAD_DIM] / den).reshape(r3, tq, HEAD_DIM)

    kept = [{} for _ in range(KV_HEADS)]

    def before_selection(g, n_blocks):
        heads = slice(g * r3, (g + 1) * r3)
        qf = jnp.concatenate([_rms(q_ref[g * r3 + r], qg) * (HEAD_DIM ** -0.5) for r in range(r3)], axis=0)
        qb = qf.astype(BF16)
        yield
        s_c = _dot_nt(qb, kc_ref[g]).reshape(r3, tq, nc) + bc_ref[heads]
        s_w = (_dot_nt(qb, kw_ref[g, pl.ds(boff, band), :]).reshape(r3, tq, band) + bw_ref[heads]
               + jnp.where(ok_w, 0.0, NEG_BIG)[None])
        yield
        s_c = jnp.where(mask_c, s_c, NEG_BIG)
        m_c = jnp.max(s_c, axis=-1, keepdims=True)
        p_c = jnp.where(mask_c, jnp.exp(s_c - m_c), 0.0)
        p_c = p_c / jnp.maximum(jnp.sum(p_c, axis=-1, keepdims=True), 1e-30)
        yield
        o_c = _dot(p_c.reshape(rows, nc).astype(BF16), vc_ref[g]).reshape(r3, tq, HEAD_DIM)
        p_sum = p_c[0] + p_c[1] + p_c[2]
        p_hi = p_sum.astype(BF16)
        p_lo = (p_sum - p_hi.astype(F32)).astype(BF16)
        importance = _dot(p_hi, ov_ref[...]) + _dot(p_lo, ov_ref[...])
        yield
        p_w = jnp.exp(s_w - jnp.max(s_w, axis=-1, keepdims=True))
        o_w = normalise(_dot(p_w.reshape(rows, band).astype(BF16), vw_ref[g, pl.ds(boff, band), :]))
        yield
        score = jnp.where(forced, SEL_FORCE, jnp.where(kk <= cur, importance, NEG_BIG))
        st = score.T
        sub = 8
        slabs = [st[v * sub:(v + 1) * sub, :] for v in range((n_blocks + sub - 1) // sub)]
        keep = [slab > 0.5 * NEG_BIG for slab in slabs]
        if n_blocks > SEL_TOPK:
            blk = lax.broadcasted_iota(jnp.int32, (sub, tq), 0)
            ranks = [jnp.zeros((sub, tq), F32) for _ in slabs]
            for b in range(n_blocks):
                row = st[b:b + 1, :]
                for v, slab in enumerate(slabs):
                    if v * sub > b:
                        beats = row >= slab
                    elif (v + 1) * sub - 1 <= b:
                        beats = row > slab
                    else:
                        beats = (row > slab) | ((row == slab) & (blk + v * sub > b))
                    ranks[v] = ranks[v] + jnp.where(beats, 1.0, 0.0)
                if b % 8 == 7:
                    yield
            keep = [k & (r < float(SEL_TOPK)) for k, r in zip(keep, ranks)]
        drop_t = jnp.concatenate([jnp.where(k, 0.0, 1.0) for k in keep]
                                 + [jnp.ones((tq - sub * len(slabs), tq), F32)], axis=0)
        drop = drop_t.T.astype(BF16)
        q_sel = jnp.concatenate([qb, jnp.concatenate([drop] * r3, axis=0)], axis=1)
        gates = jax.nn.sigmoid(gate_ref[g])
        col = lambda branch, r: gates[:, LANE_GATE + branch * r3 + r:LANE_GATE + branch * r3 + r + 1]
        kept[g].update(q_sel=q_sel, gate_s=[col(1, r) for r in range(r3)],
                       partial=[col(0, r) * o_c[r] + col(2, r) * o_w[r] for r in range(r3)])

    _interleave(before_selection(g, n_sel) for g in range(KV_HEADS))


    def half_softmax(g, pieces, with_band, nf, result):
        q_sel = kept[g]["q_sel"]
        keys_of = lambda c, lo, hi: slice(front + c * far_keys + lo, front + c * far_keys + hi)
        mx = None
        if with_band:
            first_near = front - (i * tq - nf * far_keys)
            ok_s = (k_b <= q_b + front) & (k_b >= first_near)
            s_b = (_dot_nt(q_sel, ks_ref[g, pl.ds(boff, band), :]).reshape(r3, tq, band)
                   + bs_ref[g * r3:(g + 1) * r3] + jnp.where(ok_s, 0.0, NEG_BIG)[None])
            mx = lane_fold(s_b, jnp.maximum)
            yield
        for c, lo, hi in pieces:
            s = _dot_nt(q_sel, ks_ref[g, keys_of(c, lo, hi), :]).reshape(r3, tq, hi - lo)
            sc_ref[g, c, :, :, lo:hi] = s
            fold = lane_fold(s, jnp.maximum)
            mx = fold if mx is None else jnp.maximum(mx, fold)
            yield
        m = jnp.max(mx, axis=-1, keepdims=True)
        acc = None
        if with_band:
            acc = _dot(jnp.exp(s_b - m).reshape(rows, band).astype(BF16), vs_ref[g, pl.ds(boff, band), :])
            yield
        for c, lo, hi in pieces:
            p = jnp.exp(sc_ref[g, c, :, :, lo:hi] - m).reshape(rows, hi - lo).astype(BF16)
            part = _dot(p, vs_ref[g, keys_of(c, lo, hi), :])
            acc = part if acc is None else acc + part
            yield
        result.append((m.reshape(rows, 1), acc))

    def selected(nf):
        whole = lambda cs: [(c, 0, far_keys) for c in cs]
        if nf == 0:
            first, second = None, []
        elif nf % 2:
            first, second = whole(range((nf + 1) // 2)), whole(range((nf + 1) // 2, nf))
        else:
            mid = nf // 2
            first = whole(range(mid)) + [(mid, 0, far_keys // 2)]
            second = [(mid, far_keys // 2, far_keys)] + whole(range(mid + 1, nf))
        halves = [([], []) for _ in range(KV_HEADS)]
        fragments = []
        for g in range(KV_HEADS):
            fragments.append(half_softmax(g, second, True, nf, halves[g][1]))
            if first is not None:
                fragments.append(half_softmax(g, first, False, nf, halves[g][0]))
        _interleave(fragments)
        for g in range(KV_HEADS):
            (m_b, acc_b), = halves[g][1]
            if first is None:
                o_s = normalise(acc_b)
            else:
                (m_a, acc_a), = halves[g][0]
                m_s = jnp.maximum(m_a, m_b)
                o_s = normalise(acc_a * jnp.exp(m_a - m_s) + acc_b * jnp.exp(m_b - m_s))
            for r in range(r3):
                out = kept[g]["partial"][r] + kept[g]["gate_s"][r] * o_s[r]
                o_ref[:, (g * r3 + r) * HEAD_DIM:(g * r3 + r + 1) * HEAD_DIM] = out.astype(BF16)

    for nf in range(sc_ref.shape[1]):
        pl.when(n_far == nf)(functools.partial(selected, nf))


def _nsa_pair(p4, prep, bias_c, bias_sel, bias_win, overlap, q_norm, layer, tq=WIN_BLOCK, far_keys=512):
    _, b, s, _ = p4.shape
    kc, vc, ks, vs, kw, vw = prep
    nc = kc.shape[2]
    nq = N_HEADS_C
    n_sel = s // SEL_LEN
    n_cmp = (s - CMP_LEN) // CMP_STRIDE + 1
    band = (N_WIN_PREV + 1) * tq
    padded = ks.shape[2]
    per_batch = lambda *tail: pl.BlockSpec((None, KV_HEADS) + tail, lambda i, j: (i, 0, 0, 0))
    resident = lambda *tail: pl.BlockSpec((None, KV_HEADS) + tail, lambda i, j: (i, 0, 0, 0),
                                          pipeline_mode=pl.Buffered(1))
    fixed = lambda *shape: pl.BlockSpec(shape, lambda i, j: (0,) * len(shape), pipeline_mode=pl.Buffered(1))
    return pl.pallas_call(
        functools.partial(_nsa_pair_kernel, n_cmp=n_cmp, n_sel=n_sel),
        out_shape=jax.ShapeDtypeStruct((b, s, nq * HEAD_DIM), BF16),
        grid=(b, s // tq),
        in_specs=[pl.BlockSpec((nq, None, tq, LANES), lambda i, j: (G_QC // nq, i, j, 0)),
                  pl.BlockSpec((KV_HEADS, None, tq, LANES), lambda i, j: (G_SMALL // KV_HEADS, i, j, 0)),
                  per_batch(nc, LANES), per_batch(nc, LANES),
                  resident(padded, 2 * LANES), resident(padded, 2 * LANES),
                  resident(padded, LANES), resident(padded, 2 * LANES),
                  pl.BlockSpec((nq, tq, nc), lambda i, j: (0, j, 0)),
                  fixed(nq, tq, band), fixed(nq, tq, band), fixed(nc, LANES),
                  pl.BlockSpec((None, 1, LANES), lambda i, j: (layer, 0, 0))],
        out_specs=pl.BlockSpec((None, tq, nq * HEAD_DIM), lambda i, j: (i, j, 0)),
        scratch_shapes=[pltpu.VMEM((KV_HEADS, s // far_keys, Q_PER_KV, tq, far_keys), F32)],
        compiler_params=_cparams(("parallel", "arbitrary")),
        name="nsa",
    )(p4, p4, kc, vc, ks, vs, kw, vw, bias_c, bias_sel, bias_win, overlap, q_norm)


def _t5_bucket(dist):
    n = jnp.maximum(dist, 0)
    max_exact = RPB_BUCKETS // 2
    log_ratio = jnp.log(jnp.maximum(n, 1).astype(F32) / max_exact) / math.log(RPB_MAX_DIST / max_exact)
    large = jnp.minimum(max_exact + (log_ratio * (RPB_BUCKETS - max_exact)).astype(jnp.int32), RPB_BUCKETS - 1)
    return jnp.where(n < max_exact, n, large)


def _bias_tables(rel_bias, s, tq):
    table = rel_bias.astype(F32)
    buckets = jnp.arange(RPB_BUCKETS, dtype=jnp.int32)[:, None, None]

    def look(dist):
        onehot = (_t5_bucket(dist)[None] == buckets).astype(F32)
        return jnp.einsum("nh,nqk->hqk", table, onehot, precision=lax.Precision.HIGHEST)

    nc = s // CMP_STRIDE
    pos = jnp.arange(s, dtype=jnp.int32)
    cmp_end = jnp.arange(nc, dtype=jnp.int32) * CMP_STRIDE + (CMP_LEN - 1)
    bias_c = look(pos[:, None] - cmp_end[None, :])
    q = jnp.arange(tq, dtype=jnp.int32)[:, None]
    kb = jnp.arange((N_WIN_PREV + 1) * tq, dtype=jnp.int32)[None, :]
    bias_win = look(N_WIN_PREV * tq + q - kb)
    bias_sel = bias_win - table[RPB_BUCKETS - 1][:, None, None]
    return bias_c, bias_sel, bias_win


def _selection_constants(s, tq):
    nc = s // CMP_STRIDE
    n_cmp = (s - CMP_LEN) // CMP_STRIDE + 1
    n_sel = s // SEL_LEN
    cmp_start = np.arange(nc) * CMP_STRIDE
    sel_start = np.arange(LANES) * SEL_LEN
    overlap = ((cmp_start[:, None] < sel_start[None, :] + SEL_LEN)
               & (cmp_start[:, None] + CMP_LEN > sel_start[None, :])
               & (np.arange(nc)[:, None] < n_cmp) & (np.arange(LANES)[None, :] < n_sel))
    key_pos = np.arange(-WINDOW, s)[:, None]
    in_block = (np.arange(LANES)[None, :] == (key_pos // SEL_LEN)) & (key_pos >= 0)
    block_mask = jnp.where(jnp.asarray(in_block), NEG_BIG, 0.0).astype(BF16)
    ones_col = jnp.asarray((np.arange(LANES)[None, :] == 0) & (key_pos >= 0), BF16)
    return jnp.asarray(overlap, BF16), block_mask, ones_col


def _arrange_w_in(w_in):
    depth, d, _ = w_in.shape
    w_in = w_in.astype(BF16)
    da, db, dc, dkv = N_HEADS_A * 128, N_HEADS_B * 128, N_HEADS_C * 128, KV_HEADS * 128
    o_ba = 4 * da
    o_aa = o_ba + N_HEADS_A
    o_ub = o_aa + N_HEADS_A
    o_qc = o_ub + 2 * db
    o_kc = o_qc + dc
    o_gc = o_kc + 6 * dkv
    sl = lambda lo, n: w_in[:, :, lo:lo + n]
    gate = w_in[:, :, o_gc:o_gc + 3 * N_HEADS_C].reshape(depth, d, 3, KV_HEADS, Q_PER_KV)
    zeros = lambda n: jnp.zeros((depth, d, n), w_in.dtype)
    small0 = jnp.concatenate([sl(o_ba, 2 * N_HEADS_A), gate[:, :, :, 0, :].reshape(depth, d, 3 * Q_PER_KV),
                              zeros(LANES - 2 * N_HEADS_A - 3 * Q_PER_KV)], axis=-1)
    small1 = jnp.concatenate([zeros(LANE_GATE), gate[:, :, :, 1, :].reshape(depth, d, 3 * Q_PER_KV),
                              zeros(LANES - LANE_GATE - 3 * Q_PER_KV)], axis=-1)
    parts = [sl(0, 4 * da),
             sl(o_qc, dc),
             sl(o_kc, dkv),
             sl(o_ub, 2 * db),
             sl(o_kc + dkv, 5 * dkv),
             small0, small1]
    return jnp.concatenate(parts, axis=-1).astype(BF16)


def kernel(x, attn_norm, w_in, conv_a, a_log, dt_bias, gdn_norm, sgu_ln_g, sgu_ln_b, sgu_w, sgu_b,
           nsa_q_norm, nsa_k_norm, cmp_pos, cmp_w1, cmp_w2, rel_bias, w_out, mlp_norm, w_up, w_down):
    b, s, d = x.shape
    depth = w_in.shape[0]
    t = b * s
    tq = 128

    w_in_r = _arrange_w_in(w_in)
    w_out_b = w_out.astype(BF16)
    w_up_b = w_up.astype(BF16)
    w_down_b = w_down.astype(BF16)
    attn_g = attn_norm.reshape(depth, 1, d)
    mlp_g = mlp_norm.reshape(depth, 1, d)
    conv_r = conv_a.reshape(depth, CONV_WIDTH * 3 * N_HEADS_A, LANES)
    pad_to = lambda v, lo: jnp.pad(v, ((0, 0), (lo, LANES - lo - v.shape[1])))
    ab = jnp.stack([pad_to(a_log, LANE_DECAY), pad_to(dt_bias, LANE_DECAY)], axis=1)
    gdn_g = gdn_norm.reshape(depth, 1, LANES)
    ln_g = sgu_ln_g.reshape(depth, N_HEADS_B, LANES)
    ln_b = sgu_ln_b.reshape(depth, N_HEADS_B, LANES)
    sgu_bias = jnp.broadcast_to(sgu_b[..., None], sgu_b.shape + (LANES,))
    q_g = nsa_q_norm.reshape(depth, 1, LANES)
    k_g = nsa_k_norm.reshape(depth, 1, LANES)
    w1_b = cmp_w1.astype(BF16)
    w2_b = cmp_w2.astype(BF16)
    pos_b = jnp.broadcast_to(cmp_pos.reshape(depth, 2, 1, CMP_LEN * LANES),
                             (depth, 2, 8, CMP_LEN * LANES)).astype(BF16)
    bias_c, bias_sel, bias_win = _bias_tables(rel_bias, s, tq)
    overlap, block_mask, ones_col = _selection_constants(s, tq)

    x2 = x.reshape(t, d)
    for layer in range(depth):
        p = _inproj(x2, attn_g, w_in_r, layer)
        p4 = p.reshape(N_GROUPS, b, s, LANES)
        mix_a = _gdn(p4, conv_r, ab, gdn_g, layer)
        mix_b = _sgu(p4, ln_g, ln_b, sgu_w, sgu_bias, layer)
        prep = _nsa_prep(p4, w1_b, w2_b, pos_b, k_g, block_mask, ones_col, layer)
        mix_c = _nsa_pair(p4, prep, bias_c, bias_sel, bias_win, overlap, q_g, layer, tq=tq)
        x2 = _outproj(x2, mix_a.reshape(t, -1), mix_b.reshape(t, -1), mix_c.reshape(t, -1), w_out_b, layer)
        x2 = _mlp(x2, mlp_g, w_up_b, w_down_b, layer)
    return x2.reshape(b, s, d)
```

```python
import functools
import math

import numpy as np
import jax
import jax.numpy as jnp
from jax import lax
from jax.experimental import pallas as pl
from jax.experimental.pallas import tpu as pltpu

F32 = jnp.float32
BF16 = jnp.bfloat16

LANES = 128
HEAD_DIM = 128
N_HEADS_A = 6
N_HEADS_B = 4
N_HEADS_C = 6
KV_HEADS = 2
Q_PER_KV = N_HEADS_C // KV_HEADS
CONV_WIDTH = 4
GDN_CHUNK = 64
SGU_CHUNK = 128
CMP_LEN = 32
CMP_STRIDE = 16
SEL_LEN = 64
SEL_TOPK = 16
WINDOW = 512
WIN_BLOCK = 128
N_WIN_PREV = WINDOW // WIN_BLOCK
RPB_BUCKETS = 32
RPB_MAX_DIST = 128
NORM_EPS = 1e-6
NEG_BIG = -1e30
SEL_FORCE = 1e9
VMEM_LIMIT = 56 * 1024 * 1024
NSA_VMEM_LIMIT = 60 * 1024 * 1024

G_QA, G_KA, G_VA, G_ZA = 0, 6, 12, 18
G_QC = 24
G_KCMP = 30
G_UB, G_VB = 32, 36
G_VCMP = 40
G_KSLC, G_VSLC, G_KWIN, G_VWIN = 42, 44, 46, 48
G_SMALL = 50
N_GROUPS = 52
LANE_BETA, LANE_DECAY, LANE_GATE = 0, 6, 12


def _dot(a, b, precision=None):
    return jnp.dot(a, b, preferred_element_type=F32, precision=precision)


def _dot_nt(a, b):
    return lax.dot_general(a, b, (((1,), (1,)), ((), ())), preferred_element_type=F32)


def _dot_tn(a, b):
    return lax.dot_general(a, b, (((0,), (0,)), ((), ())), preferred_element_type=F32)


def _rms(x, gain):
    return x * lax.rsqrt(jnp.mean(x * x, axis=-1, keepdims=True) + NORM_EPS) * gain


def _cparams(sem, vmem_limit=VMEM_LIMIT):
    return pltpu.CompilerParams(dimension_semantics=sem, vmem_limit_bytes=vmem_limit)


def _inproj_kernel(x0_ref, xn_ref, g_ref, w_ref, o_ref, h_ref, *, steps):
    i, j = pl.program_id(0), pl.program_id(1)
    tm, sublanes = xn_ref.shape[0], 8
    part = -(-tm // (steps * sublanes)) * sublanes

    @pl.when((i == 0) & (j == 0))
    def _():
        h_ref[0] = _rms(x0_ref[...], g_ref[...]).astype(BF16)

    cur = i % 2
    r = _dot(h_ref[cur], w_ref[...])
    for c in range(o_ref.shape[0]):
        o_ref[c] = r[:, c * LANES:(c + 1) * LANES]

    rows = pl.ds(pl.multiple_of(jnp.minimum(j * part, tm - part), sublanes), part)
    h_ref[1 - cur, rows, :] = _rms(xn_ref[rows, :], g_ref[...]).astype(BF16)


def _inproj(x2, gain, w, layer, tm=1024, tn=4 * LANES):
    t, d = x2.shape
    n = w.shape[-1]
    last = t // tm - 1
    return pl.pallas_call(
        functools.partial(_inproj_kernel, steps=n // tn),
        out_shape=jax.ShapeDtypeStruct((n // LANES, t, LANES), F32),
        grid=(t // tm, n // tn),
        in_specs=[pl.BlockSpec((tm, d), lambda i, j: (0, 0), pipeline_mode=pl.Buffered(1)),
                  pl.BlockSpec((tm, d), lambda i, j: (jnp.minimum(i + 1, last), 0)),
                  pl.BlockSpec((None, 1, d), lambda i, j: (layer, 0, 0)),
                  pl.BlockSpec((None, d, tn), lambda i, j: (layer, 0, j))],
        out_specs=pl.BlockSpec((tn // LANES, tm, LANES), lambda i, j: (j, i, 0)),
        scratch_shapes=[pltpu.VMEM((2, tm, d), BF16)],
        compiler_params=_cparams(("arbitrary", "arbitrary")),
        name="inproj",
    )(x2, x2, gain, w)


def _outproj_kernel(x_ref, a_ref, b_ref, c_ref, w_ref, o_ref):
    da, db = a_ref.shape[1], b_ref.shape[1]
    acc = _dot(a_ref[...], w_ref[0:da, :])
    acc += _dot(b_ref[...], w_ref[da:da + db, :])
    acc += _dot(c_ref[...], w_ref[da + db:, :])
    o_ref[...] = x_ref[...] + acc


def _outproj(x2, ma, mb, mc, w_out, layer, tm=512):
    t, d = x2.shape
    da, db, dc = ma.shape[1], mb.shape[1], mc.shape[1]
    return pl.pallas_call(
        _outproj_kernel,
        out_shape=jax.ShapeDtypeStruct((t, d), F32),
        grid=(t // tm,),
        in_specs=[pl.BlockSpec((tm, d), lambda i: (i, 0)),
                  pl.BlockSpec((tm, da), lambda i: (i, 0)),
                  pl.BlockSpec((tm, db), lambda i: (i, 0)),
                  pl.BlockSpec((tm, dc), lambda i: (i, 0)),
                  pl.BlockSpec((None, da + db + dc, d), lambda i: (layer, 0, 0))],
        out_specs=pl.BlockSpec((tm, d), lambda i: (i, 0)),
        compiler_params=_cparams(("parallel",)),
        name="outproj",
    )(x2, ma, mb, mc, w_out)


def _mlp_kernel(x0_ref, xn_ref, g_ref, wu_ref, wd_ref, o_ref, h_ref, keep_ref):
    i, k = pl.program_id(0), pl.program_id(1)
    part = xn_ref.shape[0] // pl.num_programs(1)
    cur = i % 2

    @pl.when((i == 0) & (k == 0))
    def _():
        keep_ref[0] = x0_ref[...]
        h_ref[0] = _rms(x0_ref[...], g_ref[...]).astype(BF16)

    @pl.when(k == 0)
    def _():
        o_ref[...] = keep_ref[cur]

    a = jnp.maximum(_dot(h_ref[cur], wu_ref[...]), 0.0)
    o_ref[...] += _dot((a * a).astype(BF16), wd_ref[...])

    rows = pl.ds(pl.multiple_of(k * part, part), part)
    nxt = xn_ref[rows, :]
    keep_ref[1 - cur, rows, :] = nxt
    h_ref[1 - cur, rows, :] = _rms(nxt, g_ref[...]).astype(BF16)


def _mlp(x2, gain, w_up, w_down, layer, tm=512, tf=1024):
    t, d = x2.shape
    f = w_up.shape[-1]
    last = t // tm - 1
    return pl.pallas_call(
        _mlp_kernel,
        out_shape=jax.ShapeDtypeStruct((t, d), F32),
        grid=(t // tm, f // tf),
        in_specs=[pl.BlockSpec((tm, d), lambda i, k: (0, 0), pipeline_mode=pl.Buffered(1)),
                  pl.BlockSpec((tm, d), lambda i, k: (jnp.minimum(i + 1, last), 0)),
                  pl.BlockSpec((None, 1, d), lambda i, k: (layer, 0, 0)),
                  pl.BlockSpec((None, d, tf), lambda i, k: (layer, 0, k)),
                  pl.BlockSpec((None, tf, d), lambda i, k: (layer, k, 0))],
        out_specs=pl.BlockSpec((tm, d), lambda i, k: (i, 0)),
        scratch_shapes=[pltpu.VMEM((2, tm, d), BF16), pltpu.VMEM((2, tm, d), F32)],
        compiler_params=_cparams(("arbitrary", "arbitrary")),
        name="mlp",
    )(x2, x2, gain, w_up, w_down)


def _sgu_kernel(p_ref, lg_ref, lb_ref, w_ref, b_ref, o_ref):
    ts = p_ref.shape[1]
    nb = N_HEADS_B
    v = [jax.nn.gelu(p_ref[nb + g]) for g in range(nb)]
    width = float(nb * LANES)
    mu = sum(jnp.sum(vg, axis=-1, keepdims=True) for vg in v) / width
    var = sum(jnp.sum(jnp.square(vg - mu), axis=-1, keepdims=True) for vg in v) / width
    inv = lax.rsqrt(var + NORM_EPS)
    row = lax.broadcasted_iota(jnp.int32, (SGU_CHUNK, SGU_CHUNK), 0)
    col = lax.broadcasted_iota(jnp.int32, (SGU_CHUNK, SGU_CHUNK), 1)
    causal = col <= row
    for g in range(nb):
        vn = ((v[g] - mu) * inv * lg_ref[g:g + 1, :] + lb_ref[g:g + 1, :]).astype(BF16)
        wg = jnp.where(causal, w_ref[g], 0.0).astype(BF16)
        for c in range(ts // SGU_CHUNK):
            rows = slice(c * SGU_CHUNK, (c + 1) * SGU_CHUNK)
            mixed = _dot(wg, vn[rows]) + b_ref[g]
            u = jax.nn.gelu(p_ref[g, rows, :])
            o_ref[rows, g * LANES:(g + 1) * LANES] = (u * mixed).astype(BF16)


def _sgu(p4, ln_g, ln_b, w_s, b_s, layer, ts=512):
    _, b, s, _ = p4.shape
    nb = N_HEADS_B
    return pl.pallas_call(
        _sgu_kernel,
        out_shape=jax.ShapeDtypeStruct((b, s, nb * LANES), BF16),
        grid=(b, s // ts),
        in_specs=[pl.BlockSpec((2 * nb, None, ts, LANES), lambda i, j: (G_UB // (2 * nb), i, j, 0)),
                  pl.BlockSpec((None, nb, LANES), lambda i, j: (layer, 0, 0)),
                  pl.BlockSpec((None, nb, LANES), lambda i, j: (layer, 0, 0)),
                  pl.BlockSpec((None, nb, SGU_CHUNK, SGU_CHUNK), lambda i, j: (layer, 0, 0, 0)),
                  pl.BlockSpec((None, nb, SGU_CHUNK, LANES), lambda i, j: (layer, 0, 0, 0))],
        out_specs=pl.BlockSpec((None, ts, nb * LANES), lambda i, j: (i, j, 0)),
        compiler_params=_cparams(("parallel", "parallel")),
        name="sgu",
    )(p4, ln_g, ln_b, w_s, b_s)


def _gdn_kernel(a_ref, sm_ref, cw_ref, ab_ref, ng_ref, o_ref,
                xbuf, q_s, k_s, v_s, beta_s, gcum_s, u_s, w_s, qg_s, kd_s, a_s, st_ref, *, batch):
    ts = a_ref.shape[1]
    ck = GDN_CHUNK
    nh = N_HEADS_A
    pad = 8

    @pl.when(pl.program_id(1) == 0)
    def _():
        xbuf[:, 0:pad, :] = jnp.zeros((3 * nh, pad, LANES), F32)
        st_ref[...] = jnp.zeros_like(st_ref)

    xbuf[:, pad:pad + ts, :] = a_ref[0:3 * nh]
    dests = (q_s, k_s, v_s)
    for grp in range(3 * nh):
        acc = None
        for j in range(CONV_WIDTH):
            lo = pad - (CONV_WIDTH - 1) + j
            term = xbuf[grp, lo:lo + ts, :] * cw_ref[j * 3 * nh + grp:j * 3 * nh + grp + 1, :]
            acc = term if acc is None else acc + term
        y = acc * jax.nn.sigmoid(acc)
        kind, h = divmod(grp, nh)
        if kind < 2:
            y = y * lax.rsqrt(jnp.sum(y * y, axis=-1, keepdims=True) + NORM_EPS)
        dests[kind][h] = y
    xbuf[:, 0:pad, :] = xbuf[:, ts:ts + pad, :]

    sm = sm_ref[...]
    beta_s[...] = jax.nn.sigmoid(sm)
    z = sm + ab_ref[1:2, :]
    softplus = jnp.maximum(z, 0.0) + jnp.log1p(jnp.exp(-jnp.abs(z)))
    g_all = -jnp.exp(ab_ref[0:1, :]) * softplus

    ii = lax.broadcasted_iota(jnp.int32, (ck, ck), 0)
    jj = lax.broadcasted_iota(jnp.int32, (ck, ck), 1)
    incl = jj <= ii
    strict = jj < ii
    tri = jnp.where(incl, 1.0, 0.0).astype(F32)
    for c in range(ts // ck):
        rows = slice(c * ck, (c + 1) * ck)
        gcum_s[rows, :] = _dot(tri, g_all[rows], precision=lax.Precision.HIGHEST)

    scale = HEAD_DIM ** -0.5
    gain = ng_ref[...]
    bf = lambda m: m.astype(BF16)
    same = lambda size: (ii // size) == (jj // size)
    n_chunks = ts // ck
    rows_of = lambda c: slice(c * ck, (c + 1) * ck)
    dcol = lambda h: slice(LANE_DECAY + h, LANE_DECAY + h + 1)

    for c0 in range(0, n_chunks, batch):
        gc = {c: gcum_s[rows_of(c), :] for c in range(c0, c0 + batch)}
        gct = {c: gc[c].T for c in gc}
        items = [(c, h) for c in range(c0, c0 + batch) for h in range(nh)]
        kk, qk = {}, {}
        for c, h in items:
            g_col = gc[c][:, dcol(h)]
            q = q_s[h, rows_of(c), :] * scale
            k = k_s[h, rows_of(c), :]
            kbf = bf(k)
            kk[c, h] = _dot_nt(bf(k * beta_s[rows_of(c), LANE_BETA + h:LANE_BETA + h + 1]), kbf)
            qk[c, h] = _dot_nt(bf(q), kbf)
            qg_s[h, rows_of(c), :] = bf(q * jnp.exp(g_col))
            kd_s[h, rows_of(c), :] = bf(k * jnp.exp(gc[c][ck - 1:ck, dcol(h)] - g_col))
        lower, diag, p1 = {}, {}, {}
        for it in items:
            c, h = it
            decay = jnp.exp(jnp.where(incl, gc[c][:, dcol(h)] - gct[c][dcol(h), :], NEG_BIG))
            lower[it] = jnp.where(strict, kk[it] * decay, 0.0)
            a_s[h, rows_of(c), :] = bf(jnp.where(incl, qk[it] * decay, 0.0))
            diag[it] = jnp.where(same(8), lower[it], 0.0)
            p1[it] = _dot(bf(diag[it]), bf(diag[it]))
        p2 = {it: _dot(bf(p1[it]), bf(p1[it])) for it in items}
        dp1 = {it: _dot(bf(diag[it]), bf(p1[it])) for it in items}
        na = {it: p1[it] - diag[it] - dp1[it] for it in items}
        nap2 = {it: _dot(bf(na[it]), bf(p2[it])) for it in items}
        n = {it: na[it] + p2[it] + nap2[it] for it in items}
        size = 8
        while size < ck:
            off = same(2 * size) & ~same(size)
            cm = {it: jnp.where(off, lower[it], 0.0) for it in items}
            y = {it: cm[it] + _dot(bf(n[it]), bf(cm[it])) for it in items}
            yn = {it: _dot(bf(y[it]), bf(n[it])) for it in items}
            n = {it: n[it] - (y[it] + yn[it]) for it in items}
            size *= 2
        for it in items:
            c, h = it
            beta = beta_s[rows_of(c), LANE_BETA + h:LANE_BETA + h + 1]
            nb = bf(n[it])
            vb = v_s[h, rows_of(c), :] * beta
            kw = k_s[h, rows_of(c), :] * (beta * jnp.exp(gc[c][:, dcol(h)]))
            u_s[h, rows_of(c), :] = vb + _dot(nb, bf(vb))
            w_s[h, rows_of(c), :] = bf(kw + _dot(nb, bf(kw)))

    heads = range(nh)
    for c in range(n_chunks):
        rows = rows_of(c)
        state = [st_ref[h] for h in heads]
        sb = [bf(s) for s in state]
        ws = [_dot(w_s[h, rows, :], sb[h]) for h in heads]
        qs = [_dot(qg_s[h, rows, :], sb[h]) for h in heads]
        vnb = [bf(u_s[h, rows, :] - ws[h]) for h in heads]
        kv = [_dot_tn(kd_s[h, rows, :], vnb[h]) for h in heads]
        av = [_dot(a_s[h, rows, :], vnb[h]) for h in heads]
        for h in heads:
            g_last = gcum_s[(c + 1) * ck - 1:(c + 1) * ck, dcol(h)]
            st_ref[h] = state[h] * jnp.exp(g_last) + kv[h]
            zg = a_ref[3 * nh + h, rows, :]
            out = _rms(qs[h] + av[h], gain) * (zg * jax.nn.sigmoid(zg))
            o_ref[rows, h * LANES:(h + 1) * LANES] = out.astype(BF16)


def _gdn(p4, conv_w, ab, norm_g, layer, ts=512, batch=4):
    _, b, s, _ = p4.shape
    nh = N_HEADS_A
    return pl.pallas_call(
        functools.partial(_gdn_kernel, batch=batch),
        out_shape=jax.ShapeDtypeStruct((b, s, nh * LANES), BF16),
        grid=(b, s // ts),
        in_specs=[pl.BlockSpec((4 * nh, None, ts, LANES), lambda i, j: (0, i, j, 0)),
                  pl.BlockSpec((None, None, ts, LANES), lambda i, j: (G_SMALL, i, j, 0)),
                  pl.BlockSpec((None, CONV_WIDTH * 3 * nh, LANES), lambda i, j: (layer, 0, 0)),
                  pl.BlockSpec((None, 2, LANES), lambda i, j: (layer, 0, 0)),
                  pl.BlockSpec((None, 1, LANES), lambda i, j: (layer, 0, 0))],
        out_specs=pl.BlockSpec((None, ts, nh * LANES), lambda i, j: (i, j, 0)),
        scratch_shapes=[pltpu.VMEM((3 * nh, ts + 8, LANES), F32),
                        pltpu.VMEM((nh, ts, LANES), F32),
                        pltpu.VMEM((nh, ts, LANES), F32),
                        pltpu.VMEM((nh, ts, LANES), F32),
                        pltpu.VMEM((ts, LANES), F32),
                        pltpu.VMEM((ts, LANES), F32),
                        pltpu.VMEM((nh, ts, HEAD_DIM), F32),
                        pltpu.VMEM((nh, ts, HEAD_DIM), BF16),
                        pltpu.VMEM((nh, ts, HEAD_DIM), BF16),
                        pltpu.VMEM((nh, ts, HEAD_DIM), BF16),
                        pltpu.VMEM((nh, ts, GDN_CHUNK), BF16),
                        pltpu.VMEM((nh, HEAD_DIM, HEAD_DIM), F32)],
        compiler_params=_cparams(("parallel", "arbitrary")),
        name="gdn",
    )(p4, p4, conv_w, ab, norm_g)


def _nsa_prep_kernel(kc2_ref, vc2_ref, ksl_ref, vsl_ref, kwn_ref, vwn_ref,
                     w1_ref, w2_ref, pos_ref, kg_ref, bm_ref, one_ref,
                     kc_o, vc_o, ks_o, vs_o, kw_o, vw_o):
    nc = kc_o.shape[0]
    half = CMP_STRIDE * LANES
    kg = kg_ref[...]
    for kind, (src, dst) in enumerate(((kc2_ref, kc_o), (vc2_ref, vc_o))):
        t2 = jnp.concatenate([src[pl.ds(l, nc, stride=CMP_STRIDE), :].astype(BF16)
                              for l in range(CMP_STRIDE)], axis=1)
        first = _dot(t2, w1_ref[kind, 0:half, :])
        second = _dot(t2, w1_ref[kind, half:2 * half, :])
        const = _dot(pos_ref[kind], w1_ref[kind])[0:1, :]
        pre = first + pltpu.roll(second, nc - 1, 0) + const
        y = _dot(jax.nn.gelu(pre).astype(BF16), w2_ref[kind])
        if kind == 0:
            y = _rms(y, kg)
        dst[...] = y.astype(BF16)
    front = ks_o.shape[0] - ksl_ref.shape[0]
    for dst, val in ((ks_o, _rms(ksl_ref[...], kg)), (kw_o, _rms(kwn_ref[...], kg)),
                     (vs_o, vsl_ref[...]), (vw_o, vwn_ref[...])):
        dst[0:front, 0:LANES] = jnp.zeros((front, LANES), BF16)
        dst[front:, 0:LANES] = val.astype(BF16)
    ks_o[:, LANES:] = bm_ref[...]
    vs_o[:, LANES:] = one_ref[...]
    vw_o[:, LANES:] = one_ref[...]


def _nsa_prep(p4, w1, w2, pos, k_norm, block_mask, ones_col, layer):
    ng, b, s, _ = p4.shape
    padded = s + WINDOW
    nc = s // CMP_STRIDE
    half = CMP_STRIDE * LANES
    grp = lambda base: (lambda i, g: (base + g, i, 0, 0))
    small = jax.ShapeDtypeStruct((b, KV_HEADS, nc, LANES), BF16)
    full = jax.ShapeDtypeStruct((b, KV_HEADS, padded, LANES), BF16)
    wide = jax.ShapeDtypeStruct((b, KV_HEADS, padded, 2 * LANES), BF16)
    out_small = pl.BlockSpec((None, None, nc, LANES), lambda i, g: (i, g, 0, 0))
    out_full = pl.BlockSpec((None, None, padded, LANES), lambda i, g: (i, g, 0, 0))
    out_wide = pl.BlockSpec((None, None, padded, 2 * LANES), lambda i, g: (i, g, 0, 0))
    const = pl.BlockSpec((padded, LANES), lambda i, g: (0, 0))
    return pl.pallas_call(
        _nsa_prep_kernel,
        out_shape=(small, small, wide, wide, full, wide),
        grid=(b, KV_HEADS),
        in_specs=[pl.BlockSpec((None, None, s, LANES), grp(G_KCMP)),
                  pl.BlockSpec((None, None, s, LANES), grp(G_VCMP)),
                  pl.BlockSpec((None, None, s, LANES), grp(G_KSLC)),
                  pl.BlockSpec((None, None, s, LANES), grp(G_VSLC)),
                  pl.BlockSpec((None, None, s, LANES), grp(G_KWIN)),
                  pl.BlockSpec((None, None, s, LANES), grp(G_VWIN)),
                  pl.BlockSpec((None, 2, 2 * half, LANES), lambda i, g: (layer, 0, 0, 0)),
                  pl.BlockSpec((None, 2, LANES, LANES), lambda i, g: (layer, 0, 0, 0)),
                  pl.BlockSpec((None, 2, 8, 2 * half), lambda i, g: (layer, 0, 0, 0)),
                  pl.BlockSpec((None, 1, LANES), lambda i, g: (layer, 0, 0)),
                  const, const],
        out_specs=(out_small, out_small, out_wide, out_wide, out_full, out_wide),
        compiler_params=_cparams(("parallel", "parallel")),
        name="nsa_prep",
    )(p4, p4, p4, p4, p4, p4, w1, w2, pos, k_norm, block_mask, ones_col)


def _interleave(fragments):
    live = list(fragments)
    while live:
        for fragment in list(live):
            try:
                next(fragment)
            except StopIteration:
                live.remove(fragment)


def _nsa_pair_kernel(q_ref, gate_ref, kc_ref, vc_ref, ks_ref, vs_ref, kw_ref, vw_ref,
                     bc_ref, bs_ref, bw_ref, ov_ref, qg_ref, o_ref, sc_ref, *, n_cmp, n_sel):
    tq = q_ref.shape[1]
    r3 = Q_PER_KV
    i = pl.program_id(1)
    nc = kc_ref.shape[1]
    rows = r3 * tq
    band = (N_WIN_PREV + 1) * tq
    front = N_WIN_PREV * tq
    far_keys = sc_ref.shape[-1]
    boff = pl.multiple_of(i * tq, tq)
    n_far = jnp.maximum(i - 1, 0) // (far_keys // tq)

    qq = lax.broadcasted_iota(jnp.int32, (tq, tq), 0)
    kk = lax.broadcasted_iota(jnp.int32, (tq, tq), 1)
    q_b = lax.broadcasted_iota(jnp.int32, (tq, band), 0)
    k_b = lax.broadcasted_iota(jnp.int32, (tq, band), 1)
    qpos_c = i * tq + lax.broadcasted_iota(jnp.int32, (tq, nc), 0)
    n_idx = lax.broadcasted_iota(jnp.int32, (tq, nc), 1)
    mask_c = ((qpos_c >= n_idx * CMP_STRIDE + (CMP_LEN - 1)) & (n_idx < n_cmp))[None]
    ok_w = (k_b > q_b) & (k_b <= q_b + front) & (k_b >= front - i * tq)
    cur = (i * tq + qq) // SEL_LEN
    forced = (kk == 0) | (kk == cur) | (kk == cur - 1)
    qg = qg_ref[...]

    def lane_fold(x, op):
        out = x[..., 0:tq]
        for c in range(1, x.shape[-1] // tq):
            out = op(out, x[..., c * tq:(c + 1) * tq])
        return out

    def normalise(acc):
        den = jnp.maximum(acc[:, HEAD_DIM:HEAD_DIM + 1], 1e-30)
        return (acc[:, 0:HEAD_DIM] / den).reshape(r3, tq, HEAD_DIM)

    kept = [{} for _ in range(KV_HEADS)]

    def before_selection(g):
        n_blocks = n_sel
        heads = slice(g * r3, (g + 1) * r3)
        qf = jnp.concatenate([_rms(q_ref[g * r3 + r], qg) * (HEAD_DIM ** -0.5) for r in range(r3)], axis=0)
        qb = qf.astype(BF16)
        yield
        s_c = _dot_nt(qb, kc_ref[g]).reshape(r3, tq, nc) + bc_ref[heads]
        s_w = (_dot_nt(qb, kw_ref[g, pl.ds(boff, band), :]).reshape(r3, tq, band) + bw_ref[heads]
               + jnp.where(ok_w, 0.0, NEG_BIG)[None])
        yield
        s_c = jnp.where(mask_c, s_c, NEG_BIG)
        m_c = jnp.max(s_c, axis=-1, keepdims=True)
        p_c = jnp.where(mask_c, jnp.exp(s_c - m_c), 0.0)
        p_c = p_c / jnp.maximum(jnp.sum(p_c, axis=-1, keepdims=True), 1e-30)
        yield
        o_c = _dot(p_c.reshape(rows, nc).astype(BF16), vc_ref[g]).reshape(r3, tq, HEAD_DIM)
        p_sum = p_c[0] + p_c[1] + p_c[2]
        p_hi = p_sum.astype(BF16)
        p_lo = (p_sum - p_hi.astype(F32)).astype(BF16)
        importance = _dot(p_hi, ov_ref[...]) + _dot(p_lo, ov_ref[...])
        yield
        p_w = jnp.exp(s_w - jnp.max(s_w, axis=-1, keepdims=True))
        o_w = normalise(_dot(p_w.reshape(rows, band).astype(BF16), vw_ref[g, pl.ds(boff, band), :]))
        yield
        score = jnp.where(forced, SEL_FORCE, jnp.where(kk <= cur, importance, NEG_BIG))
        st = score.T
        sub = 8
        slabs = [st[v * sub:(v + 1) * sub, :] for v in range((n_blocks + sub - 1) // sub)]
        keep = [slab > 0.5 * NEG_BIG for slab in slabs]
        if n_blocks > SEL_TOPK:
            blk = lax.broadcasted_iota(jnp.int32, (sub, tq), 0)
            ranks = [jnp.zeros((sub, tq), F32) for _ in slabs]
            for b in range(n_blocks):
                row = st[b:b + 1, :]
                for v, slab in enumerate(slabs):
                    if v * sub > b:
                        beats = row >= slab
                    elif (v + 1) * sub - 1 <= b:
                        beats = row > slab
                    else:
                        beats = (row > slab) | ((row == slab) & (blk + v * sub > b))
                    ranks[v] = ranks[v] + jnp.where(beats, 1.0, 0.0)
                if b % 8 == 7:
                    yield
            keep = [k & (r < float(SEL_TOPK)) for k, r in zip(keep, ranks)]
        drop_t = jnp.concatenate([jnp.where(k, 0.0, 1.0) for k in keep]
                                 + [jnp.ones((tq - sub * len(slabs), tq), F32)], axis=0)
        drop = drop_t.T.astype(BF16)
        q_sel = jnp.concatenate([qb, jnp.concatenate([drop] * r3, axis=0)], axis=1)
        gates = jax.nn.sigmoid(gate_ref[g])
        col = lambda branch, r: gates[:, LANE_GATE + branch * r3 + r:LANE_GATE + branch * r3 + r + 1]
        kept[g].update(q_sel=q_sel, gate_s=[col(1, r) for r in range(r3)],
                       partial=[col(0, r) * o_c[r] + col(2, r) * o_w[r] for r in range(r3)])


    def half_softmax(g, pieces, with_band, nf, result):
        q_sel = kept[g]["q_sel"]
        keys_of = lambda c, lo, hi: slice(front + c * far_keys + lo, front + c * far_keys + hi)
        mx = None
        if with_band:
            first_near = front - (i * tq - nf * far_keys)
            ok_s = (k_b <= q_b + front) & (k_b >= first_near)
            s_b = (_dot_nt(q_sel, ks_ref[g, pl.ds(boff, band), :]).reshape(r3, tq, band)
                   + bs_ref[g * r3:(g + 1) * r3] + jnp.where(ok_s, 0.0, NEG_BIG)[None])
            mx = lane_fold(s_b, jnp.maximum)
            yield
        for c, lo, hi in pieces:
            s = _dot_nt(q_sel, ks_ref[g, keys_of(c, lo, hi), :]).reshape(r3, tq, hi - lo)
            sc_ref[g, c, :, :, lo:hi] = s
            fold = lane_fold(s, jnp.maximum)
            mx = fold if mx is None else jnp.maximum(mx, fold)
            yield
        m = jnp.max(mx, axis=-1, keepdims=True)
        acc = None
        if with_band:
            acc = _dot(jnp.exp(s_b - m).reshape(rows, band).astype(BF16), vs_ref[g, pl.ds(boff, band), :])
            yield
        for c, lo, hi in pieces:
            p = jnp.exp(sc_ref[g, c, :, :, lo:hi] - m).reshape(rows, hi - lo).astype(BF16)
            part = _dot(p, vs_ref[g, keys_of(c, lo, hi), :])
            acc = part if acc is None else acc + part
            yield
        result.append((m.reshape(rows, 1), acc))

    def selected(nf):
        whole = lambda cs: [(c, 0, far_keys) for c in cs]
        if nf == 0:
            first, second = None, []
        elif nf % 2:
            first, second = whole(range((nf + 1) // 2)), whole(range((nf + 1) // 2, nf))
        else:
            mid = nf // 2
            first = whole(range(mid)) + [(mid, 0, far_keys // 2)]
            second = [(mid, far_keys // 2, far_keys)] + whole(range(mid + 1, nf))
        halves = [([], []) for _ in range(KV_HEADS)]
        fragments = []
        for g in range(KV_HEADS):
            fragments.append(half_softmax(g, second, True, nf, halves[g][1]))
            if first is not None:
                fragments.append(half_softmax(g, first, False, nf, halves[g][0]))
        _interleave(fragments)
        for g in range(KV_HEADS):
            (m_b, acc_b), = halves[g][1]
            if first is None:
                o_s = normalise(acc_b)
            else:
                (m_a, acc_a), = halves[g][0]
                m_s = jnp.maximum(m_a, m_b)
                o_s = normalise(acc_a * jnp.exp(m_a - m_s) + acc_b * jnp.exp(m_b - m_s))
            for r in range(r3):
                out = kept[g]["partial"][r] + kept[g]["gate_s"][r] * o_s[r]
                o_ref[:, (g * r3 + r) * HEAD_DIM:(g * r3 + r + 1) * HEAD_DIM] = out.astype(BF16)

    _interleave(before_selection(g) for g in range(KV_HEADS))
    for nf in range(sc_ref.shape[1]):
        pl.when(n_far == nf)(functools.partial(selected, nf))


def _nsa_pair(p4, prep, bias_c, bias_sel, bias_win, overlap, q_norm, layer, tq=WIN_BLOCK, far_keys=512):
    _, b, s, _ = p4.shape
    kc, vc, ks, vs, kw, vw = prep
    nc = kc.shape[2]
    nq = N_HEADS_C
    n_sel = s // SEL_LEN
    n_cmp = (s - CMP_LEN) // CMP_STRIDE + 1
    band = (N_WIN_PREV + 1) * tq
    padded = ks.shape[2]
    per_batch = lambda *tail: pl.BlockSpec((None, KV_HEADS) + tail, lambda i, j: (i, 0, 0, 0))
    resident = lambda *tail: pl.BlockSpec((None, KV_HEADS) + tail, lambda i, j: (i, 0, 0, 0),
                                          pipeline_mode=pl.Buffered(1))
    fixed = lambda *shape: pl.BlockSpec(shape, lambda i, j: (0,) * len(shape), pipeline_mode=pl.Buffered(1))
    return pl.pallas_call(
        functools.partial(_nsa_pair_kernel, n_cmp=n_cmp, n_sel=n_sel),
        out_shape=jax.ShapeDtypeStruct((b, s, nq * HEAD_DIM), BF16),
        grid=(b, s // tq),
        in_specs=[pl.BlockSpec((nq, None, tq, LANES), lambda i, j: (G_QC // nq, i, j, 0)),
                  pl.BlockSpec((KV_HEADS, None, tq, LANES), lambda i, j: (G_SMALL // KV_HEADS, i, j, 0)),
                  per_batch(nc, LANES), per_batch(nc, LANES),
                  per_batch(padded, 2 * LANES), per_batch(padded, 2 * LANES),
                  per_batch(padded, LANES), per_batch(padded, 2 * LANES),
                  pl.BlockSpec((nq, tq, nc), lambda i, j: (0, j, 0)),
                  fixed(nq, tq, band), fixed(nq, tq, band), fixed(nc, LANES),
                  pl.BlockSpec((None, 1, LANES), lambda i, j: (layer, 0, 0))],
        out_specs=pl.BlockSpec((None, tq, nq * HEAD_DIM), lambda i, j: (i, j, 0)),
        scratch_shapes=[pltpu.VMEM((KV_HEADS, s // far_keys, Q_PER_KV, tq, far_keys), F32)],
        compiler_params=_cparams(("parallel", "arbitrary"), vmem_limit=NSA_VMEM_LIMIT),
        name="nsa",
    )(p4, p4, kc, vc, ks, vs, kw, vw, bias_c, bias_sel, bias_win, overlap, q_norm)


def _t5_bucket(dist):
    n = jnp.maximum(dist, 0)
    max_exact = RPB_BUCKETS // 2
    log_ratio = jnp.log(jnp.maximum(n, 1).astype(F32) / max_exact) / math.log(RPB_MAX_DIST / max_exact)
    large = jnp.minimum(max_exact + (log_ratio * (RPB_BUCKETS - max_exact)).astype(jnp.int32), RPB_BUCKETS - 1)
    return jnp.where(n < max_exact, n, large)


def _bias_tables(rel_bias, s, tq):
    table = rel_bias.astype(F32)
    buckets = jnp.arange(RPB_BUCKETS, dtype=jnp.int32)[:, None, None]

    def look(dist):
        onehot = (_t5_bucket(dist)[None] == buckets).astype(F32)
        return jnp.einsum("nh,nqk->hqk", table, onehot, precision=lax.Precision.HIGHEST)

    nc = s // CMP_STRIDE
    pos = jnp.arange(s, dtype=jnp.int32)
    cmp_end = jnp.arange(nc, dtype=jnp.int32) * CMP_STRIDE + (CMP_LEN - 1)
    bias_c = look(pos[:, None] - cmp_end[None, :])
    q = jnp.arange(tq, dtype=jnp.int32)[:, None]
    kb = jnp.arange((N_WIN_PREV + 1) * tq, dtype=jnp.int32)[None, :]
    bias_win = look(N_WIN_PREV * tq + q - kb)
    bias_sel = bias_win - table[RPB_BUCKETS - 1][:, None, None]
    return bias_c, bias_sel, bias_win


def _selection_constants(s, tq):
    nc = s // CMP_STRIDE
    n_cmp = (s - CMP_LEN) // CMP_STRIDE + 1
    n_sel = s // SEL_LEN
    cmp_start = np.arange(nc) * CMP_STRIDE
    sel_start = np.arange(LANES) * SEL_LEN
    overlap = ((cmp_start[:, None] < sel_start[None, :] + SEL_LEN)
               & (cmp_start[:, None] + CMP_LEN > sel_start[None, :])
               & (np.arange(nc)[:, None] < n_cmp) & (np.arange(LANES)[None, :] < n_sel))
    key_pos = np.arange(-WINDOW, s)[:, None]
    in_block = (np.arange(LANES)[None, :] == (key_pos // SEL_LEN)) & (key_pos >= 0)
    block_mask = jnp.where(jnp.asarray(in_block), NEG_BIG, 0.0).astype(BF16)
    ones_col = jnp.asarray((np.arange(LANES)[None, :] == 0) & (key_pos >= 0), BF16)
    return jnp.asarray(overlap, BF16), block_mask, ones_col


def _arrange_w_in(w_in):
    depth, d, _ = w_in.shape
    w_in = w_in.astype(BF16)
    da, db, dc, dkv = N_HEADS_A * 128, N_HEADS_B * 128, N_HEADS_C * 128, KV_HEADS * 128
    o_ba = 4 * da
    o_aa = o_ba + N_HEADS_A
    o_ub = o_aa + N_HEADS_A
    o_qc = o_ub + 2 * db
    o_kc = o_qc + dc
    o_gc = o_kc + 6 * dkv
    sl = lambda lo, n: w_in[:, :, lo:lo + n]
    gate = w_in[:, :, o_gc:o_gc + 3 * N_HEADS_C].reshape(depth, d, 3, KV_HEADS, Q_PER_KV)
    zeros = lambda n: jnp.zeros((depth, d, n), w_in.dtype)
    small0 = jnp.concatenate([sl(o_ba, 2 * N_HEADS_A), gate[:, :, :, 0, :].reshape(depth, d, 3 * Q_PER_KV),
                              zeros(LANES - 2 * N_HEADS_A - 3 * Q_PER_KV)], axis=-1)
    small1 = jnp.concatenate([zeros(LANE_GATE), gate[:, :, :, 1, :].reshape(depth, d, 3 * Q_PER_KV),
                              zeros(LANES - LANE_GATE - 3 * Q_PER_KV)], axis=-1)
    parts = [sl(0, 4 * da),
             sl(o_qc, dc),
             sl(o_kc, dkv),
             sl(o_ub, 2 * db),
             sl(o_kc + dkv, 5 * dkv),
             small0, small1]
    return jnp.concatenate(parts, axis=-1).astype(BF16)


def kernel(x, attn_norm, w_in, conv_a, a_log, dt_bias, gdn_norm, sgu_ln_g, sgu_ln_b, sgu_w, sgu_b,
           nsa_q_norm, nsa_k_norm, cmp_pos, cmp_w1, cmp_w2, rel_bias, w_out, mlp_norm, w_up, w_down):
    b, s, d = x.shape
    depth = w_in.shape[0]
    t = b * s
    tq = 128

    w_in_r = _arrange_w_in(w_in)
    w_out_b = w_out.astype(BF16)
    w_up_b = w_up.astype(BF16)
    w_down_b = w_down.astype(BF16)
    attn_g = attn_norm.reshape(depth, 1, d)
    mlp_g = mlp_norm.reshape(depth, 1, d)
    conv_r = conv_a.reshape(depth, CONV_WIDTH * 3 * N_HEADS_A, LANES)
    pad_to = lambda v, lo: jnp.pad(v, ((0, 0), (lo, LANES - lo - v.shape[1])))
    ab = jnp.stack([pad_to(a_log, LANE_DECAY), pad_to(dt_bias, LANE_DECAY)], axis=1)
    gdn_g = gdn_norm.reshape(depth, 1, LANES)
    ln_g = sgu_ln_g.reshape(depth, N_HEADS_B, LANES)
    ln_b = sgu_ln_b.reshape(depth, N_HEADS_B, LANES)
    sgu_bias = jnp.broadcast_to(sgu_b[..., None], sgu_b.shape + (LANES,))
    q_g = nsa_q_norm.reshape(depth, 1, LANES)
    k_g = nsa_k_norm.reshape(depth, 1, LANES)
    w1_b = cmp_w1.astype(BF16)
    w2_b = cmp_w2.astype(BF16)
    pos_b = jnp.broadcast_to(cmp_pos.reshape(depth, 2, 1, CMP_LEN * LANES),
                             (depth, 2, 8, CMP_LEN * LANES)).astype(BF16)
    bias_c, bias_sel, bias_win = _bias_tables(rel_bias, s, tq)
    overlap, block_mask, ones_col = _selection_constants(s, tq)

    x2 = x.reshape(t, d)
    for layer in range(depth):
        p = _inproj(x2, attn_g, w_in_r, layer)
        p4 = p.reshape(N_GROUPS, b, s, LANES)
        mix_a = _gdn(p4, conv_r, ab, gdn_g, layer)
        mix_b = _sgu(p4, ln_g, ln_b, sgu_w, sgu_bias, layer)
        prep = _nsa_prep(p4, w1_b, w2_b, pos_b, k_g, block_mask, ones_col, layer)
        mix_c = _nsa_pair(p4, prep, bias_c, bias_sel, bias_win, overlap, q_g, layer, tq=tq)
        x2 = _outproj(x2, mix_a.reshape(t, -1), mix_b.reshape(t, -1), mix_c.reshape(t, -1), w_out_b, layer)
        x2 = _mlp(x2, mlp_g, w_up_b, w_down_b, layer)
    return x2.reshape(b, s, d)
```

```python
import functools
import math

import numpy as np
import jax
import jax.numpy as jnp
from jax import lax
from jax.experimental import pallas as pl
from jax.experimental.pallas import tpu as pltpu

F32 = jnp.float32
BF16 = jnp.bfloat16

LANES = 128
HEAD_DIM = 128
N_HEADS_A = 6
N_HEADS_B = 4
N_HEADS_C = 6
KV_HEADS = 2
Q_PER_KV = N_HEADS_C // KV_HEADS
CONV_WIDTH = 4
GDN_CHUNK = 64
SGU_CHUNK = 128
CMP_LEN = 32
CMP_STRIDE = 16
SEL_LEN = 64
SEL_TOPK = 16
WINDOW = 512
WIN_BLOCK = 128
N_WIN_PREV = WINDOW // WIN_BLOCK
RPB_BUCKETS = 32
RPB_MAX_DIST = 128
NORM_EPS = 1e-6
NEG_BIG = -1e30
SEL_FORCE = 1e9
VMEM_LIMIT = 56 * 1024 * 1024
NSA_VMEM_LIMIT = 60 * 1024 * 1024

G_QA, G_KA, G_VA, G_ZA = 0, 6, 12, 18
G_QC = 24
G_KCMP = 30
G_UB, G_VB = 32, 36
G_VCMP = 40
G_KSLC, G_VSLC, G_KWIN, G_VWIN = 42, 44, 46, 48
G_SMALL = 50
N_GROUPS = 52
LANE_BETA, LANE_DECAY, LANE_GATE = 0, 6, 12


def _dot(a, b, precision=None):
    return jnp.dot(a, b, preferred_element_type=F32, precision=precision)


def _dot_nt(a, b):
    return lax.dot_general(a, b, (((1,), (1,)), ((), ())), preferred_element_type=F32)


def _dot_tn(a, b):
    return lax.dot_general(a, b, (((0,), (0,)), ((), ())), preferred_element_type=F32)


def _rms(x, gain):
    return x * lax.rsqrt(jnp.mean(x * x, axis=-1, keepdims=True) + NORM_EPS) * gain


def _cparams(sem, vmem_limit=VMEM_LIMIT):
    return pltpu.CompilerParams(dimension_semantics=sem, vmem_limit_bytes=vmem_limit)


def _inproj_kernel(x0_ref, xn_ref, g_ref, w_ref, o_ref, h_ref, *, steps):
    i, j = pl.program_id(0), pl.program_id(1)
    tm, sublanes = xn_ref.shape[0], 8
    part = -(-tm // (steps * sublanes)) * sublanes

    @pl.when((i == 0) & (j == 0))
    def _():
        h_ref[0] = _rms(x0_ref[...], g_ref[...]).astype(BF16)

    cur = i % 2
    r = _dot(h_ref[cur], w_ref[...])
    for c in range(o_ref.shape[0]):
        o_ref[c] = r[:, c * LANES:(c + 1) * LANES]

    rows = pl.ds(pl.multiple_of(jnp.minimum(j * part, tm - part), sublanes), part)
    h_ref[1 - cur, rows, :] = _rms(xn_ref[rows, :], g_ref[...]).astype(BF16)


def _inproj(x2, gain, w, layer, tm=1024, tn=4 * LANES):
    t, d = x2.shape
    n = w.shape[-1]
    last = t // tm - 1
    return pl.pallas_call(
        functools.partial(_inproj_kernel, steps=n // tn),
        out_shape=jax.ShapeDtypeStruct((n // LANES, t, LANES), F32),
        grid=(t // tm, n // tn),
        in_specs=[pl.BlockSpec((tm, d), lambda i, j: (0, 0), pipeline_mode=pl.Buffered(1)),
                  pl.BlockSpec((tm, d), lambda i, j: (jnp.minimum(i + 1, last), 0)),
                  pl.BlockSpec((None, 1, d), lambda i, j: (layer, 0, 0)),
                  pl.BlockSpec((None, d, tn), lambda i, j: (layer, 0, j))],
        out_specs=pl.BlockSpec((tn // LANES, tm, LANES), lambda i, j: (j, i, 0)),
        scratch_shapes=[pltpu.VMEM((2, tm, d), BF16)],
        compiler_params=_cparams(("arbitrary", "arbitrary")),
        name="inproj",
    )(x2, x2, gain, w)


def _outproj_kernel(x_ref, a_ref, b_ref, c_ref, w_ref, o_ref):
    da, db = a_ref.shape[1], b_ref.shape[1]
    acc = _dot(a_ref[...], w_ref[0:da, :])
    acc += _dot(b_ref[...], w_ref[da:da + db, :])
    acc += _dot(c_ref[...], w_ref[da + db:, :])
    o_ref[...] = x_ref[...] + acc


def _outproj(x2, ma, mb, mc, w_out, layer, tm=1024):
    t, d = x2.shape
    da, db, dc = ma.shape[1], mb.shape[1], mc.shape[1]
    return pl.pallas_call(
        _outproj_kernel,
        out_shape=jax.ShapeDtypeStruct((t, d), F32),
        grid=(t // tm,),
        in_specs=[pl.BlockSpec((tm, d), lambda i: (i, 0)),
                  pl.BlockSpec((tm, da), lambda i: (i, 0)),
                  pl.BlockSpec((tm, db), lambda i: (i, 0)),
                  pl.BlockSpec((tm, dc), lambda i: (i, 0)),
                  pl.BlockSpec((None, da + db + dc, d), lambda i: (layer, 0, 0), pipeline_mode=pl.Buffered(1))],
        out_specs=pl.BlockSpec((tm, d), lambda i: (i, 0)),
        compiler_params=_cparams(("parallel",)),
        name="outproj",
    )(x2, ma, mb, mc, w_out)


def _mlp_kernel(x0_ref, xn_ref, g_ref, wu_ref, wd_ref, o_ref, h_ref, keep_ref):
    i, k = pl.program_id(0), pl.program_id(1)
    part = xn_ref.shape[0] // pl.num_programs(1)
    cur = i % 2

    @pl.when((i == 0) & (k == 0))
    def _():
        keep_ref[0] = x0_ref[...]
        h_ref[0] = _rms(x0_ref[...], g_ref[...]).astype(BF16)

    @pl.when(k == 0)
    def _():
        o_ref[...] = keep_ref[cur]

    a = jnp.maximum(_dot(h_ref[cur], wu_ref[...]), 0.0)
    o_ref[...] += _dot((a * a).astype(BF16), wd_ref[...])

    rows = pl.ds(pl.multiple_of(k * part, part), part)
    nxt = xn_ref[rows, :]
    keep_ref[1 - cur, rows, :] = nxt
    h_ref[1 - cur, rows, :] = _rms(nxt, g_ref[...]).astype(BF16)


def _mlp(x2, gain, w_up, w_down, layer, tm=512, tf=1024):
    t, d = x2.shape
    f = w_up.shape[-1]
    last = t // tm - 1
    return pl.pallas_call(
        _mlp_kernel,
        out_shape=jax.ShapeDtypeStruct((t, d), F32),
        grid=(t // tm, f // tf),
        in_specs=[pl.BlockSpec((tm, d), lambda i, k: (0, 0), pipeline_mode=pl.Buffered(1)),
                  pl.BlockSpec((tm, d), lambda i, k: (jnp.minimum(i + 1, last), 0)),
                  pl.BlockSpec((None, 1, d), lambda i, k: (layer, 0, 0)),
                  pl.BlockSpec((None, d, tf), lambda i, k: (layer, 0, k)),
                  pl.BlockSpec((None, tf, d), lambda i, k: (layer, k, 0))],
        out_specs=pl.BlockSpec((tm, d), lambda i, k: (i, 0)),
        scratch_shapes=[pltpu.VMEM((2, tm, d), BF16), pltpu.VMEM((2, tm, d), F32)],
        compiler_params=_cparams(("arbitrary", "arbitrary")),
        name="mlp",
    )(x2, x2, gain, w_up, w_down)


def _sgu_kernel(p_ref, lg_ref, lb_ref, w_ref, b_ref, o_ref):
    ts = p_ref.shape[1]
    nb = N_HEADS_B
    v = [jax.nn.gelu(p_ref[nb + g]) for g in range(nb)]
    width = float(nb * LANES)
    mu = sum(jnp.sum(vg, axis=-1, keepdims=True) for vg in v) / width
    var = sum(jnp.sum(jnp.square(vg - mu), axis=-1, keepdims=True) for vg in v) / width
    inv = lax.rsqrt(var + NORM_EPS)
    row = lax.broadcasted_iota(jnp.int32, (SGU_CHUNK, SGU_CHUNK), 0)
    col = lax.broadcasted_iota(jnp.int32, (SGU_CHUNK, SGU_CHUNK), 1)
    causal = col <= row
    for g in range(nb):
        vn = ((v[g] - mu) * inv * lg_ref[g:g + 1, :] + lb_ref[g:g + 1, :]).astype(BF16)
        wg = jnp.where(causal, w_ref[g], 0.0).astype(BF16)
        for c in range(ts // SGU_CHUNK):
            rows = slice(c * SGU_CHUNK, (c + 1) * SGU_CHUNK)
            mixed = _dot(wg, vn[rows]) + b_ref[g]
            u = jax.nn.gelu(p_ref[g, rows, :])
            o_ref[rows, g * LANES:(g + 1) * LANES] = (u * mixed).astype(BF16)


def _sgu(p4, ln_g, ln_b, w_s, b_s, layer, ts=512):
    _, b, s, _ = p4.shape
    nb = N_HEADS_B
    return pl.pallas_call(
        _sgu_kernel,
        out_shape=jax.ShapeDtypeStruct((b, s, nb * LANES), BF16),
        grid=(b, s // ts),
        in_specs=[pl.BlockSpec((2 * nb, None, ts, LANES), lambda i, j: (G_UB // (2 * nb), i, j, 0)),
                  pl.BlockSpec((None, nb, LANES), lambda i, j: (layer, 0, 0)),
                  pl.BlockSpec((None, nb, LANES), lambda i, j: (layer, 0, 0)),
                  pl.BlockSpec((None, nb, SGU_CHUNK, SGU_CHUNK), lambda i, j: (layer, 0, 0, 0)),
                  pl.BlockSpec((None, nb, SGU_CHUNK, LANES), lambda i, j: (layer, 0, 0, 0))],
        out_specs=pl.BlockSpec((None, ts, nb * LANES), lambda i, j: (i, j, 0)),
        compiler_params=_cparams(("parallel", "parallel")),
        name="sgu",
    )(p4, ln_g, ln_b, w_s, b_s)


def _gdn_kernel(a_ref, sm_ref, cw_ref, ab_ref, ng_ref, o_ref,
                xbuf, q_s, k_s, v_s, beta_s, gcum_s, u_s, w_s, qg_s, kd_s, a_s, st_ref, *, batch):
    ts = a_ref.shape[1]
    ck = GDN_CHUNK
    nh = N_HEADS_A
    pad = 8

    @pl.when(pl.program_id(1) == 0)
    def _():
        xbuf[:, 0:pad, :] = jnp.zeros((3 * nh, pad, LANES), F32)
        st_ref[...] = jnp.zeros_like(st_ref)

    xbuf[:, pad:pad + ts, :] = a_ref[0:3 * nh]
    dests = (q_s, k_s, v_s)
    for grp in range(3 * nh):
        acc = None
        for j in range(CONV_WIDTH):
            lo = pad - (CONV_WIDTH - 1) + j
            term = xbuf[grp, lo:lo + ts, :] * cw_ref[j * 3 * nh + grp:j * 3 * nh + grp + 1, :]
            acc = term if acc is None else acc + term
        y = acc * jax.nn.sigmoid(acc)
        kind, h = divmod(grp, nh)
        if kind < 2:
            y = y * lax.rsqrt(jnp.sum(y * y, axis=-1, keepdims=True) + NORM_EPS)
        dests[kind][h] = y
    xbuf[:, 0:pad, :] = xbuf[:, ts:ts + pad, :]

    sm = sm_ref[...]
    beta_s[...] = jax.nn.sigmoid(sm)
    z = sm + ab_ref[1:2, :]
    softplus = jnp.maximum(z, 0.0) + jnp.log1p(jnp.exp(-jnp.abs(z)))
    g_all = -jnp.exp(ab_ref[0:1, :]) * softplus

    ii = lax.broadcasted_iota(jnp.int32, (ck, ck), 0)
    jj = lax.broadcasted_iota(jnp.int32, (ck, ck), 1)
    incl = jj <= ii
    strict = jj < ii
    tri = jnp.where(incl, 1.0, 0.0).astype(F32)
    for c in range(ts // ck):
        rows = slice(c * ck, (c + 1) * ck)
        gcum_s[rows, :] = _dot(tri, g_all[rows], precision=lax.Precision.HIGHEST)

    scale = HEAD_DIM ** -0.5
    gain = ng_ref[...]
    bf = lambda m: m.astype(BF16)
    same = lambda size: (ii // size) == (jj // size)
    n_chunks = ts // ck
    rows_of = lambda c: slice(c * ck, (c + 1) * ck)
    dcol = lambda h: slice(LANE_DECAY + h, LANE_DECAY + h + 1)

    for c0 in range(0, n_chunks, batch):
        gc = {c: gcum_s[rows_of(c), :] for c in range(c0, c0 + batch)}
        gct = {c: gc[c].T for c in gc}
        items = [(c, h) for c in range(c0, c0 + batch) for h in range(nh)]
        kk, qk = {}, {}
        for c, h in items:
            g_col = gc[c][:, dcol(h)]
            q = q_s[h, rows_of(c), :] * scale
            k = k_s[h, rows_of(c), :]
            kbf = bf(k)
            kk[c, h] = _dot_nt(bf(k * beta_s[rows_of(c), LANE_BETA + h:LANE_BETA + h + 1]), kbf)
            qk[c, h] = _dot_nt(bf(q), kbf)
            qg_s[h, rows_of(c), :] = bf(q * jnp.exp(g_col))
            kd_s[h, rows_of(c), :] = bf(k * jnp.exp(gc[c][ck - 1:ck, dcol(h)] - g_col))
        lower, diag, p1 = {}, {}, {}
        for it in items:
            c, h = it
            decay = jnp.exp(jnp.where(incl, gc[c][:, dcol(h)] - gct[c][dcol(h), :], NEG_BIG))
            lower[it] = jnp.where(strict, kk[it] * decay, 0.0)
            a_s[h, rows_of(c), :] = bf(jnp.where(incl, qk[it] * decay, 0.0))
            diag[it] = jnp.where(same(8), lower[it], 0.0)
            p1[it] = _dot(bf(diag[it]), bf(diag[it]))
        p2 = {it: _dot(bf(p1[it]), bf(p1[it])) for it in items}
        dp1 = {it: _dot(bf(diag[it]), bf(p1[it])) for it in items}
        na = {it: p1[it] - diag[it] - dp1[it] for it in items}
        nap2 = {it: _dot(bf(na[it]), bf(p2[it])) for it in items}
        n = {it: na[it] + p2[it] + nap2[it] for it in items}
        size = 8
        while size < ck:
            off = same(2 * size) & ~same(size)
            cm = {it: jnp.where(off, lower[it], 0.0) for it in items}
            y = {it: cm[it] + _dot(bf(n[it]), bf(cm[it])) for it in items}
            yn = {it: _dot(bf(y[it]), bf(n[it])) for it in items}
            n = {it: n[it] - (y[it] + yn[it]) for it in items}
            size *= 2
        for it in items:
            c, h = it
            beta = beta_s[rows_of(c), LANE_BETA + h:LANE_BETA + h + 1]
            nb = bf(n[it])
            vb = v_s[h, rows_of(c), :] * beta
            kw = k_s[h, rows_of(c), :] * (beta * jnp.exp(gc[c][:, dcol(h)]))
            u_s[h, rows_of(c), :] = vb + _dot(nb, bf(vb))
            w_s[h, rows_of(c), :] = bf(kw + _dot(nb, bf(kw)))

    heads = range(nh)
    for c in range(n_chunks):
        rows = rows_of(c)
        state = [st_ref[h] for h in heads]
        sb = [bf(s) for s in state]
        ws = [_dot(w_s[h, rows, :], sb[h]) for h in heads]
        qs = [_dot(qg_s[h, rows, :], sb[h]) for h in heads]
        vnb = [bf(u_s[h, rows, :] - ws[h]) for h in heads]
        kv = [_dot_tn(kd_s[h, rows, :], vnb[h]) for h in heads]
        av = [_dot(a_s[h, rows, :], vnb[h]) for h in heads]
        for h in heads:
            g_last = gcum_s[(c + 1) * ck - 1:(c + 1) * ck, dcol(h)]
            st_ref[h] = state[h] * jnp.exp(g_last) + kv[h]
            zg = a_ref[3 * nh + h, rows, :]
            out = _rms(qs[h] + av[h], gain) * (zg * jax.nn.sigmoid(zg))
            o_ref[rows, h * LANES:(h + 1) * LANES] = out.astype(BF16)


def _gdn(p4, conv_w, ab, norm_g, layer, ts=512, batch=4):
    _, b, s, _ = p4.shape
    nh = N_HEADS_A
    return pl.pallas_call(
        functools.partial(_gdn_kernel, batch=batch),
        out_shape=jax.ShapeDtypeStruct((b, s, nh * LANES), BF16),
        grid=(b, s // ts),
        in_specs=[pl.BlockSpec((4 * nh, None, ts, LANES), lambda i, j: (0, i, j, 0)),
                  pl.BlockSpec((None, None, ts, LANES), lambda i, j: (G_SMALL, i, j, 0)),
                  pl.BlockSpec((None, CONV_WIDTH * 3 * nh, LANES), lambda i, j: (layer, 0, 0)),
                  pl.BlockSpec((None, 2, LANES), lambda i, j: (layer, 0, 0)),
                  pl.BlockSpec((None, 1, LANES), lambda i, j: (layer, 0, 0))],
        out_specs=pl.BlockSpec((None, ts, nh * LANES), lambda i, j: (i, j, 0)),
        scratch_shapes=[pltpu.VMEM((3 * nh, ts + 8, LANES), F32),
                        pltpu.VMEM((nh, ts, LANES), F32),
                        pltpu.VMEM((nh, ts, LANES), F32),
                        pltpu.VMEM((nh, ts, LANES), F32),
                        pltpu.VMEM((ts, LANES), F32),
                        pltpu.VMEM((ts, LANES), F32),
                        pltpu.VMEM((nh, ts, HEAD_DIM), F32),
                        pltpu.VMEM((nh, ts, HEAD_DIM), BF16),
                        pltpu.VMEM((nh, ts, HEAD_DIM), BF16),
                        pltpu.VMEM((nh, ts, HEAD_DIM), BF16),
                        pltpu.VMEM((nh, ts, GDN_CHUNK), BF16),
                        pltpu.VMEM((nh, HEAD_DIM, HEAD_DIM), F32)],
        compiler_params=_cparams(("parallel", "arbitrary")),
        name="gdn",
    )(p4, p4, conv_w, ab, norm_g)


def _nsa_prep_kernel(kc2_ref, vc2_ref, ksl_ref, vsl_ref, kwn_ref, vwn_ref,
                     w1_ref, w2_ref, pos_ref, kg_ref, bm_ref, one_ref,
                     kc_o, vc_o, ks_o, vs_o, kw_o, vw_o):
    nc = kc_o.shape[0]
    half = CMP_STRIDE * LANES
    kg = kg_ref[...]
    for kind, (src, dst) in enumerate(((kc2_ref, kc_o), (vc2_ref, vc_o))):
        t2 = jnp.concatenate([src[pl.ds(l, nc, stride=CMP_STRIDE), :].astype(BF16)
                              for l in range(CMP_STRIDE)], axis=1)
        first = _dot(t2, w1_ref[kind, 0:half, :])
        second = _dot(t2, w1_ref[kind, half:2 * half, :])
        const = _dot(pos_ref[kind], w1_ref[kind])[0:1, :]
        pre = first + pltpu.roll(second, nc - 1, 0) + const
        y = _dot(jax.nn.gelu(pre).astype(BF16), w2_ref[kind])
        if kind == 0:
            y = _rms(y, kg)
        dst[...] = y.astype(BF16)
    front = ks_o.shape[0] - ksl_ref.shape[0]
    for dst, val in ((ks_o, _rms(ksl_ref[...], kg)), (kw_o, _rms(kwn_ref[...], kg)),
                     (vs_o, vsl_ref[...]), (vw_o, vwn_ref[...])):
        dst[0:front, 0:LANES] = jnp.zeros((front, LANES), BF16)
        dst[front:, 0:LANES] = val.astype(BF16)
    ks_o[:, LANES:] = bm_ref[...]
    vs_o[:, LANES:] = one_ref[...]
    vw_o[:, LANES:] = one_ref[...]


def _nsa_prep(p4, w1, w2, pos, k_norm, block_mask, ones_col, layer):
    ng, b, s, _ = p4.shape
    padded = s + WINDOW
    nc = s // CMP_STRIDE
    half = CMP_STRIDE * LANES
    grp = lambda base: (lambda i, g: (base + g, i, 0, 0))
    small = jax.ShapeDtypeStruct((b, KV_HEADS, nc, LANES), BF16)
    full = jax.ShapeDtypeStruct((b, KV_HEADS, padded, LANES), BF16)
    wide = jax.ShapeDtypeStruct((b, KV_HEADS, padded, 2 * LANES), BF16)
    out_small = pl.BlockSpec((None, None, nc, LANES), lambda i, g: (i, g, 0, 0))
    out_full = pl.BlockSpec((None, None, padded, LANES), lambda i, g: (i, g, 0, 0))
    out_wide = pl.BlockSpec((None, None, padded, 2 * LANES), lambda i, g: (i, g, 0, 0))
    const = pl.BlockSpec((padded, LANES), lambda i, g: (0, 0))
    return pl.pallas_call(
        _nsa_prep_kernel,
        out_shape=(small, small, wide, wide, full, wide),
        grid=(b, KV_HEADS),
        in_specs=[pl.BlockSpec((None, None, s, LANES), grp(G_KCMP)),
                  pl.BlockSpec((None, None, s, LANES), grp(G_VCMP)),
                  pl.BlockSpec((None, None, s, LANES), grp(G_KSLC)),
                  pl.BlockSpec((None, None, s, LANES), grp(G_VSLC)),
                  pl.BlockSpec((None, None, s, LANES), grp(G_KWIN)),
                  pl.BlockSpec((None, None, s, LANES), grp(G_VWIN)),
                  pl.BlockSpec((None, 2, 2 * half, LANES), lambda i, g: (layer, 0, 0, 0)),
                  pl.BlockSpec((None, 2, LANES, LANES), lambda i, g: (layer, 0, 0, 0)),
                  pl.BlockSpec((None, 2, 8, 2 * half), lambda i, g: (layer, 0, 0, 0)),
                  pl.BlockSpec((None, 1, LANES), lambda i, g: (layer, 0, 0)),
                  const, const],
        out_specs=(out_small, out_small, out_wide, out_wide, out_full, out_wide),
        compiler_params=_cparams(("parallel", "parallel")),
        name="nsa_prep",
    )(p4, p4, p4, p4, p4, p4, w1, w2, pos, k_norm, block_mask, ones_col)


def _interleave(fragments):
    live = list(fragments)
    while live:
        for fragment in list(live):
            try:
                next(fragment)
            except StopIteration:
                live.remove(fragment)


def _nsa_pair_kernel(q_ref, gate_ref, kc_ref, vc_ref, ks_ref, vs_ref, kw_ref, vw_ref,
                     bc_ref, bs_ref, bw_ref, ov_ref, qg_ref, o_ref, sc_ref, *, n_cmp, n_sel):
    tq = q_ref.shape[1]
    r3 = Q_PER_KV
    i = pl.program_id(1)
    nc = kc_ref.shape[1]
    rows = r3 * tq
    band = (N_WIN_PREV + 1) * tq
    front = N_WIN_PREV * tq
    far_keys = sc_ref.shape[-1]
    boff = pl.multiple_of(i * tq, tq)
    n_far = jnp.maximum(i - 1, 0) // (far_keys // tq)

    qq = lax.broadcasted_iota(jnp.int32, (tq, tq), 0)
    kk = lax.broadcasted_iota(jnp.int32, (tq, tq), 1)
    q_b = lax.broadcasted_iota(jnp.int32, (tq, band), 0)
    k_b = lax.broadcasted_iota(jnp.int32, (tq, band), 1)
    qpos_c = i * tq + lax.broadcasted_iota(jnp.int32, (tq, nc), 0)
    n_idx = lax.broadcasted_iota(jnp.int32, (tq, nc), 1)
    mask_c = ((qpos_c >= n_idx * CMP_STRIDE + (CMP_LEN - 1)) & (n_idx < n_cmp))[None]
    ok_w = (k_b > q_b) & (k_b <= q_b + front) & (k_b >= front - i * tq)
    cur = (i * tq + qq) // SEL_LEN
    forced = (kk == 0) | (kk == cur) | (kk == cur - 1)
    qg = qg_ref[...]

    def lane_fold(x, op):
        out = x[..., 0:tq]
        for c in range(1, x.shape[-1] // tq):
            out = op(out, x[..., c * tq:(c + 1) * tq])
        return out

    def normalise(acc):
        den = jnp.maximum(acc[:, HEAD_DIM:HEAD_DIM + 1], 1e-30)
        return (acc[:, 0:HEAD_DIM] / den).reshape(r3, tq, HEAD_DIM)

    kept = [{} for _ in range(KV_HEADS)]

    def before_selection(g):
        n_blocks = n_sel
        heads = slice(g * r3, (g + 1) * r3)
        qf = jnp.concatenate([_rms(q_ref[g * r3 + r], qg) * (HEAD_DIM ** -0.5) for r in range(r3)], axis=0)
        qb = qf.astype(BF16)
        yield
        s_c = _dot_nt(qb, kc_ref[g]).reshape(r3, tq, nc) + bc_ref[heads]
        s_w = (_dot_nt(qb, kw_ref[g, pl.ds(boff, band), :]).reshape(r3, tq, band) + bw_ref[heads]
               + jnp.where(ok_w, 0.0, NEG_BIG)[None])
        yield
        s_c = jnp.where(mask_c, s_c, NEG_BIG)
        m_c = jnp.max(s_c, axis=-1, keepdims=True)
        p_c = jnp.where(mask_c, jnp.exp(s_c - m_c), 0.0)
        p_c = p_c / jnp.maximum(jnp.sum(p_c, axis=-1, keepdims=True), 1e-30)
        yield
        o_c = _dot(p_c.reshape(rows, nc).astype(BF16), vc_ref[g]).reshape(r3, tq, HEAD_DIM)
        p_sum = p_c[0] + p_c[1] + p_c[2]
        p_hi = p_sum.astype(BF16)
        p_lo = (p_sum - p_hi.astype(F32)).astype(BF16)
        importance = _dot(p_hi, ov_ref[...]) + _dot(p_lo, ov_ref[...])
        yield
        p_w = jnp.exp(s_w - jnp.max(s_w, axis=-1, keepdims=True))
        o_w = normalise(_dot(p_w.reshape(rows, band).astype(BF16), vw_ref[g, pl.ds(boff, band), :]))
        yield
        score = jnp.where(forced, SEL_FORCE, jnp.where(kk <= cur, importance, NEG_BIG))
        st = score.T
        sub = 8
        slabs = [st[v * sub:(v + 1) * sub, :] for v in range((n_blocks + sub - 1) // sub)]
        keep = [slab > 0.5 * NEG_BIG for slab in slabs]
        if n_blocks > SEL_TOPK:
            blk = lax.broadcasted_iota(jnp.int32, (sub, tq), 0)
            ranks = [jnp.zeros((sub, tq), F32) for _ in slabs]
            for b in range(n_blocks):
                row = st[b:b + 1, :]
                for v, slab in enumerate(slabs):
                    if v * sub > b:
                        beats = row >= slab
                    elif (v + 1) * sub - 1 <= b:
                        beats = row > slab
                    else:
                        beats = (row > slab) | ((row == slab) & (blk + v * sub > b))
                    ranks[v] = ranks[v] + jnp.where(beats, 1.0, 0.0)
                if b % 8 == 7:
                    yield
            keep = [k & (r < float(SEL_TOPK)) for k, r in zip(keep, ranks)]
        drop_t = jnp.concatenate([jnp.where(k, 0.0, 1.0) for k in keep]
                                 + [jnp.ones((tq - sub * len(slabs), tq), F32)], axis=0)
        drop = drop_t.T.astype(BF16)
        q_sel = jnp.concatenate([qb, jnp.concatenate([drop] * r3, axis=0)], axis=1)
        gates = jax.nn.sigmoid(gate_ref[g])
        col = lambda branch, r: gates[:, LANE_GATE + branch * r3 + r:LANE_GATE + branch * r3 + r + 1]
        kept[g].update(q_sel=q_sel, gate_s=[col(1, r) for r in range(r3)],
                       partial=[col(0, r) * o_c[r] + col(2, r) * o_w[r] for r in range(r3)])


    def half_softmax(g, pieces, with_band, nf, result):
        q_sel = kept[g]["q_sel"]
        keys_of = lambda c, lo, hi: slice(front + c * far_keys + lo, front + c * far_keys + hi)
        mx = None
        if with_band:
            first_near = front - (i * tq - nf * far_keys)
            ok_s = (k_b <= q_b + front) & (k_b >= first_near)
            s_b = (_dot_nt(q_sel, ks_ref[g, pl.ds(boff, band), :]).reshape(r3, tq, band)
                   + bs_ref[g * r3:(g + 1) * r3] + jnp.where(ok_s, 0.0, NEG_BIG)[None])
            mx = lane_fold(s_b, jnp.maximum)
            yield
        for c, lo, hi in pieces:
            s = _dot_nt(q_sel, ks_ref[g, keys_of(c, lo, hi), :]).reshape(r3, tq, hi - lo)
            sc_ref[g, c, :, :, lo:hi] = s
            fold = lane_fold(s, jnp.maximum)
            mx = fold if mx is None else jnp.maximum(mx, fold)
            yield
        m = jnp.max(mx, axis=-1, keepdims=True)
        acc = None
        if with_band:
            acc = _dot(jnp.exp(s_b - m).reshape(rows, band).astype(BF16), vs_ref[g, pl.ds(boff, band), :])
            yield
        for c, lo, hi in pieces:
            p = jnp.exp(sc_ref[g, c, :, :, lo:hi] - m).reshape(rows, hi - lo).astype(BF16)
            part = _dot(p, vs_ref[g, keys_of(c, lo, hi), :])
            acc = part if acc is None else acc + part
            yield
        result.append((m.reshape(rows, 1), acc))

    def selected(nf):
        whole = lambda cs: [(c, 0, far_keys) for c in cs]
        if nf == 0:
            first, second = None, []
        elif nf % 2:
            first, second = whole(range((nf + 1) // 2)), whole(range((nf + 1) // 2, nf))
        else:
            mid = nf // 2
            first = whole(range(mid)) + [(mid, 0, far_keys // 2)]
            second = [(mid, far_keys // 2, far_keys)] + whole(range(mid + 1, nf))
        halves = [([], []) for _ in range(KV_HEADS)]
        fragments = []
        for g in range(KV_HEADS):
            fragments.append(half_softmax(g, second, True, nf, halves[g][1]))
            if first is not None:
                fragments.append(half_softmax(g, first, False, nf, halves[g][0]))
        _interleave(fragments)
        for g in range(KV_HEADS):
            (m_b, acc_b), = halves[g][1]
            if first is None:
                o_s = normalise(acc_b)
            else:
                (m_a, acc_a), = halves[g][0]
                m_s = jnp.maximum(m_a, m_b)
                o_s = normalise(acc_a * jnp.exp(m_a - m_s) + acc_b * jnp.exp(m_b - m_s))
            for r in range(r3):
                out = kept[g]["partial"][r] + kept[g]["gate_s"][r] * o_s[r]
                o_ref[:, (g * r3 + r) * HEAD_DIM:(g * r3 + r + 1) * HEAD_DIM] = out.astype(BF16)

    _interleave(before_selection(g) for g in range(KV_HEADS))
    for nf in range(sc_ref.shape[1]):
        pl.when(n_far == nf)(functools.partial(selected, nf))


def _nsa_pair(p4, prep, bias_c, bias_sel, bias_win, overlap, q_norm, layer, tq=WIN_BLOCK, far_keys=512):
    _, b, s, _ = p4.shape
    kc, vc, ks, vs, kw, vw = prep
    nc = kc.shape[2]
    nq = N_HEADS_C
    n_sel = s // SEL_LEN
    n_cmp = (s - CMP_LEN) // CMP_STRIDE + 1
    band = (N_WIN_PREV + 1) * tq
    padded = ks.shape[2]
    per_batch = lambda *tail: pl.BlockSpec((None, KV_HEADS) + tail, lambda i, j: (i, 0, 0, 0))
    resident = lambda *tail: pl.BlockSpec((None, KV_HEADS) + tail, lambda i, j: (i, 0, 0, 0),
                                          pipeline_mode=pl.Buffered(1))
    fixed = lambda *shape: pl.BlockSpec(shape, lambda i, j: (0,) * len(shape), pipeline_mode=pl.Buffered(1))
    return pl.pallas_call(
        functools.partial(_nsa_pair_kernel, n_cmp=n_cmp, n_sel=n_sel),
        out_shape=jax.ShapeDtypeStruct((b, s, nq * HEAD_DIM), BF16),
        grid=(b, s // tq),
        in_specs=[pl.BlockSpec((nq, None, tq, LANES), lambda i, j: (G_QC // nq, i, j, 0)),
                  pl.BlockSpec((KV_HEADS, None, tq, LANES), lambda i, j: (G_SMALL // KV_HEADS, i, j, 0)),
                  per_batch(nc, LANES), per_batch(nc, LANES),
                  per_batch(padded, 2 * LANES), per_batch(padded, 2 * LANES),
                  per_batch(padded, LANES), per_batch(padded, 2 * LANES),
                  pl.BlockSpec((nq, tq, nc), lambda i, j: (0, j, 0)),
                  fixed(nq, tq, band), fixed(nq, tq, band), fixed(nc, LANES),
                  pl.BlockSpec((None, 1, LANES), lambda i, j: (layer, 0, 0))],
        out_specs=pl.BlockSpec((None, tq, nq * HEAD_DIM), lambda i, j: (i, j, 0)),
        scratch_shapes=[pltpu.VMEM((KV_HEADS, s // far_keys, Q_PER_KV, tq, far_keys), F32)],
        compiler_params=_cparams(("parallel", "arbitrary"), vmem_limit=NSA_VMEM_LIMIT),
        name="nsa",
    )(p4, p4, kc, vc, ks, vs, kw, vw, bias_c, bias_sel, bias_win, overlap, q_norm)


def _t5_bucket(dist):
    n = jnp.maximum(dist, 0)
    max_exact = RPB_BUCKETS // 2
    log_ratio = jnp.log(jnp.maximum(n, 1).astype(F32) / max_exact) / math.log(RPB_MAX_DIST / max_exact)
    large = jnp.minimum(max_exact + (log_ratio * (RPB_BUCKETS - max_exact)).astype(jnp.int32), RPB_BUCKETS - 1)
    return jnp.where(n < max_exact, n, large)


def _bias_tables(rel_bias, s, tq):
    table = rel_bias.astype(F32)
    buckets = jnp.arange(RPB_BUCKETS, dtype=jnp.int32)[:, None, None]

    def look(dist):
        onehot = (_t5_bucket(dist)[None] == buckets).astype(F32)
        return jnp.einsum("nh,nqk->hqk", table, onehot, precision=lax.Precision.HIGHEST)

    nc = s // CMP_STRIDE
    pos = jnp.arange(s, dtype=jnp.int32)
    cmp_end = jnp.arange(nc, dtype=jnp.int32) * CMP_STRIDE + (CMP_LEN - 1)
    bias_c = look(pos[:, None] - cmp_end[None, :])
    q = jnp.arange(tq, dtype=jnp.int32)[:, None]
    kb = jnp.arange((N_WIN_PREV + 1) * tq, dtype=jnp.int32)[None, :]
    bias_win = look(N_WIN_PREV * tq + q - kb)
    bias_sel = bias_win - table[RPB_BUCKETS - 1][:, None, None]
    return bias_c, bias_sel, bias_win


def _selection_constants(s, tq):
    nc = s // CMP_STRIDE
    n_cmp = (s - CMP_LEN) // CMP_STRIDE + 1
    n_sel = s // SEL_LEN
    cmp_start = np.arange(nc) * CMP_STRIDE
    sel_start = np.arange(LANES) * SEL_LEN
    overlap = ((cmp_start[:, None] < sel_start[None, :] + SEL_LEN)
               & (cmp_start[:, None] + CMP_LEN > sel_start[None, :])
               & (np.arange(nc)[:, None] < n_cmp) & (np.arange(LANES)[None, :] < n_sel))
    key_pos = np.arange(-WINDOW, s)[:, None]
    in_block = (np.arange(LANES)[None, :] == (key_pos // SEL_LEN)) & (key_pos >= 0)
    block_mask = jnp.where(jnp.asarray(in_block), NEG_BIG, 0.0).astype(BF16)
    ones_col = jnp.asarray((np.arange(LANES)[None, :] == 0) & (key_pos >= 0), BF16)
    return jnp.asarray(overlap, BF16), block_mask, ones_col


def _arrange_w_in(w_in):
    depth, d, _ = w_in.shape
    w_in = w_in.astype(BF16)
    da, db, dc, dkv = N_HEADS_A * 128, N_HEADS_B * 128, N_HEADS_C * 128, KV_HEADS * 128
    o_ba = 4 * da
    o_aa = o_ba + N_HEADS_A
    o_ub = o_aa + N_HEADS_A
    o_qc = o_ub + 2 * db
    o_kc = o_qc + dc
    o_gc = o_kc + 6 * dkv
    sl = lambda lo, n: w_in[:, :, lo:lo + n]
    gate = w_in[:, :, o_gc:o_gc + 3 * N_HEADS_C].reshape(depth, d, 3, KV_HEADS, Q_PER_KV)
    zeros = lambda n: jnp.zeros((depth, d, n), w_in.dtype)
    small0 = jnp.concatenate([sl(o_ba, 2 * N_HEADS_A), gate[:, :, :, 0, :].reshape(depth, d, 3 * Q_PER_KV),
                              zeros(LANES - 2 * N_HEADS_A - 3 * Q_PER_KV)], axis=-1)
    small1 = jnp.concatenate([zeros(LANE_GATE), gate[:, :, :, 1, :].reshape(depth, d, 3 * Q_PER_KV),
                              zeros(LANES - LANE_GATE - 3 * Q_PER_KV)], axis=-1)
    parts = [sl(0, 4 * da),
             sl(o_qc, dc),
             sl(o_kc, dkv),
             sl(o_ub, 2 * db),
             sl(o_kc + dkv, 5 * dkv),
             small0, small1]
    return jnp.concatenate(parts, axis=-1).astype(BF16)


def kernel(x, attn_norm, w_in, conv_a, a_log, dt_bias, gdn_norm, sgu_ln_g, sgu_ln_b, sgu_w, sgu_b,
           nsa_q_norm, nsa_k_norm, cmp_pos, cmp_w1, cmp_w2, rel_bias, w_out, mlp_norm, w_up, w_down):
    b, s, d = x.shape
    depth = w_in.shape[0]
    t = b * s
    tq = 128

    w_in_r = _arrange_w_in(w_in)
    w_out_b = w_out.astype(BF16)
    w_up_b = w_up.astype(BF16)
    w_down_b = w_down.astype(BF16)
    attn_g = attn_norm.reshape(depth, 1, d)
    mlp_g = mlp_norm.reshape(depth, 1, d)
    conv_r = conv_a.reshape(depth, CONV_WIDTH * 3 * N_HEADS_A, LANES)
    pad_to = lambda v, lo: jnp.pad(v, ((0, 0), (lo, LANES - lo - v.shape[1])))
    ab = jnp.stack([pad_to(a_log, LANE_DECAY), pad_to(dt_bias, LANE_DECAY)], axis=1)
    gdn_g = gdn_norm.reshape(depth, 1, LANES)
    ln_g = sgu_ln_g.reshape(depth, N_HEADS_B, LANES)
    ln_b = sgu_ln_b.reshape(depth, N_HEADS_B, LANES)
    sgu_bias = jnp.broadcast_to(sgu_b[..., None], sgu_b.shape + (LANES,))
    q_g = nsa_q_norm.reshape(depth, 1, LANES)
    k_g = nsa_k_norm.reshape(depth, 1, LANES)
    w1_b = cmp_w1.astype(BF16)
    w2_b = cmp_w2.astype(BF16)
    pos_b = jnp.broadcast_to(cmp_pos.reshape(depth, 2, 1, CMP_LEN * LANES),
                             (depth, 2, 8, CMP_LEN * LANES)).astype(BF16)
    bias_c, bias_sel, bias_win = _bias_tables(rel_bias, s, tq)
    overlap, block_mask, ones_col = _selection_constants(s, tq)

    x2 = x.reshape(t, d)
    for layer in range(depth):
        p = _inproj(x2, attn_g, w_in_r, layer)
        p4 = p.reshape(N_GROUPS, b, s, LANES)
        mix_a = _gdn(p4, conv_r, ab, gdn_g, layer)
        mix_b = _sgu(p4, ln_g, ln_b, sgu_w, sgu_bias, layer)
        prep = _nsa_prep(p4, w1_b, w2_b, pos_b, k_g, block_mask, ones_col, layer)
        mix_c = _nsa_pair(p4, prep, bias_c, bias_sel, bias_win, overlap, q_g, layer, tq=tq)
        x2 = _outproj(x2, mix_a.reshape(t, -1), mix_b.reshape(t, -1), mix_c.reshape(t, -1), w_out_b, layer)
        x2 = _mlp(x2, mlp_g, w_up_b, w_down_b, layer)
    return x2.reshape(b, s, d)
```
